```python
import jax, jax.numpy as jnp
from jax import lax
import numpy as np

D_MODEL = 2048
BATCH = 4
SEQ = 8192
DEPTH = 4

GRID_W = 64
CTX_LEN = 256
N_MIXERS = 4
HEAD_DIM = 128
N_HEADS = D_MODEL // HEAD_DIM
NA_ROWS = 8
NA_COLS = 16
SWA_KV_HEADS = 2
SWA_WINDOW = 128
BLOCK = 128
GQA_KV_HEADS = 4
ML_HEADS = 4
ML_V_DIM = D_MODEL // ML_HEADS
ML_QK_DIM = ML_V_DIM // 2
ML_CHUNK = 64
ML_FGATE_BIAS = 3.0
D_FF = 5632
CONV_W = 3
ROPE_BASE = 10000.0
NORM_EPS = 1e-6
NEG_INF = -1e30

kernel_name = 'hybrid_diffusion_backbone'


def rms_norm(x, g):
    xf = x.astype(jnp.float32)
    y = xf * lax.rsqrt(jnp.mean(xf * xf, axis=-1, keepdims=True) + NORM_EPS)
    return (y * g.astype(jnp.float32)).astype(x.dtype)


def adaln(cond, w, b):
    return jnp.split(jax.nn.silu(cond) @ w + b, 6, axis=-1)


def modulate(h, shift, scale):
    return h * (1 + scale) + shift


def axial_rope_tables(n_tokens):
    t = jnp.arange(n_tokens)
    row = (t // GRID_W).astype(jnp.float32)
    col = (t % GRID_W).astype(jnp.float32)
    n_freq = HEAD_DIM // 4
    inv = ROPE_BASE ** (-jnp.arange(n_freq, dtype=jnp.float32) / n_freq)
    ang = jnp.stack([row[:, None] * inv, col[:, None] * inv], axis=1)
    return jnp.cos(ang), jnp.sin(ang)


def apply_rope(x, cos, sin):
    B, T, H, Dh = x.shape
    xr = x.astype(jnp.float32).reshape(B, T, H, 2, 2, Dh // 4)
    x1, x2 = xr[..., 0, :], xr[..., 1, :]
    c, s = cos[None, :, None], sin[None, :, None]
    out = jnp.stack([x1 * c - x2 * s, x2 * c + x1 * s], axis=-2)
    return out.reshape(B, T, H, Dh).astype(x.dtype)


def softmax_with_sink(s, sink):
    if sink is None:
        return jax.nn.softmax(s, axis=-1)
    m = jnp.maximum(jnp.max(s, axis=-1, keepdims=True), sink)
    e = jnp.exp(s - m)
    return e / (jnp.sum(e, axis=-1, keepdims=True) + jnp.exp(sink - m))


def multi_source_attention(q, sources, sink=None):
    scale = q.shape[-1] ** -0.5
    scores = []
    for k, _, mask in sources:
        s = jnp.einsum('bqngd,bknd->bngqk', q, k, preferred_element_type=jnp.float32) * scale
        scores.append(s if mask is None else jnp.where(mask, s, NEG_INF))
    p = softmax_with_sink(jnp.concatenate(scores, axis=-1), sink)
    out, start = None, 0
    for (_, v, _), s in zip(sources, scores):
        n = s.shape[-1]
        o = jnp.einsum('bngqk,bknd->bqngd', p[..., start:start + n].astype(v.dtype), v)
        out = o if out is None else out + o
        start += n
    return out


def project_gqa(h, w_qkv, q_g, k_g, n_kv):
    B, T, _ = h.shape
    q, k, v = jnp.split(h @ w_qkv, [N_HEADS * HEAD_DIM, (N_HEADS + n_kv) * HEAD_DIM], axis=-1)
    q = rms_norm(q.reshape(B, T, N_HEADS, HEAD_DIM), q_g)
    k = rms_norm(k.reshape(B, T, n_kv, HEAD_DIM), k_g)
    return q, k, v.reshape(B, T, n_kv, HEAD_DIM)


def group_heads(q, n_kv):
    B, T = q.shape[:2]
    return q.reshape(B, T, n_kv, N_HEADS // n_kv, HEAD_DIM)


def context_self_attention(qc, kc, vc, n_kv, w_o, sink=None):
    B, L = qc.shape[:2]
    o = multi_source_attention(group_heads(qc, n_kv), [(kc, vc, None)], sink)
    return o.reshape(B, L, D_MODEL) @ w_o


def neighbourhood_attention(hx, hc, w_qkv, q_g, k_g, rel_bias, w_o, need_ctx):
    B, S, _ = hx.shape
    rows = S // GRID_W
    kr = min(NA_ROWS, rows)
    q, k, v = project_gqa(hx, w_qkv, q_g, k_g, N_HEADS)
    qc, kc, vc = project_gqa(hc, w_qkv, q_g, k_g, N_HEADS)
    scale = HEAD_DIM ** -0.5
    qg = q.reshape(B, rows, GRID_W, N_HEADS, HEAD_DIM)
    kg = k.reshape(B, rows, GRID_W, N_HEADS, HEAD_DIM)
    vg = v.reshape(B, rows, GRID_W, N_HEADS, HEAD_DIM)
    row_start = jnp.clip(jnp.arange(rows) - kr // 2, 0, rows - kr)
    qcol = jnp.arange(GRID_W)
    col_idx = jnp.clip(qcol - NA_COLS // 2, 0, GRID_W - NA_COLS)[:, None] + jnp.arange(NA_COLS)
    dcol = col_idx - qcol[:, None] + (NA_COLS - 1)
    rb = rel_bias.astype(jnp.float32)
    n_loc = kr * NA_COLS

    def one_row(r):
        r0 = row_start[r]
        k_win = lax.dynamic_slice_in_dim(kg, r0, kr, axis=1)[:, :, col_idx]
        v_win = lax.dynamic_slice_in_dim(vg, r0, kr, axis=1)[:, :, col_idx]
        q_r = lax.dynamic_index_in_dim(qg, r, axis=1, keepdims=False)
        drow = r0 + jnp.arange(kr) - r + (NA_ROWS - 1)
        bias = rb[:, drow[None, :, None], dcol[:, None, :]]
        s_loc = jnp.einsum('bqhd,brqkhd->bhqrk', q_r, k_win, preferred_element_type=jnp.float32) * scale + bias
        s_ctx = jnp.einsum('bqhd,blhd->bhql', q_r, kc, preferred_element_type=jnp.float32) * scale
        p = jax.nn.softmax(jnp.concatenate([s_loc.reshape(B, N_HEADS, GRID_W, n_loc), s_ctx], axis=-1), axis=-1)
        p_loc = p[..., :n_loc].reshape(B, N_HEADS, GRID_W, kr, NA_COLS).astype(v.dtype)
        return (jnp.einsum('bhqrk,brqkhd->bqhd', p_loc, v_win)
                + jnp.einsum('bhql,blhd->bqhd', p[..., n_loc:].astype(v.dtype), vc))

    o = lax.map(one_row, jnp.arange(rows))
    out_x = jnp.moveaxis(o, 0, 1).reshape(B, S, D_MODEL) @ w_o
    out_c = context_self_attention(qc, kc, vc, N_HEADS, w_o) if need_ctx else None
    return out_x, out_c


def sliding_window_attention(hx, hc, w_qkv, q_g, k_g, sinks, w_o, cos, sin, need_ctx):
    B, S, _ = hx.shape
    q, k, v = project_gqa(hx, w_qkv, q_g, k_g, SWA_KV_HEADS)
    qc, kc, vc = project_gqa(hc, w_qkv, q_g, k_g, SWA_KV_HEADS)
    qb = group_heads(apply_rope(q, cos, sin), SWA_KV_HEADS)
    pad = ((0, 0), (BLOCK, BLOCK), (0, 0), (0, 0))
    k_pad = jnp.pad(apply_rope(k, cos, sin), pad)
    v_pad = jnp.pad(v, pad)
    rel = jnp.arange(3 * BLOCK)[None, :] - BLOCK - jnp.arange(BLOCK)[:, None]
    band = jnp.abs(rel) <= SWA_WINDOW
    sink = sinks.astype(jnp.float32).reshape(SWA_KV_HEADS, N_HEADS // SWA_KV_HEADS)[None, :, :, None, None]

    def one_block(i):
        start = i * BLOCK
        q_blk = lax.dynamic_slice_in_dim(qb, start, BLOCK, axis=1)
        k_blk = lax.dynamic_slice_in_dim(k_pad, start, 3 * BLOCK, axis=1)
        v_blk = lax.dynamic_slice_in_dim(v_pad, start, 3 * BLOCK, axis=1)
        kpos = start - BLOCK + jnp.arange(3 * BLOCK)
        mask = band & ((kpos >= 0) & (kpos < S))[None, :]
        return multi_source_attention(q_blk, [(k_blk, v_blk, mask), (kc, vc, None)], sink)

    o = lax.map(one_block, jnp.arange(S // BLOCK))
    out_x = jnp.moveaxis(o, 0, 1).reshape(B, S, D_MODEL) @ w_o
    out_c = context_self_attention(qc, kc, vc, SWA_KV_HEADS, w_o, sink) if need_ctx else None
    return out_x, out_c


def dense_gqa_attention(hx, hc, w_qkv, q_g, k_g, w_o, cos, sin, need_ctx):
    B, S, _ = hx.shape
    q, k, v = project_gqa(hx, w_qkv, q_g, k_g, GQA_KV_HEADS)
    qc, kc, vc = project_gqa(hc, w_qkv, q_g, k_g, GQA_KV_HEADS)
    qb = group_heads(apply_rope(q, cos, sin), GQA_KV_HEADS)
    k = apply_rope(k, cos, sin)

    def one_block(i):
        q_blk = lax.dynamic_slice_in_dim(qb, i * BLOCK, BLOCK, axis=1)
        return multi_source_attention(q_blk, [(k, v, None), (kc, vc, None)])

    o = lax.map(one_block, jnp.arange(S // BLOCK))
    out_x = jnp.moveaxis(o, 0, 1).reshape(B, S, D_MODEL) @ w_o
    out_c = context_self_attention(qc, kc, vc, GQA_KV_HEADS, w_o) if need_ctx else None
    return out_x, out_c


def mlstm_scan(q, k, v, log_i, log_f, state, emit):
    B, T = q.shape[:2]
    nc = T // ML_CHUNK

    def chunks(a):
        return jnp.moveaxis(a.reshape((B, nc, ML_CHUNK) + a.shape[2:]), 1, 0)

    lower = jnp.tril(jnp.ones((ML_CHUNK, ML_CHUNK), dtype=bool))

    def step(carry, xs):
        C, n, m = carry
        qc, kc, vc, lic, lfc = xs
        b = jnp.moveaxis(jnp.cumsum(lfc, axis=1), 1, 2)
        li = jnp.moveaxis(lic, 1, 2)
        out = None
        if emit:
            dmat = jnp.where(lower, b[..., :, None] - b[..., None, :] + li[..., None, :], NEG_INF)
            g = b + m[..., None]
            m_t = jnp.maximum(g, jnp.max(dmat, axis=-1))
            w = jnp.exp(dmat - m_t[..., None]) * jnp.einsum('bthd,bshd->bhts', qc, kc)
            w_prev = jnp.exp(g - m_t)
            num = jnp.einsum('bhts,bshv->bhtv', w, vc) + w_prev[..., None] * jnp.einsum('bthd,bhdv->bhtv', qc, C)
            den = jnp.sum(w, axis=-1) + w_prev * jnp.einsum('bthd,bhd->bht', qc, n)
            h = num / jnp.maximum(jnp.abs(den), jnp.exp(-m_t))[..., None]
            out = jnp.moveaxis(h, 1, 2)
        b_end = b[..., -1]
        lw = b_end[..., None] - b + li
        m_new = jnp.maximum(b_end + m, jnp.max(lw, axis=-1))
        decay = jnp.exp(b_end + m - m_new)
        ws = jnp.exp(lw - m_new[..., None])
        C_new = decay[..., None, None] * C + jnp.einsum('bhs,bshd,bshv->bhdv', ws, kc, vc)
        n_new = decay[..., None] * n + jnp.einsum('bhs,bshd->bhd', ws, kc)
        return (C_new, n_new, m_new), out

    state, hs = lax.scan(step, state, tuple(chunks(a) for a in (q, k, v, log_i, log_f)))
    if emit:
        hs = jnp.moveaxis(hs, 0, 1).reshape(B, T, ML_HEADS, ML_V_DIM)
    return hs, state


def mlstm_mixer(hx, hc, w_in, gate_b, head_g, w_o, need_ctx):
    f32 = jnp.float32
    qk, vd = ML_HEADS * ML_QK_DIM, ML_HEADS * ML_V_DIM

    def project(h):
        B, T, _ = h.shape
        q, k, v, o, gates = jnp.split(h @ w_in, [qk, 2 * qk, 2 * qk + vd, 2 * qk + 2 * vd], axis=-1)
        q = q.reshape(B, T, ML_HEADS, ML_QK_DIM).astype(f32)
        k = k.reshape(B, T, ML_HEADS, ML_QK_DIM).astype(f32) * ML_QK_DIM ** -0.5
        v = v.reshape(B, T, ML_HEADS, ML_V_DIM).astype(f32)
        gates = (gates.astype(f32) + gate_b.astype(f32)).reshape(B, T, 4, ML_HEADS)
        fwd = (gates[:, :, 0], jax.nn.log_sigmoid(gates[:, :, 1]))
        bwd = (gates[:, :, 2], jax.nn.log_sigmoid(gates[:, :, 3]))
        return (q, k, v), o, fwd, bwd

    def flip(a):
        return jnp.flip(a, axis=1)

    def run_both(qkv, fwd, bwd, st_f, st_b, emit):
        h_f, st_f = mlstm_scan(*qkv, *fwd, st_f, emit)
        h_b, st_b = mlstm_scan(*(flip(a) for a in qkv), *(flip(a) for a in bwd), st_b, emit)
        h = (h_f + flip(h_b)) if emit else None
        return h, st_f, st_b

    def read_out(h, o):
        B, T = h.shape[:2]
        hn = rms_norm(h, head_g.reshape(ML_HEADS, ML_V_DIM)).reshape(B, T, vd).astype(o.dtype)
        return (jax.nn.sigmoid(o) * hn) @ w_o

    qkv_x, o_x, fwd_x, bwd_x = project(hx)
    qkv_c, o_c, fwd_c, bwd_c = project(hc)
    B = hx.shape[0]
    init = (jnp.zeros((B, ML_HEADS, ML_QK_DIM, ML_V_DIM), f32),
            jnp.zeros((B, ML_HEADS, ML_QK_DIM), f32),
            jnp.zeros((B, ML_HEADS), f32))
    h_c, st_f, st_b = run_both(qkv_c, fwd_c, bwd_c, init, init, need_ctx)
    h_x, _, _ = run_both(qkv_x, fwd_x, bwd_x, st_f, st_b, True)
    out_x = read_out(h_x, o_x)
    out_c = read_out(h_c, o_c) if need_ctx else None
    return out_x, out_c


def conv_glu(h, w_in, conv_w, conv_b, w_out):
    T = h.shape[1]
    g, u = jnp.split(h @ w_in, 2, axis=-1)
    pad = CONV_W // 2
    gp = jnp.pad(g, ((0, 0), (pad, pad), (0, 0)))
    gc = conv_b
    for j in range(CONV_W):
        gc = gc + gp[:, j:j + T] * conv_w[j]
    return (jax.nn.gelu(gc) * u) @ w_out


def setup_inputs(seed: int = 0) -> dict:
    key = jax.random.key(seed)
    ks = iter(jax.random.split(key, 48))
    f32 = jnp.float32
    D = D_MODEL

    def nrm(shape, scale):
        return scale * jax.random.normal(next(ks), shape, f32)

    def gain(shape):
        return 1.0 + nrm(shape, 0.02)

    nA, nB, nC, nD = (len(range(m, DEPTH, N_MIXERS)) for m in range(N_MIXERS))
    hd = N_HEADS * HEAD_DIM
    w_swa = (N_HEADS + 2 * SWA_KV_HEADS) * HEAD_DIM
    w_gqa = (N_HEADS + 2 * GQA_KV_HEADS) * HEAD_DIM
    w_ml = 2 * ML_HEADS * ML_QK_DIM + 2 * ML_HEADS * ML_V_DIM + 4 * ML_HEADS
    gate_base = jnp.array([0.0, ML_FGATE_BIAS, 0.0, ML_FGATE_BIAS], f32)[None, :, None]
    return {
        'x': nrm((BATCH, SEQ, D), 1.0),
        'c': nrm((BATCH, D), 1.0),
        'ctx': nrm((BATCH, CTX_LEN, D), 1.0),
        'c_ctx': nrm((D,), 1.0),
        'ada_w': nrm((DEPTH, D, 6 * D), 0.5 * D ** -0.5),
        'ada_b': nrm((DEPTH, 6 * D), 0.02),
        'norm1_g': gain((DEPTH, D)),
        'norm2_g': gain((DEPTH, D)),
        'ffn_w_in': nrm((DEPTH, D, 2 * D_FF), D ** -0.5),
        'ffn_conv_w': nrm((DEPTH, CONV_W, D_FF), CONV_W ** -0.5),
        'ffn_conv_b': nrm((DEPTH, D_FF), 0.02),
        'ffn_w_out': nrm((DEPTH, D_FF, D), D_FF ** -0.5),
        'na_w_qkv': nrm((nA, D, 3 * hd), D ** -0.5),
        'na_q_g': gain((nA, HEAD_DIM)),
        'na_k_g': gain((nA, HEAD_DIM)),
        'na_rel_bias': nrm((nA, N_HEADS, 2 * NA_ROWS - 1, 2 * NA_COLS - 1), 0.1),
        'na_w_o': nrm((nA, hd, D), hd ** -0.5),
        'swa_w_qkv': nrm((nB, D, w_swa), D ** -0.5),
        'swa_q_g': gain((nB, HEAD_DIM)),
        'swa_k_g': gain((nB, HEAD_DIM)),
        'swa_sinks': nrm((nB, N_HEADS), 1.0),
        'swa_w_o': nrm((nB, hd, D), hd ** -0.5),
        'ml_w_in': nrm((nC, D, w_ml), D ** -0.5),
        'ml_gate_b': (gate_base + nrm((nC, 4, ML_HEADS), 0.1)).reshape(nC, 4 * ML_HEADS),
        'ml_head_g': gain((nC, ML_HEADS * ML_V_DIM)),
        'ml_w_o': nrm((nC, ML_HEADS * ML_V_DIM, D), (ML_HEADS * ML_V_DIM) ** -0.5),
        'gqa_w_qkv': nrm((nD, D, w_gqa), D ** -0.5),
        'gqa_q_g': gain((nD, HEAD_DIM)),
        'gqa_k_g': gain((nD, HEAD_DIM)),
        'gqa_w_o': nrm((nD, hd, D), hd ** -0.5),
    }


def reference(x, c, ctx, c_ctx, ada_w, ada_b, norm1_g, norm2_g, ffn_w_in, ffn_conv_w, ffn_conv_b, ffn_w_out,
              na_w_qkv, na_q_g, na_k_g, na_rel_bias, na_w_o,
              swa_w_qkv, swa_q_g, swa_k_g, swa_sinks, swa_w_o,
              ml_w_in, ml_gate_b, ml_head_g, ml_w_o,
              gqa_w_qkv, gqa_q_g, gqa_k_g, gqa_w_o):
    S = x.shape[1]
    cos, sin = axial_rope_tables(S)
    xc = ctx
    for i in range(DEPTH):
        kind, j = i % N_MIXERS, i // N_MIXERS
        need_ctx = i < DEPTH - 1
        sh1, sc1, g1, sh2, sc2, g2 = adaln(c[:, None, :], ada_w[i], ada_b[i])
        csh1, csc1, cg1, csh2, csc2, cg2 = adaln(c_ctx[None, :], ada_w[i], ada_b[i])
        hx = modulate(rms_norm(x, norm1_g[i]), sh1, sc1)
        hc = modulate(rms_norm(xc, norm1_g[i]), csh1, csc1)
        if kind == 0:
            ox, oc = neighbourhood_attention(hx, hc, na_w_qkv[j], na_q_g[j], na_k_g[j], na_rel_bias[j], na_w_o[j], need_ctx)
        elif kind == 1:
            ox, oc = sliding_window_attention(hx, hc, swa_w_qkv[j], swa_q_g[j], swa_k_g[j], swa_sinks[j], swa_w_o[j],
                                              cos, sin, need_ctx)
        elif kind == 2:
            ox, oc = mlstm_mixer(hx, hc, ml_w_in[j], ml_gate_b[j], ml_head_g[j], ml_w_o[j], need_ctx)
        else:
            ox, oc = dense_gqa_attention(hx, hc, gqa_w_qkv[j], gqa_q_g[j], gqa_k_g[j], gqa_w_o[j], cos, sin, need_ctx)
        x = x + g1 * ox
        hx = modulate(rms_norm(x, norm2_g[i]), sh2, sc2)
        x = x + g2 * conv_glu(hx, ffn_w_in[i], ffn_conv_w[i], ffn_conv_b[i], ffn_w_out[i])
        if need_ctx:
            xc = xc + cg1 * oc
            hc = modulate(rms_norm(xc, norm2_g[i]), csh2, csc2)
            xc = xc + cg2 * conv_glu(hc, ffn_w_in[i], ffn_conv_w[i], ffn_conv_b[i], ffn_w_out[i])
    return x
```

```python
import functools
import math

import numpy as np
import jax
import jax.numpy as jnp
from jax import lax
from jax.experimental import pallas as pl
from jax.experimental.pallas import tpu as pltpu

_F32 = jnp.float32
_BF = jnp.bfloat16
_EPS = 1e-6
_NEG = -1e30
_LOG2E = 1.4426950408889634
_GRID_W = 64
_SWA_WINDOW = 128
_ROPE_BASE = 10000.0
_LANES = 128
_HALO = 8
_VMEM_LIMIT = 56 << 20


def _cp(*sem):
    return pltpu.CompilerParams(dimension_semantics=sem, vmem_limit_bytes=_VMEM_LIMIT)


def _dot(a, b):
    return jnp.dot(a, b, preferred_element_type=_F32)


def _dot_nt(a, b):
    return lax.dot_general(a, b, (((1,), (1,)), ((), ())), preferred_element_type=_F32)


def _dot_tn(a, b):
    return lax.dot_general(a, b, (((0,), (0,)), ((), ())), preferred_element_type=_F32)


def _norm_mod(x, g, sh, sc):
    ms = jnp.mean(x * x, axis=-1, keepdims=True)
    y = x * lax.rsqrt(ms + _EPS) * g
    return y * (1.0 + sc) + sh


def _adaln_body(c_ref, w_ref, b_ref, o_ref):
    c = c_ref[...]
    s = (c * jax.nn.sigmoid(c)).astype(_BF)
    o_ref[0] = _dot(s, w_ref[0].astype(_BF)) + b_ref[0]


def _adaln(cond, ada_w, ada_b):
    depth, d, n = ada_w.shape
    bn = 1024
    return pl.pallas_call(
        _adaln_body,
        grid=(depth, n // bn),
        in_specs=[
            pl.BlockSpec((8, d), lambda l, j: (0, 0)),
            pl.BlockSpec((1, d, bn), lambda l, j: (l, 0, j)),
            pl.BlockSpec((1, 1, bn), lambda l, j: (l, 0, j)),
        ],
        out_specs=pl.BlockSpec((1, 8, bn), lambda l, j: (l, 0, j)),
        out_shape=jax.ShapeDtypeStruct((depth, 8, n), _F32),
        compiler_params=_cp("arbitrary", "arbitrary"),
        name="adaln",
    )(cond, ada_w, ada_b.reshape(depth, 1, n))


def _rope_partner(y):
    lane = lax.broadcasted_iota(jnp.int32, (1, _LANES), 1)
    first = (lane & 32) == 0
    return jnp.where(first, pltpu.roll(y, 96, 1), pltpu.roll(y, 32, 1))


def _proj_body(layouts, has_rope, n_gain, *refs):
    x_ref, g_ref, sh_ref, sc_ref, w_ref = refs[:5]
    pos = 5
    if has_rope:
        cs_ref, sn_ref = refs[pos], refs[pos + 1]
        pos += 2
    gain_refs = refs[pos:pos + n_gain]
    pos += n_gain
    o_ref, h_ref = refs[pos], refs[pos + 1]
    j = pl.program_id(1)

    @pl.when(j == 0)
    def _():
        h_ref[...] = _norm_mod(x_ref[...], g_ref[...], sh_ref[0], sc_ref[0]).astype(_BF)

    acc = _dot(h_ref[...], w_ref[...])

    for lo, hi, segs in layouts:
        @pl.when((j >= lo) & (j < hi))
        def _(segs=segs):
            for c0, c1, kind, gi, rope, mult in segs:
                if kind == "plain":
                    a = acc[:, c0:c1]
                    if mult != 1.0:
                        a = a * mult
                    o_ref[:, c0:c1] = a.astype(o_ref.dtype)
                else:
                    gain = gain_refs[gi][...]
                    for c in range(c0, c1, _LANES):
                        a = acc[:, c:c + _LANES]
                        y = a * lax.rsqrt(jnp.mean(a * a, axis=-1, keepdims=True) + _EPS) * gain
                        if rope:
                            y = y * cs_ref[...] + _rope_partner(y) * sn_ref[...]
                        o_ref[:, c:c + _LANES] = y.astype(o_ref.dtype)


def _proj(X, g, sh, sc, W, layouts, geo, bn, out_dtype, rope=None, gains=()):
    t, d = X.shape
    n = W.shape[1]
    bm = geo["bm"]
    nrt = t // bm
    mrow = geo["mrow"]
    in_specs = [
        pl.BlockSpec((bm, d), lambda i, j: (i, 0)),
        pl.BlockSpec((1, d), lambda i, j: (0, 0)),
        pl.BlockSpec((1, 1, d), lambda i, j: (mrow(i), 0, 0)),
        pl.BlockSpec((1, 1, d), lambda i, j: (mrow(i), 0, 0)),
        pl.BlockSpec((d, bn), lambda i, j: (0, j)),
    ]
    args = [X, g, sh, sc, W]
    if rope is not None:
        tps, nxt = geo["tiles_per_seq"], geo["n_x_tiles"]
        tab = lambda i, j: (jnp.where(i < nxt, i % tps, tps), 0)
        in_specs += [pl.BlockSpec((bm, _LANES), tab), pl.BlockSpec((bm, _LANES), tab)]
        args += list(rope)
    for gn in gains:
        in_specs.append(pl.BlockSpec((1, _LANES), lambda i, j: (0, 0)))
        args.append(gn)
    return pl.pallas_call(
        functools.partial(_proj_body, layouts, rope is not None, len(gains)),
        grid=(nrt, n // bn),
        in_specs=in_specs,
        out_specs=pl.BlockSpec((bm, bn), lambda i, j: (i, j)),
        out_shape=jax.ShapeDtypeStruct((t, n), out_dtype),
        scratch_shapes=[pltpu.VMEM((bm, d), _BF)],
        compiler_params=_cp("arbitrary", "arbitrary"),
        name="proj",
    )(*args)


def _qkv_layouts(n_q, n_kv, bn, rope):
    hd = _LANES
    bounds = [(0, n_q * hd, "q"), (n_q * hd, (n_q + n_kv) * hd, "k"), ((n_q + n_kv) * hd, (n_q + 2 * n_kv) * hd, "v")]
    n = (n_q + 2 * n_kv) * hd
    per_tile = []
    for jt in range(n // bn):
        lo, hi = jt * bn, (jt + 1) * bn
        segs = []
        for b0, b1, nm in bounds:
            s0, s1 = max(lo, b0), min(hi, b1)
            if s0 < s1:
                if nm == "v":
                    segs.append((s0 - lo, s1 - lo, "plain", 0, False, 1.0))
                else:
                    segs.append((s0 - lo, s1 - lo, "head", 0 if nm == "q" else 1, rope, 1.0))
        per_tile.append(tuple(segs))
    layouts = []
    for jt, segs in enumerate(per_tile):
        if layouts and layouts[-1][2] == segs and layouts[-1][1] == jt:
            layouts[-1] = (layouts[-1][0], jt + 1, segs)
        else:
            layouts.append((jt, jt + 1, segs))
    return tuple(layouts)


def _oproj_body(a_ref, w_ref, x_ref, gt_ref, o_ref):
    o_ref[...] = x_ref[...] + gt_ref[0] * _dot(a_ref[...], w_ref[...])


def _oproj(A, W, X, gate, geo, nrt, bm, bn):
    k = A.shape[1]
    d = W.shape[1]
    mrow = functools.partial(geo["mrow_bm"], bm)
    return pl.pallas_call(
        _oproj_body,
        grid=(nrt, d // bn),
        in_specs=[
            pl.BlockSpec((bm, k), lambda i, j: (i, 0)),
            pl.BlockSpec((k, bn), lambda i, j: (0, j)),
            pl.BlockSpec((bm, bn), lambda i, j: (i, j)),
            pl.BlockSpec((1, 1, bn), lambda i, j: (mrow(i), 0, j)),
        ],
        out_specs=pl.BlockSpec((bm, bn), lambda i, j: (i, j)),
        out_shape=jax.ShapeDtypeStruct((nrt * bm, d), _F32),
        compiler_params=_cp("arbitrary", "arbitrary"),
        name="oproj",
    )(A, W, X, gate)


def _ffn1_body(bm, n_x_rows, seq, ctx_len, x_ref, xp_ref, xn_ref, g_ref, sh_ref, sc_ref,
               wg_ref, wu_ref, cw_ref, cb_ref, o_ref, h_ref):
    i = pl.program_id(0)
    j = pl.program_id(1)

    @pl.when(j == 0)
    def _():
        g, sh, sc = g_ref[...], sh_ref[0], sc_ref[0]
        h_ref[0:bm, :] = _norm_mod(x_ref[...], g, sh, sc).astype(_BF)
        halo = jnp.concatenate([xp_ref[...], xn_ref[...]], axis=0)
        h_ref[bm:bm + 2 * _HALO, :] = _norm_mod(halo, g, sh, sc).astype(_BF)

    gx = _dot(h_ref[...], wg_ref[...])
    u = _dot(h_ref[0:bm, :], wu_ref[...])
    gm = gx[0:bm]
    g_prev = gx[bm + _HALO - 1:bm + _HALO]
    g_next = gx[bm + _HALO:bm + _HALO + 1]
    row = lax.broadcasted_iota(jnp.int32, (bm, 1), 0)
    tok = i * bm + row
    period = jnp.where(i * bm >= n_x_rows, ctx_len, seq)
    up = jnp.where(row == 0, g_prev, pltpu.roll(gm, 1, 0))
    dn = jnp.where(row == bm - 1, g_next, pltpu.roll(gm, bm - 1, 0))
    up = jnp.where((tok & (period - 1)) != 0, up, 0.0)
    dn = jnp.where(((tok + 1) & (period - 1)) != 0, dn, 0.0)
    cw = cw_ref[...]
    gc = cb_ref[...] + up * cw[0:1] + gm * cw[1:2] + dn * cw[2:3]
    o_ref[...] = (jax.nn.gelu(gc) * u).astype(o_ref.dtype)


def _ffn1(X, g, sh, sc, w_in, conv_w, conv_b, geo, nrt, bf):
    t, d = X.shape
    f = conv_w.shape[1]
    bm = geo["bm"]
    mrow = geo["mrow"]
    hb = bm // _HALO
    last = t // _HALO - 1
    nf = f // bf
    body = functools.partial(_ffn1_body, bm, geo["n_x_rows"], geo["seq"], geo["ctx_len"])
    return pl.pallas_call(
        body,
        grid=(nrt, nf),
        in_specs=[
            pl.BlockSpec((bm, d), lambda i, j: (i, 0)),
            pl.BlockSpec((_HALO, d), lambda i, j: (jnp.maximum(i * hb - 1, 0), 0)),
            pl.BlockSpec((_HALO, d), lambda i, j: (jnp.minimum((i + 1) * hb, last), 0)),
            pl.BlockSpec((1, d), lambda i, j: (0, 0)),
            pl.BlockSpec((1, 1, d), lambda i, j: (mrow(i), 0, 0)),
            pl.BlockSpec((1, 1, d), lambda i, j: (mrow(i), 0, 0)),
            pl.BlockSpec((d, bf), lambda i, j: (0, j)),
            pl.BlockSpec((d, bf), lambda i, j: (0, j + nf)),
            pl.BlockSpec((3, bf), lambda i, j: (0, j)),
            pl.BlockSpec((1, bf), lambda i, j: (0, j)),
        ],
        out_specs=pl.BlockSpec((bm, bf), lambda i, j: (i, j)),
        out_shape=jax.ShapeDtypeStruct((nrt * bm, f), _BF),
        scratch_shapes=[pltpu.VMEM((bm + 2 * _HALO, d), _BF)],
        compiler_params=_cp("arbitrary", "arbitrary"),
        name="ffn1",
    )(X, X, X, g, sh, sc, w_in, w_in, conv_w, conv_b)


def _na_body(seq, bq, kwin, q_ref, k_ref, v_ref, kc_ref, vc_ref, b_ref, o_ref):
    j = pl.program_id(2)
    ks = jnp.clip(j * bq - (kwin - bq) // 2, 0, seq - kwin)
    ks = pl.multiple_of(ks, 256)
    q = q_ref[...]
    k = k_ref[pl.ds(ks, kwin), :]
    v = v_ref[pl.ds(ks, kwin), :]
    s = _dot_nt(q, k) + b_ref[0, 0]
    sc = _dot_nt(q, kc_ref[...])
    m = jnp.maximum(jnp.max(s, axis=-1, keepdims=True), jnp.max(sc, axis=-1, keepdims=True))
    p = jnp.exp2(s - m)
    pc = jnp.exp2(sc - m)
    l = jnp.sum(p, axis=-1, keepdims=True) + jnp.sum(pc, axis=-1, keepdims=True)
    o = _dot(p.astype(_BF), v) + _dot(pc.astype(_BF), vc_ref[...])
    o_ref[...] = (o / l).astype(o_ref.dtype)


def _na_bias_table(rel_bias, rows, rq, rk):
    h, nr2, nc2 = rel_bias.shape
    na_rows, na_cols = (nr2 + 1) // 2, (nc2 + 1) // 2
    kr = min(na_rows, rows)
    w = _GRID_W
    nblk = rows // rq
    tabs = []
    for jb in (0, min(1, nblk - 1), nblk - 1):
        kb0 = int(np.clip(jb * rq - (rk - rq) // 2, 0, rows - rk))
        r = jb * rq + np.arange(rq)
        r0 = np.clip(r - kr // 2, 0, rows - kr)
        krow = kb0 + np.arange(rk)
        drow = krow[None, :] - r[:, None] + (na_rows - 1)
        row_ok = (krow[None, :] >= r0[:, None]) & (krow[None, :] < r0[:, None] + kr)
        c = np.arange(w)
        c0 = np.clip(c - na_cols // 2, 0, w - na_cols)
        dcol = c[None, :] - c[:, None] + (na_cols - 1)
        col_ok = (c[None, :] >= c0[:, None]) & (c[None, :] < c0[:, None] + na_cols)
        drow_f = np.broadcast_to(np.clip(drow, 0, nr2 - 1)[:, None, :, None], (rq, w, rk, w)).reshape(rq * w, rk * w)
        dcol_f = np.broadcast_to(np.clip(dcol, 0, nc2 - 1)[None, :, None, :], (rq, w, rk, w)).reshape(rq * w, rk * w)
        ok = (row_ok[:, None, :, None] & col_ok[None, :, None, :]).reshape(rq * w, rk * w)
        vals = rel_bias.astype(_F32)[:, drow_f, dcol_f] * _LOG2E
        tabs.append(jnp.where(jnp.asarray(ok)[None], vals, _NEG))
    return jnp.stack(tabs, axis=0)


def _na_attention(QKV, bias_tab, geo, n_heads):
    b, s, lc = geo["batch"], geo["seq"], geo["ctx_len"]
    t = QKV.shape[0]
    bq, kwin = bias_tab.shape[2], bias_tab.shape[3]
    nj = s // bq
    cblk = geo["n_x_rows"] // lc

    def btype(j):
        return jnp.where(j == 0, 0, jnp.where(j == nj - 1, 2, 1))

    return pl.pallas_call(
        functools.partial(_na_body, s, bq, kwin),
        grid=(b, n_heads, nj),
        in_specs=[
            pl.BlockSpec((bq, _LANES), lambda bi, h, j: (bi * nj + j, h)),
            pl.BlockSpec((s, _LANES), lambda bi, h, j: (bi, n_heads + h)),
            pl.BlockSpec((s, _LANES), lambda bi, h, j: (bi, 2 * n_heads + h)),
            pl.BlockSpec((lc, _LANES), lambda bi, h, j: (cblk + bi, n_heads + h)),
            pl.BlockSpec((lc, _LANES), lambda bi, h, j: (cblk + bi, 2 * n_heads + h)),
            pl.BlockSpec((1, 1, bq, kwin), lambda bi, h, j: (btype(j), h, 0, 0)),
        ],
        out_specs=pl.BlockSpec((bq, _LANES), lambda bi, h, j: (bi * nj + j, h)),
        out_shape=jax.ShapeDtypeStruct((t, n_heads * _LANES), _BF),
        compiler_params=_cp("arbitrary", "arbitrary", "arbitrary"),
        name="na_attn",
    )(QKV, QKV, QKV, QKV, QKV, bias_tab)


def _ctx_attn_body(group, has_sink, *refs):
    if has_sink:
        q_ref, k_ref, v_ref, sk_ref, _, o_ref = refs
    else:
        q_ref, k_ref, v_ref, _, o_ref = refs
    k = k_ref[...]
    v = v_ref[...]
    for g in range(group):
        c = g * _LANES
        s = _dot_nt(q_ref[:, c:c + _LANES], k)
        m = jnp.max(s, axis=-1, keepdims=True)
        if has_sink:
            sk = sk_ref[0][:, c:c + 1]
            m = jnp.maximum(m, sk)
        p = jnp.exp2(s - m)
        l = jnp.sum(p, axis=-1, keepdims=True)
        if has_sink:
            l = l + jnp.exp2(sk - m)
        o_ref[:, c:c + _LANES] = (_dot(p.astype(_BF), v) / l).astype(o_ref.dtype)


def _ctx_attention(QKV, O, geo, n_heads, n_kv, sink=None):
    b, lc = geo["batch"], geo["ctx_len"]
    group = n_heads // n_kv
    gw = group * _LANES
    cblk = geo["n_x_rows"] // lc
    in_specs = [
        pl.BlockSpec((lc, gw), lambda bi, n: (cblk + bi, n)),
        pl.BlockSpec((lc, _LANES), lambda bi, n: (cblk + bi, n_heads + n)),
        pl.BlockSpec((lc, _LANES), lambda bi, n: (cblk + bi, n_heads + n_kv + n)),
    ]
    args = [QKV, QKV, QKV]
    if sink is not None:
        in_specs.append(pl.BlockSpec((1, 1, gw), lambda bi, n: (n, 0, 0)))
        args.append(sink)
    in_specs.append(pl.BlockSpec(memory_space=pl.ANY))
    args.append(O)
    return pl.pallas_call(
        functools.partial(_ctx_attn_body, group, sink is not None),
        grid=(b, n_kv),
        in_specs=in_specs,
        out_specs=pl.BlockSpec((lc, gw), lambda bi, n: (cblk + bi, n)),
        out_shape=jax.ShapeDtypeStruct(O.shape, O.dtype),
        input_output_aliases={len(args) - 1: 0},
        compiler_params=_cp("arbitrary", "arbitrary"),
        name="ctx_attn",
    )(*args)


def _swa_body(seq, bq, win, group, q_ref, k_ref, v_ref, kc_ref, vc_ref, sk_ref, o_ref):
    t = pl.program_id(2)
    kw = bq + 2 * win
    ks = jnp.clip(t * bq - win, 0, seq - kw)
    ks = pl.multiple_of(ks, _LANES)
    k = k_ref[pl.ds(ks, kw), :]
    v = v_ref[pl.ds(ks, kw), :]
    kc = kc_ref[...]
    vc = vc_ref[...]
    qpos = t * bq + lax.broadcasted_iota(jnp.int32, (bq, 1), 0)
    kpos = ks + lax.broadcasted_iota(jnp.int32, (1, kw), 1)
    band = jnp.abs(kpos - qpos) <= win
    for g in range(group):
        c = g * _LANES
        q = q_ref[:, c:c + _LANES]
        s = jnp.where(band, _dot_nt(q, k), _NEG)
        sc = _dot_nt(q, kc)
        sk = sk_ref[0][:, c:c + 1]
        m = jnp.maximum(jnp.maximum(jnp.max(s, axis=-1, keepdims=True), jnp.max(sc, axis=-1, keepdims=True)), sk)
        p = jnp.exp2(s - m)
        pc = jnp.exp2(sc - m)
        l = jnp.sum(p, axis=-1, keepdims=True) + jnp.sum(pc, axis=-1, keepdims=True) + jnp.exp2(sk - m)
        o = _dot(p.astype(_BF), v) + _dot(pc.astype(_BF), vc)
        o_ref[:, c:c + _LANES] = (o / l).astype(o_ref.dtype)


def _swa_attention(QKV, sink, geo, n_heads, n_kv, bq):
    b, s, lc = geo["batch"], geo["seq"], geo["ctx_len"]
    t = QKV.shape[0]
    group = n_heads // n_kv
    gw = group * _LANES
    nq = s // bq
    cblk = geo["n_x_rows"] // lc
    return pl.pallas_call(
        functools.partial(_swa_body, s, bq, _SWA_WINDOW, group),
        grid=(b, n_kv, nq),
        in_specs=[
            pl.BlockSpec((bq, gw), lambda bi, n, j: (bi * nq + j, n)),
            pl.BlockSpec((s, _LANES), lambda bi, n, j: (bi, n_heads + n)),
            pl.BlockSpec((s, _LANES), lambda bi, n, j: (bi, n_heads + n_kv + n)),
            pl.BlockSpec((lc, _LANES), lambda bi, n, j: (cblk + bi, n_heads + n)),
            pl.BlockSpec((lc, _LANES), lambda bi, n, j: (cblk + bi, n_heads + n_kv + n)),
            pl.BlockSpec((1, 1, gw), lambda bi, n, j: (n, 0, 0)),
        ],
        out_specs=pl.BlockSpec((bq, gw), lambda bi, n, j: (bi * nq + j, n)),
        out_shape=jax.ShapeDtypeStruct((t, n_heads * _LANES), _BF),
        compiler_params=_cp("arbitrary", "arbitrary", "arbitrary"),
        name="swa_attn",
    )(QKV, QKV, QKV, QKV, QKV, sink)


def _gqa_body(seq, bq, ck, group, q_ref, k_ref, v_ref, kc_ref, vc_ref, o_ref, m_ref, l_ref, acc_ref):
    qs = jnp.concatenate([q_ref[:, g * _LANES:(g + 1) * _LANES] for g in range(group)], axis=0)
    m_ref[...] = jnp.full(m_ref.shape, _NEG, _F32)
    l_ref[...] = jnp.zeros(l_ref.shape, _F32)
    acc_ref[...] = jnp.zeros(acc_ref.shape, _F32)

    def step(k, v):
        s = _dot_nt(qs, k)
        m_old = m_ref[...]
        m_new = jnp.maximum(m_old, jnp.max(s, axis=-1, keepdims=True))
        alpha = jnp.exp2(m_old - m_new)
        p = jnp.exp2(s - m_new)
        l_ref[...] = alpha * l_ref[...] + jnp.sum(p, axis=-1, keepdims=True)
        acc_ref[...] = alpha * acc_ref[...] + _dot(p.astype(_BF), v)
        m_ref[...] = m_new

    def chunk(c, carry):
        off = pl.multiple_of(c * ck, ck)
        step(k_ref[pl.ds(off, ck), :], v_ref[pl.ds(off, ck), :])
        return carry

    lax.fori_loop(0, seq // ck, chunk, 0)
    step(kc_ref[...], vc_ref[...])
    out = acc_ref[...] / l_ref[...]
    for g in range(group):
        o_ref[:, g * _LANES:(g + 1) * _LANES] = out[g * bq:(g + 1) * bq].astype(o_ref.dtype)


def _gqa_attention(QKV, geo, n_heads, n_kv, bq, ck):
    b, s, lc = geo["batch"], geo["seq"], geo["ctx_len"]
    group = n_heads // n_kv
    gw = group * _LANES
    nq = s // bq
    cblk = geo["n_x_rows"] // lc
    rows = group * bq
    return pl.pallas_call(
        functools.partial(_gqa_body, s, bq, ck, group),
        grid=(b, n_kv, nq),
        in_specs=[
            pl.BlockSpec((bq, gw), lambda bi, n, j: (bi * nq + j, n)),
            pl.BlockSpec((s, _LANES), lambda bi, n, j: (bi, n_heads + n)),
            pl.BlockSpec((s, _LANES), lambda bi, n, j: (bi, n_heads + n_kv + n)),
            pl.BlockSpec((lc, _LANES), lambda bi, n, j: (cblk + bi, n_heads + n)),
            pl.BlockSpec((lc, _LANES), lambda bi, n, j: (cblk + bi, n_heads + n_kv + n)),
        ],
        out_specs=pl.BlockSpec((bq, gw), lambda bi, n, j: (bi * nq + j, n)),
        out_shape=jax.ShapeDtypeStruct((geo["n_x_rows"], n_heads * _LANES), _BF),
        scratch_shapes=[pltpu.VMEM((rows, 1), _F32), pltpu.VMEM((rows, 1), _F32), pltpu.VMEM((rows, _LANES), _F32)],
        compiler_params=_cp("arbitrary", "arbitrary", "arbitrary"),
        name="gqa_attn",
    )(QKV, QKV, QKV, QKV, QKV)


def _log_sigmoid(x):
    return jnp.minimum(x, 0.0) - jnp.log1p(jnp.exp(-jnp.abs(x)))


def _mlstm_body(n_heads, chunk, q_ref, k_ref, v_ref, gt_ref, gb_ref, o_ref, c_ref, n_ref, m_ref):
    h = pl.program_id(1)
    d = pl.program_id(2)
    c = pl.program_id(3)

    @pl.when(c == 0)
    def _():
        c_ref[...] = jnp.zeros(c_ref.shape, _F32)
        n_ref[...] = jnp.zeros(n_ref.shape, _F32)
        m_ref[...] = jnp.zeros(m_ref.shape, _F32)

    L = chunk
    gates = gt_ref[...] + gb_ref[...]
    lane = lax.broadcasted_iota(jnp.int32, (1, _LANES), 1)
    li_col = jnp.sum(jnp.where(lane == (2 * d) * n_heads + h, gates, 0.0), axis=-1, keepdims=True)
    lf_pre = jnp.sum(jnp.where(lane == (2 * d + 1) * n_heads + h, gates, 0.0), axis=-1, keepdims=True)
    lf_col = _log_sigmoid(lf_pre)
    ti = lax.broadcasted_iota(jnp.int32, (L, L), 0)
    si = lax.broadcasted_iota(jnp.int32, (L, L), 1)
    eye = ti == si
    li_row = jnp.sum(jnp.where(eye, li_col, 0.0), axis=0, keepdims=True)
    lf_row = jnp.sum(jnp.where(eye, lf_col, 0.0), axis=0, keepdims=True)
    sign = 1 - 2 * d
    allowed = (ti - si) * sign >= 0
    allowed_t = (si - ti) * sign >= 0
    b_col = jnp.sum(jnp.where(allowed, lf_row, 0.0), axis=-1, keepdims=True)
    b_row = jnp.sum(jnp.where(allowed_t, lf_col, 0.0), axis=0, keepdims=True)
    total = jnp.sum(lf_col, axis=0, keepdims=True)
    m_prev = m_ref[0:1, 0:1]

    q = q_ref[...]
    k = k_ref[...]
    v = v_ref[...]
    dmat = jnp.where(allowed, b_col - b_row + li_row, _NEG)
    g_col = b_col + m_prev
    m_t = jnp.maximum(g_col, jnp.max(dmat, axis=-1, keepdims=True))
    w = jnp.exp(dmat - m_t) * _dot_nt(q, k)
    w_prev = jnp.exp(g_col - m_t)
    cmat = c_ref[...]
    nvec = n_ref[0:1, :]
    num = _dot(w.astype(_BF), v) + w_prev * _dot(q, cmat.astype(_BF))
    qn = jnp.sum(q.astype(_F32) * nvec, axis=-1, keepdims=True)
    den = jnp.sum(w, axis=-1, keepdims=True) + w_prev * qn
    o_ref[0] = (num / jnp.maximum(jnp.abs(den), jnp.exp(-m_t))).astype(o_ref.dtype)

    lw = total - b_col + li_col
    m_new = jnp.maximum(total + m_prev, jnp.max(lw, axis=0, keepdims=True))
    decay = jnp.exp(total + m_prev - m_new)
    kw = k.astype(_F32) * jnp.exp(lw - m_new)
    c_ref[...] = decay * cmat + _dot_tn(kw.astype(_BF), v)
    n_ref[...] = jnp.broadcast_to(decay * nvec + jnp.sum(kw, axis=0, keepdims=True), n_ref.shape)
    m_ref[...] = jnp.broadcast_to(m_new, m_ref.shape)


def _mlstm_scan(P, G, gate_b, geo, n_heads, dqk, dv, chunk):
    b, s, lc = geo["batch"], geo["seq"], geo["ctx_len"]
    t = P.shape[0]
    nctx, nx = lc // chunk, s // chunk
    cbase = geo["n_x_rows"] // chunk
    qb, vb = dqk // _LANES, dv // _LANES

    def rblk(bi, d, c):
        cc = c - nctx
        in_ctx = cbase + bi * nctx + jnp.where(d == 0, c, nctx - 1 - c)
        in_x = bi * nx + jnp.where(d == 0, cc, nx - 1 - cc)
        return jnp.where(c < nctx, in_ctx, in_x)

    k_off = n_heads * dqk // dqk
    v_off = 2 * n_heads * dqk // dv
    return pl.pallas_call(
        functools.partial(_mlstm_body, n_heads, chunk),
        grid=(b, n_heads, 2, nctx + nx),
        in_specs=[
            pl.BlockSpec((chunk, dqk), lambda bi, h, d, c: (rblk(bi, d, c), h)),
            pl.BlockSpec((chunk, dqk), lambda bi, h, d, c: (rblk(bi, d, c), k_off + h)),
            pl.BlockSpec((chunk, dv), lambda bi, h, d, c: (rblk(bi, d, c), v_off + h)),
            pl.BlockSpec((chunk, _LANES), lambda bi, h, d, c: (rblk(bi, d, c), 0)),
            pl.BlockSpec((1, _LANES), lambda bi, h, d, c: (0, 0)),
        ],
        out_specs=pl.BlockSpec((1, chunk, dv), lambda bi, h, d, c: (d, rblk(bi, d, c), h)),
        out_shape=jax.ShapeDtypeStruct((2, t, n_heads * dv), _F32),
        scratch_shapes=[pltpu.VMEM((dqk, dv), _F32), pltpu.VMEM((8, dqk), _F32), pltpu.VMEM((8, _LANES), _F32)],
        compiler_params=_cp("arbitrary", "arbitrary", "arbitrary", "arbitrary"),
        name="mlstm_scan",
    )(P, P, P, G, gate_b)


def _ml_out_body(n_heads, dv, hs_ref, og_ref, hg_ref, w_ref, x_ref, gt_ref, o_ref):
    hsum = hs_ref[0] + hs_ref[1]
    parts = []
    for hh in range(n_heads):
        a = hsum[:, hh * dv:(hh + 1) * dv]
        hn = a * lax.rsqrt(jnp.mean(a * a, axis=-1, keepdims=True) + _EPS) * hg_ref[:, hh * dv:(hh + 1) * dv]
        og = og_ref[:, hh * dv:(hh + 1) * dv].astype(_F32)
        parts.append((jax.nn.sigmoid(og) * hn).astype(_BF))
    a = jnp.concatenate(parts, axis=-1)
    o_ref[...] = x_ref[...] + gt_ref[0] * _dot(a, w_ref[...])


def _ml_out(Hs, P, head_g, W, X, gate, geo, n_heads, dv, bm):
    t, d = X.shape
    kd = n_heads * dv
    o_off = (P.shape[1] - kd) // kd
    mrow = functools.partial(geo["mrow_bm"], bm)
    return pl.pallas_call(
        functools.partial(_ml_out_body, n_heads, dv),
        grid=(t // bm,),
        in_specs=[
            pl.BlockSpec((2, bm, kd), lambda i: (0, i, 0)),
            pl.BlockSpec((bm, kd), lambda i: (i, o_off)),
            pl.BlockSpec((1, kd), lambda i: (0, 0)),
            pl.BlockSpec((kd, d), lambda i: (0, 0)),
            pl.BlockSpec((bm, d), lambda i: (i, 0)),
            pl.BlockSpec((1, 1, d), lambda i: (mrow(i), 0, 0)),
        ],
        out_specs=pl.BlockSpec((bm, d), lambda i: (i, 0)),
        out_shape=jax.ShapeDtypeStruct((t, d), _F32),
        compiler_params=_cp("arbitrary"),
        name="ml_out",
    )(Hs, P, head_g, W, X, gate)


def _rope_tables(seq, pad_rows):
    tpos = np.arange(seq)
    row = (tpos // _GRID_W).astype(np.float32)
    col = (tpos % _GRID_W).astype(np.float32)
    nf = _LANES // 4
    inv = (_ROPE_BASE ** (-jnp.arange(nf, dtype=_F32) / nf))
    ar = jnp.asarray(row)[:, None] * inv
    ac = jnp.asarray(col)[:, None] * inv
    cr, sr, cc, sc = jnp.cos(ar), jnp.sin(ar), jnp.cos(ac), jnp.sin(ac)
    cs = jnp.concatenate([cr, cr, cc, cc], axis=1)
    sn = jnp.concatenate([-sr, sr, -sc, sc], axis=1)
    cs = jnp.concatenate([cs, jnp.ones((pad_rows, _LANES), _F32)], axis=0)
    sn = jnp.concatenate([sn, jnp.zeros((pad_rows, _LANES), _F32)], axis=0)
    return cs, sn


def _geometry(batch, seq, ctx_len):
    bm = min(1024, batch * ctx_len)
    assert seq % bm == 0 and (batch * ctx_len) % bm == 0
    assert seq & (seq - 1) == 0 and ctx_len & (ctx_len - 1) == 0
    n_x_rows = batch * seq

    def mrow_bm(bm_, i):
        return jnp.minimum((i * bm_) // seq, batch)

    return dict(batch=batch, seq=seq, ctx_len=ctx_len, bm=bm, n_x_rows=n_x_rows,
                n_x_tiles=n_x_rows // bm, tiles_per_seq=seq // bm,
                mrow=functools.partial(mrow_bm, bm), mrow_bm=mrow_bm)


def kernel(x, c, ctx, c_ctx, ada_w, ada_b, norm1_g, norm2_g, ffn_w_in, ffn_conv_w, ffn_conv_b, ffn_w_out,
           na_w_qkv, na_q_g, na_k_g, na_rel_bias, na_w_o,
           swa_w_qkv, swa_q_g, swa_k_g, swa_sinks, swa_w_o,
           ml_w_in, ml_gate_b, ml_head_g, ml_w_o,
           gqa_w_qkv, gqa_q_g, gqa_k_g, gqa_w_o):
    batch, seq, d = x.shape
    ctx_len = ctx.shape[1]
    depth = ada_w.shape[0]
    n_heads = d // _LANES
    geo = _geometry(batch, seq, ctx_len)
    bm = geo["bm"]
    n_x_rows = geo["n_x_rows"]
    n_x_tiles = geo["n_x_tiles"]
    qscale = (_LANES ** -0.5) * _LOG2E

    X = jnp.concatenate([x.reshape(n_x_rows, d), ctx.reshape(batch * ctx_len, d)], axis=0)
    t_all = X.shape[0]
    nrt_all = t_all // bm

    cond = jnp.concatenate([c, c_ctx[None, :], jnp.zeros((8 - batch - 1, d), _F32)], axis=0)
    mods = _adaln(cond, ada_w, ada_b)

    cs, sn = _rope_tables(seq, bm)
    f = ffn_conv_w.shape[2]
    bf = 512
    bm_o = min(512, bm)

    for i in range(depth):
        kind, jl = i % 4, i // 4
        need_ctx = i < depth - 1
        mod = [mods[i, :batch + 1, k * d:(k + 1) * d].reshape(batch + 1, 1, d) for k in range(6)]
        sh1, sc1, g1, sh2, sc2, g2 = mod
        n1 = norm1_g[i].reshape(1, d)
        n2 = norm2_g[i].reshape(1, d)
        nrt_o = t_all // bm_o if need_ctx else n_x_rows // bm_o

        if kind == 0:
            w = na_w_qkv[jl].astype(_BF)
            gq = (na_q_g[jl] * qscale).reshape(1, _LANES)
            gk = na_k_g[jl].reshape(1, _LANES)
            bn = 1024
            QKV = _proj(X, n1, sh1, sc1, w, _qkv_layouts(n_heads, n_heads, bn, False), geo, bn, _BF, gains=(gq, gk))
            rows = seq // _GRID_W
            tab = _na_bias_table(na_rel_bias[jl], rows, 8, 16)
            O = _na_attention(QKV, tab, geo, n_heads)
            if need_ctx:
                O = _ctx_attention(QKV, O, geo, n_heads, n_heads)
            X = _oproj(O, na_w_o[jl].astype(_BF), X, g1, geo, nrt_o, bm_o, d)
        elif kind == 1:
            n_kv = (swa_w_qkv.shape[2] // _LANES - n_heads) // 2
            w = swa_w_qkv[jl].astype(_BF)
            gq = (swa_q_g[jl] * qscale).reshape(1, _LANES)
            gk = swa_k_g[jl].reshape(1, _LANES)
            bn = 512
            QKV = _proj(X, n1, sh1, sc1, w, _qkv_layouts(n_heads, n_kv, bn, True), geo, bn, _BF,
                        rope=(cs, sn), gains=(gq, gk))
            group = n_heads // n_kv
            sink = jnp.repeat(swa_sinks[jl].astype(_F32) * _LOG2E, _LANES).reshape(n_kv, 1, group * _LANES)
            O = _swa_attention(QKV, sink, geo, n_heads, n_kv, 256)
            if need_ctx:
                O = _ctx_attention(QKV, O, geo, n_heads, n_kv, sink)
            X = _oproj(O, swa_w_o[jl].astype(_BF), X, g1, geo, nrt_o, bm_o, d)
        elif kind == 2:
            mh = ml_gate_b.shape[1] // 4
            dv = d // mh
            dqk = dv // 2
            nmain = 2 * mh * dqk + 2 * mh * dv
            w_main = ml_w_in[jl][:, :nmain].astype(_BF)
            w_gate = jnp.pad(ml_w_in[jl][:, nmain:], ((0, 0), (0, _LANES - 4 * mh))).astype(_BF)
            bn = 1024
            segs_by_tile = []
            k0, k1 = mh * dqk, 2 * mh * dqk
            for jt in range(nmain // bn):
                lo, hi = jt * bn, (jt + 1) * bn
                cuts = sorted({lo, hi, min(max(k0, lo), hi), min(max(k1, lo), hi)})
                segs = tuple((a - lo, b_ - lo, "plain", 0, False, dqk ** -0.5 if k0 <= a < k1 else 1.0)
                             for a, b_ in zip(cuts[:-1], cuts[1:]))
                segs_by_tile.append((jt, jt + 1, segs))
            P = _proj(X, n1, sh1, sc1, w_main, tuple(segs_by_tile), geo, bn, _BF)
            G = _proj(X, n1, sh1, sc1, w_gate, ((0, 1, ((0, _LANES, "plain", 0, False, 1.0),)),), geo, _LANES, _F32)
            gb = jnp.pad(ml_gate_b[jl].astype(_F32), (0, _LANES - 4 * mh)).reshape(1, _LANES)
            Hs = _mlstm_scan(P, G, gb, geo, mh, dqk, dv, 256)
            X = _ml_out(Hs, P, ml_head_g[jl].reshape(1, mh * dv), ml_w_o[jl].astype(_BF), X, g1, geo, mh, dv, bm_o)
        else:
            n_kv = (gqa_w_qkv.shape[2] // _LANES - n_heads) // 2
            w = gqa_w_qkv[jl].astype(_BF)
            gq = (gqa_q_g[jl] * qscale).reshape(1, _LANES)
            gk = gqa_k_g[jl].reshape(1, _LANES)
            bn = 512
            QKV = _proj(X, n1, sh1, sc1, w, _qkv_layouts(n_heads, n_kv, bn, True), geo, bn, _BF,
                        rope=(cs, sn), gains=(gq, gk))
            O = _gqa_attention(QKV, geo, n_heads, n_kv, 256, 512)
            if need_ctx:
                Oc = jnp.zeros((t_all, d), _BF).at[:n_x_rows].set(O)
                O = _ctx_attention(QKV, Oc, geo, n_heads, n_kv)
            X = _oproj(O, gqa_w_o[jl].astype(_BF), X, g1, geo, nrt_o, bm_o, d)

        nrt_f = nrt_all if need_ctx else n_x_tiles
        A = _ffn1(X, n2, sh2, sc2, ffn_w_in[i].astype(_BF), ffn_conv_w[i], ffn_conv_b[i].reshape(1, f), geo, nrt_f, bf)
        X = _oproj(A, ffn_w_out[i].astype(_BF), X, g2, geo, A.shape[0] // bm_o, bm_o, 1024)

    return X[:n_x_rows].reshape(batch, seq, d)
```

```python
import functools
import math

import numpy as np
import jax
import jax.numpy as jnp
from jax import lax
from jax.experimental import pallas as pl
from jax.experimental.pallas import tpu as pltpu

_F32 = jnp.float32
_BF = jnp.bfloat16
_EPS = 1e-6
_NEG = -1e30
_LOG2E = 1.4426950408889634
_GRID_W = 64
_SWA_WINDOW = 128
_ROPE_BASE = 10000.0
_LANES = 128
_HALO = 8
_VMEM_LIMIT = 56 << 20


def _cp(*sem):
    return pltpu.CompilerParams(dimension_semantics=sem, vmem_limit_bytes=_VMEM_LIMIT)


def _dot(a, b):
    return jnp.dot(a, b, preferred_element_type=_F32)


def _dot_nt(a, b):
    return lax.dot_general(a, b, (((1,), (1,)), ((), ())), preferred_element_type=_F32)


def _dot_tn(a, b):
    return lax.dot_general(a, b, (((0,), (0,)), ((), ())), preferred_element_type=_F32)


def _norm_mod(x, g, sh, sc):
    ms = jnp.mean(x * x, axis=-1, keepdims=True)
    y = x * lax.rsqrt(ms + _EPS) * g
    return y * (1.0 + sc) + sh


def _adaln_body(c_ref, w_ref, b_ref, o_ref):
    c = c_ref[...]
    s = (c * jax.nn.sigmoid(c)).astype(_BF)
    o_ref[0] = _dot(s, w_ref[0].astype(_BF)) + b_ref[0]


def _adaln(cond, ada_w, ada_b):
    depth, d, n = ada_w.shape
    bn = 1024
    return pl.pallas_call(
        _adaln_body,
        grid=(depth, n // bn),
        in_specs=[
            pl.BlockSpec((8, d), lambda l, j: (0, 0)),
            pl.BlockSpec((1, d, bn), lambda l, j: (l, 0, j)),
            pl.BlockSpec((1, 1, bn), lambda l, j: (l, 0, j)),
        ],
        out_specs=pl.BlockSpec((1, 8, bn), lambda l, j: (l, 0, j)),
        out_shape=jax.ShapeDtypeStruct((depth, 8, n), _F32),
        compiler_params=_cp("arbitrary", "arbitrary"),
        name="adaln",
    )(cond, ada_w, ada_b.reshape(depth, 1, n))


def _rope_partner(y):
    lane = lax.broadcasted_iota(jnp.int32, (1, _LANES), 1)
    first = (lane & 32) == 0
    return jnp.where(first, pltpu.roll(y, 96, 1), pltpu.roll(y, 32, 1))


def _proj_body(layouts, has_rope, n_gain, *refs):
    x_ref, g_ref, sh_ref, sc_ref, w_ref = refs[:5]
    pos = 5
    if has_rope:
        cs_ref, sn_ref = refs[pos], refs[pos + 1]
        pos += 2
    gain_refs = refs[pos:pos + n_gain]
    pos += n_gain
    o_ref, h_ref = refs[pos], refs[pos + 1]
    j = pl.program_id(1)

    @pl.when(j == 0)
    def _():
        h_ref[...] = _norm_mod(x_ref[...], g_ref[...], sh_ref[0], sc_ref[0]).astype(_BF)

    acc = _dot(h_ref[...], w_ref[...])

    for lo, hi, segs in layouts:
        @pl.when((j >= lo) & (j < hi))
        def _(segs=segs):
            for c0, c1, kind, gi, rope, mult in segs:
                if kind == "plain":
                    a = acc[:, c0:c1]
                    if mult != 1.0:
                        a = a * mult
                    o_ref[:, c0:c1] = a.astype(o_ref.dtype)
                else:
                    gain = gain_refs[gi][...]
                    for c in range(c0, c1, _LANES):
                        a = acc[:, c:c + _LANES]
                        y = a * lax.rsqrt(jnp.mean(a * a, axis=-1, keepdims=True) + _EPS) * gain
                        if rope:
                            y = y * cs_ref[...] + _rope_partner(y) * sn_ref[...]
                        o_ref[:, c:c + _LANES] = y.astype(o_ref.dtype)


def _proj(X, g, sh, sc, W, layouts, geo, bn, out_dtype, rope=None, gains=()):
    t, d = X.shape
    n = W.shape[1]
    bm = geo["bm"]
    nrt = t // bm
    mrow = geo["mrow"]
    in_specs = [
        pl.BlockSpec((bm, d), lambda i, j: (i, 0)),
        pl.BlockSpec((1, d), lambda i, j: (0, 0)),
        pl.BlockSpec((1, 1, d), lambda i, j: (mrow(i), 0, 0)),
        pl.BlockSpec((1, 1, d), lambda i, j: (mrow(i), 0, 0)),
        pl.BlockSpec((d, bn), lambda i, j: (0, j)),
    ]
    args = [X, g, sh, sc, W]
    if rope is not None:
        tps, nxt = geo["tiles_per_seq"], geo["n_x_tiles"]
        tab = lambda i, j: (jnp.where(i < nxt, i % tps, tps), 0)
        in_specs += [pl.BlockSpec((bm, _LANES), tab), pl.BlockSpec((bm, _LANES), tab)]
        args += list(rope)
    for gn in gains:
        in_specs.append(pl.BlockSpec((1, _LANES), lambda i, j: (0, 0)))
        args.append(gn)
    return pl.pallas_call(
        functools.partial(_proj_body, layouts, rope is not None, len(gains)),
        grid=(nrt, n // bn),
        in_specs=in_specs,
        out_specs=pl.BlockSpec((bm, bn), lambda i, j: (i, j)),
        out_shape=jax.ShapeDtypeStruct((t, n), out_dtype),
        scratch_shapes=[pltpu.VMEM((bm, d), _BF)],
        compiler_params=_cp("arbitrary", "arbitrary"),
        name="proj",
    )(*args)


def _qkv_layouts(n_q, n_kv, bn, rope):
    hd = _LANES
    bounds = [(0, n_q * hd, "q"), (n_q * hd, (n_q + n_kv) * hd, "k"), ((n_q + n_kv) * hd, (n_q + 2 * n_kv) * hd, "v")]
    n = (n_q + 2 * n_kv) * hd
    per_tile = []
    for jt in range(n // bn):
        lo, hi = jt * bn, (jt + 1) * bn
        segs = []
        for b0, b1, nm in bounds:
            s0, s1 = max(lo, b0), min(hi, b1)
            if s0 < s1:
                if nm == "v":
                    segs.append((s0 - lo, s1 - lo, "plain", 0, False, 1.0))
                else:
                    segs.append((s0 - lo, s1 - lo, "head", 0 if nm == "q" else 1, rope, 1.0))
        per_tile.append(tuple(segs))
    layouts = []
    for jt, segs in enumerate(per_tile):
        if layouts and layouts[-1][2] == segs and layouts[-1][1] == jt:
            layouts[-1] = (layouts[-1][0], jt + 1, segs)
        else:
            layouts.append((jt, jt + 1, segs))
    return tuple(layouts)


def _oproj_body(a_ref, w_ref, x_ref, gt_ref, o_ref):
    o_ref[...] = x_ref[...] + gt_ref[0] * _dot(a_ref[...], w_ref[...])


def _oproj(A, W, X, gate, geo, nrt, bm, bn):
    k = A.shape[1]
    d = W.shape[1]
    mrow = functools.partial(geo["mrow_bm"], bm)
    return pl.pallas_call(
        _oproj_body,
        grid=(nrt, d // bn),
        in_specs=[
            pl.BlockSpec((bm, k), lambda i, j: (i, 0)),
            pl.BlockSpec((k, bn), lambda i, j: (0, j)),
            pl.BlockSpec((bm, bn), lambda i, j: (i, j)),
            pl.BlockSpec((1, 1, bn), lambda i, j: (mrow(i), 0, j)),
        ],
        out_specs=pl.BlockSpec((bm, bn), lambda i, j: (i, j)),
        out_shape=jax.ShapeDtypeStruct((nrt * bm, d), _F32),
        compiler_params=_cp("arbitrary", "arbitrary"),
        name="oproj",
    )(A, W, X, gate)


def _ffn1_body(bm, n_x_rows, seq, ctx_len, x_ref, xp_ref, xn_ref, g_ref, sh_ref, sc_ref,
               wg_ref, wu_ref, cw_ref, cb_ref, o_ref, h_ref):
    i = pl.program_id(0)
    j = pl.program_id(1)

    @pl.when(j == 0)
    def _():
        g, sh, sc = g_ref[...], sh_ref[0], sc_ref[0]
        h_ref[0:bm, :] = _norm_mod(x_ref[...], g, sh, sc).astype(_BF)
        halo = jnp.concatenate([xp_ref[...], xn_ref[...]], axis=0)
        h_ref[bm:bm + 2 * _HALO, :] = _norm_mod(halo, g, sh, sc).astype(_BF)

    gx = _dot(h_ref[...], wg_ref[...])
    u = _dot(h_ref[0:bm, :], wu_ref[...])
    gm = gx[0:bm]
    g_prev = gx[bm + _HALO - 1:bm + _HALO]
    g_next = gx[bm + _HALO:bm + _HALO + 1]
    row = lax.broadcasted_iota(jnp.int32, (bm, 1), 0)
    tok = i * bm + row
    period = jnp.where(i * bm >= n_x_rows, ctx_len, seq)
    up = jnp.where(row == 0, g_prev, pltpu.roll(gm, 1, 0))
    dn = jnp.where(row == bm - 1, g_next, pltpu.roll(gm, bm - 1, 0))
    up = jnp.where((tok & (period - 1)) != 0, up, 0.0)
    dn = jnp.where(((tok + 1) & (period - 1)) != 0, dn, 0.0)
    cw = cw_ref[...]
    gc = cb_ref[...] + up * cw[0:1] + gm * cw[1:2] + dn * cw[2:3]
    o_ref[...] = (jax.nn.gelu(gc) * u).astype(o_ref.dtype)


def _ffn1(X, g, sh, sc, w_in, conv_w, conv_b, geo, nrt, bf):
    t, d = X.shape
    f = conv_w.shape[1]
    bm = geo["bm"]
    mrow = geo["mrow"]
    hb = bm // _HALO
    last = t // _HALO - 1
    nf = f // bf
    body = functools.partial(_ffn1_body, bm, geo["n_x_rows"], geo["seq"], geo["ctx_len"])
    return pl.pallas_call(
        body,
        grid=(nrt, nf),
        in_specs=[
            pl.BlockSpec((bm, d), lambda i, j: (i, 0)),
            pl.BlockSpec((_HALO, d), lambda i, j: (jnp.maximum(i * hb - 1, 0), 0)),
            pl.BlockSpec((_HALO, d), lambda i, j: (jnp.minimum((i + 1) * hb, last), 0)),
            pl.BlockSpec((1, d), lambda i, j: (0, 0)),
            pl.BlockSpec((1, 1, d), lambda i, j: (mrow(i), 0, 0)),
            pl.BlockSpec((1, 1, d), lambda i, j: (mrow(i), 0, 0)),
            pl.BlockSpec((d, bf), lambda i, j: (0, j)),
            pl.BlockSpec((d, bf), lambda i, j: (0, j + nf)),
            pl.BlockSpec((3, bf), lambda i, j: (0, j)),
            pl.BlockSpec((1, bf), lambda i, j: (0, j)),
        ],
        out_specs=pl.BlockSpec((bm, bf), lambda i, j: (i, j)),
        out_shape=jax.ShapeDtypeStruct((nrt * bm, f), _BF),
        scratch_shapes=[pltpu.VMEM((bm + 2 * _HALO, d), _BF)],
        compiler_params=_cp("arbitrary", "arbitrary"),
        name="ffn1",
    )(X, X, X, g, sh, sc, w_in, w_in, conv_w, conv_b)


def _with_ones(v):
    return jnp.concatenate([v, jnp.ones(v.shape, v.dtype)], axis=1)


def _na_body(seq, bq, kwin, q_ref, k_ref, v_ref, kc_ref, vc_ref, b_ref, o_ref):
    j = pl.program_id(2)
    ks = jnp.clip(j * bq - (kwin - bq) // 2, 0, seq - kwin)
    ks = pl.multiple_of(ks, 256)
    q = q_ref[...]
    k = k_ref[pl.ds(ks, kwin), :]
    v = v_ref[pl.ds(ks, kwin), :]
    s = _dot_nt(q, k) + b_ref[0, 0]
    sc = _dot_nt(q, kc_ref[...])
    m = jnp.maximum(jnp.max(s, axis=-1, keepdims=True), jnp.max(sc, axis=-1, keepdims=True))
    p = jnp.exp2(s - m).astype(_BF)
    pc = jnp.exp2(sc - m).astype(_BF)
    acc = _dot(p, _with_ones(v)) + _dot(pc, _with_ones(vc_ref[...]))
    o_ref[...] = (acc[:, :_LANES] / acc[:, _LANES:]).astype(o_ref.dtype)


def _na_bias_table(rel_bias, rows, rq, rk):
    h, nr2, nc2 = rel_bias.shape
    na_rows, na_cols = (nr2 + 1) // 2, (nc2 + 1) // 2
    kr = min(na_rows, rows)
    w = _GRID_W
    nblk = rows // rq
    c = np.arange(w)
    c0 = np.clip(c - na_cols // 2, 0, w - na_cols)
    col_ok = (c[None, :] >= c0[:, None]) & (c[None, :] < c0[:, None] + na_cols)
    rbp = jnp.pad(rel_bias.astype(_F32) * _LOG2E, ((0, 0), (0, 0), (w, w)))
    tcol = jnp.stack([rbp[:, :, na_cols - 1 - qc + w:na_cols - 1 - qc + 2 * w] for qc in range(w)], axis=2)
    tcol = jnp.where(jnp.asarray(col_ok)[None, None], tcol, _NEG)
    masked = jnp.full((h, w, w), _NEG, _F32)
    tabs = []
    for jb in (0, min(1, nblk - 1), nblk - 1):
        kb0 = int(np.clip(jb * rq - (rk - rq) // 2, 0, rows - rk))
        bands = []
        for qi in range(rq):
            r = jb * rq + qi
            r0 = int(np.clip(r - kr // 2, 0, rows - kr))
            blocks = []
            for ki in range(rk):
                krow = kb0 + ki
                blocks.append(tcol[:, krow - r + na_rows - 1] if r0 <= krow < r0 + kr else masked)
            bands.append(jnp.concatenate(blocks, axis=-1))
        tabs.append(jnp.concatenate(bands, axis=1))
    return jnp.stack(tabs, axis=0)


def _na_attention(QKV, bias_tab, geo, n_heads):
    b, s, lc = geo["batch"], geo["seq"], geo["ctx_len"]
    t = QKV.shape[0]
    bq, kwin = bias_tab.shape[2], bias_tab.shape[3]
    nj = s // bq
    cblk = geo["n_x_rows"] // lc

    def btype(j):
        return jnp.where(j == 0, 0, jnp.where(j == nj - 1, 2, 1))

    return pl.pallas_call(
        functools.partial(_na_body, s, bq, kwin),
        grid=(b, n_heads, nj),
        in_specs=[
            pl.BlockSpec((bq, _LANES), lambda bi, h, j: (bi * nj + j, h)),
            pl.BlockSpec((s, _LANES), lambda bi, h, j: (bi, n_heads + h)),
            pl.BlockSpec((s, _LANES), lambda bi, h, j: (bi, 2 * n_heads + h)),
            pl.BlockSpec((lc, _LANES), lambda bi, h, j: (cblk + bi, n_heads + h)),
            pl.BlockSpec((lc, _LANES), lambda bi, h, j: (cblk + bi, 2 * n_heads + h)),
            pl.BlockSpec((1, 1, bq, kwin), lambda bi, h, j: (btype(j), h, 0, 0)),
        ],
        out_specs=pl.BlockSpec((bq, _LANES), lambda bi, h, j: (bi * nj + j, h)),
        out_shape=jax.ShapeDtypeStruct((t, n_heads * _LANES), _BF),
        compiler_params=_cp("arbitrary", "arbitrary", "arbitrary"),
        name="na_attn",
    )(QKV, QKV, QKV, QKV, QKV, bias_tab)


def _ctx_attn_body(group, has_sink, *refs):
    if has_sink:
        q_ref, k_ref, v_ref, sk_ref, _, o_ref = refs
    else:
        q_ref, k_ref, v_ref, _, o_ref = refs
    k = k_ref[...]
    vx = _with_ones(v_ref[...])
    for g in range(group):
        c = g * _LANES
        s = _dot_nt(q_ref[:, c:c + _LANES], k)
        m = jnp.max(s, axis=-1, keepdims=True)
        if has_sink:
            sk = sk_ref[0][:, c:c + 1]
            m = jnp.maximum(m, sk)
        acc = _dot(jnp.exp2(s - m).astype(_BF), vx)
        l = acc[:, _LANES:]
        if has_sink:
            l = l + jnp.exp2(sk - m)
        o_ref[:, c:c + _LANES] = (acc[:, :_LANES] / l).astype(o_ref.dtype)


def _ctx_attention(QKV, O, geo, n_heads, n_kv, sink=None):
    b, lc = geo["batch"], geo["ctx_len"]
    group = n_heads // n_kv
    gw = group * _LANES
    cblk = geo["n_x_rows"] // lc
    in_specs = [
        pl.BlockSpec((lc, gw), lambda bi, n: (cblk + bi, n)),
        pl.BlockSpec((lc, _LANES), lambda bi, n: (cblk + bi, n_heads + n)),
        pl.BlockSpec((lc, _LANES), lambda bi, n: (cblk + bi, n_heads + n_kv + n)),
    ]
    args = [QKV, QKV, QKV]
    if sink is not None:
        in_specs.append(pl.BlockSpec((1, 1, gw), lambda bi, n: (n, 0, 0)))
        args.append(sink)
    in_specs.append(pl.BlockSpec(memory_space=pl.ANY))
    args.append(O)
    return pl.pallas_call(
        functools.partial(_ctx_attn_body, group, sink is not None),
        grid=(b, n_kv),
        in_specs=in_specs,
        out_specs=pl.BlockSpec((lc, gw), lambda bi, n: (cblk + bi, n)),
        out_shape=jax.ShapeDtypeStruct(O.shape, O.dtype),
        input_output_aliases={len(args) - 1: 0},
        compiler_params=_cp("arbitrary", "arbitrary"),
        name="ctx_attn",
    )(*args)


def _swa_body(seq, bq, win, group, q_ref, k_ref, v_ref, kc_ref, vc_ref, sk_ref, o_ref):
    t = pl.program_id(2)
    kw = bq + 2 * win
    ks = jnp.clip(t * bq - win, 0, seq - kw)
    ks = pl.multiple_of(ks, _LANES)
    k = k_ref[pl.ds(ks, kw), :]
    v = v_ref[pl.ds(ks, kw), :]
    kc = kc_ref[...]
    vx = _with_ones(v)
    vcx = _with_ones(vc_ref[...])
    qpos = t * bq + lax.broadcasted_iota(jnp.int32, (bq, 1), 0)
    kpos = ks + lax.broadcasted_iota(jnp.int32, (1, kw), 1)
    band = jnp.abs(kpos - qpos) <= win
    for g in range(group):
        c = g * _LANES
        q = q_ref[:, c:c + _LANES]
        s = jnp.where(band, _dot_nt(q, k), _NEG)
        sc = _dot_nt(q, kc)
        sk = sk_ref[0][:, c:c + 1]
        m = jnp.maximum(jnp.maximum(jnp.max(s, axis=-1, keepdims=True), jnp.max(sc, axis=-1, keepdims=True)), sk)
        p = jnp.exp2(s - m).astype(_BF)
        pc = jnp.exp2(sc - m).astype(_BF)
        acc = _dot(p, vx) + _dot(pc, vcx)
        l = acc[:, _LANES:] + jnp.exp2(sk - m)
        o_ref[:, c:c + _LANES] = (acc[:, :_LANES] / l).astype(o_ref.dtype)


def _swa_attention(QKV, sink, geo, n_heads, n_kv, bq):
    b, s, lc = geo["batch"], geo["seq"], geo["ctx_len"]
    t = QKV.shape[0]
    group = n_heads // n_kv
    gw = group * _LANES
    nq = s // bq
    cblk = geo["n_x_rows"] // lc
    return pl.pallas_call(
        functools.partial(_swa_body, s, bq, _SWA_WINDOW, group),
        grid=(b, n_kv, nq),
        in_specs=[
            pl.BlockSpec((bq, gw), lambda bi, n, j: (bi * nq + j, n)),
            pl.BlockSpec((s, _LANES), lambda bi, n, j: (bi, n_heads + n)),
            pl.BlockSpec((s, _LANES), lambda bi, n, j: (bi, n_heads + n_kv + n)),
            pl.BlockSpec((lc, _LANES), lambda bi, n, j: (cblk + bi, n_heads + n)),
            pl.BlockSpec((lc, _LANES), lambda bi, n, j: (cblk + bi, n_heads + n_kv + n)),
            pl.BlockSpec((1, 1, gw), lambda bi, n, j: (n, 0, 0)),
        ],
        out_specs=pl.BlockSpec((bq, gw), lambda bi, n, j: (bi * nq + j, n)),
        out_shape=jax.ShapeDtypeStruct((t, n_heads * _LANES), _BF),
        compiler_params=_cp("arbitrary", "arbitrary", "arbitrary"),
        name="swa_attn",
    )(QKV, QKV, QKV, QKV, QKV, sink)


def _gqa_body(seq, ck, group, q_ref, k_ref, v_ref, kc_ref, vc_ref, o_ref, m_ref, acc_ref):
    m_ref[...] = jnp.full(m_ref.shape, _NEG, _F32)
    acc_ref[...] = jnp.zeros(acc_ref.shape, _F32)

    def step(k, v):
        width = k.shape[0]
        vx = jnp.concatenate([v, jnp.ones((width, _LANES), v.dtype)], axis=1)
        for g in range(group):
            s = _dot_nt(q_ref[:, g * _LANES:(g + 1) * _LANES], k)
            m_old = m_ref[g]
            m_new = jnp.maximum(m_old, jnp.max(s, axis=-1, keepdims=True))
            alpha = jnp.exp2(m_old - m_new)
            p = jnp.exp2(s - pltpu.repeat(m_new, width // _LANES, axis=1)).astype(_BF)
            acc_ref[g] = pltpu.repeat(alpha, 2, axis=1) * acc_ref[g] + _dot(p, vx)
            m_ref[g] = m_new

    def chunk(c, carry):
        off = pl.multiple_of(c * ck, ck)
        step(k_ref[pl.ds(off, ck), :], v_ref[pl.ds(off, ck), :])
        return carry

    lax.fori_loop(0, seq // ck, chunk, 0)
    step(kc_ref[...], vc_ref[...])
    for g in range(group):
        acc = acc_ref[g]
        o_ref[:, g * _LANES:(g + 1) * _LANES] = (acc[:, :_LANES] / acc[:, _LANES:]).astype(o_ref.dtype)


def _gqa_attention(QKV, geo, n_heads, n_kv, bq, ck):
    b, s, lc = geo["batch"], geo["seq"], geo["ctx_len"]
    group = n_heads // n_kv
    gw = group * _LANES
    nq = s // bq
    cblk = geo["n_x_rows"] // lc
    return pl.pallas_call(
        functools.partial(_gqa_body, s, ck, group),
        grid=(b, n_kv, nq),
        in_specs=[
            pl.BlockSpec((bq, gw), lambda bi, n, j: (bi * nq + j, n)),
            pl.BlockSpec((s, _LANES), lambda bi, n, j: (bi, n_heads + n)),
            pl.BlockSpec((s, _LANES), lambda bi, n, j: (bi, n_heads + n_kv + n)),
            pl.BlockSpec((lc, _LANES), lambda bi, n, j: (cblk + bi, n_heads + n)),
            pl.BlockSpec((lc, _LANES), lambda bi, n, j: (cblk + bi, n_heads + n_kv + n)),
        ],
        out_specs=pl.BlockSpec((bq, gw), lambda bi, n, j: (bi * nq + j, n)),
        out_shape=jax.ShapeDtypeStruct((geo["n_x_rows"], n_heads * _LANES), _BF),
        scratch_shapes=[pltpu.VMEM((group, bq, _LANES), _F32), pltpu.VMEM((group, bq, 2 * _LANES), _F32)],
        compiler_params=_cp("arbitrary", "arbitrary", "arbitrary"),
        name="gqa_attn",
    )(QKV, QKV, QKV, QKV, QKV)


def _log_sigmoid(x):
    return jnp.minimum(x, 0.0) - jnp.log1p(jnp.exp(-jnp.abs(x)))


def _mlstm_body(n_heads, chunk, q_ref, k_ref, v_ref, gt_ref, gb_ref, o_ref, c_ref, n_ref, m_ref):
    h = pl.program_id(1)
    d = pl.program_id(2)
    c = pl.program_id(3)

    @pl.when(c == 0)
    def _():
        c_ref[...] = jnp.zeros(c_ref.shape, _F32)
        n_ref[...] = jnp.zeros(n_ref.shape, _F32)
        m_ref[...] = jnp.zeros(m_ref.shape, _F32)

    L = chunk
    gates = gt_ref[...] + gb_ref[...]
    lane = lax.broadcasted_iota(jnp.int32, (1, _LANES), 1)
    li_col = jnp.sum(jnp.where(lane == (2 * d) * n_heads + h, gates, 0.0), axis=-1, keepdims=True)
    lf_pre = jnp.sum(jnp.where(lane == (2 * d + 1) * n_heads + h, gates, 0.0), axis=-1, keepdims=True)
    lf_col = _log_sigmoid(lf_pre)
    ti = lax.broadcasted_iota(jnp.int32, (L, L), 0)
    si = lax.broadcasted_iota(jnp.int32, (L, L), 1)
    eye = ti == si
    li_row = jnp.sum(jnp.where(eye, li_col, 0.0), axis=0, keepdims=True)
    lf_row = jnp.sum(jnp.where(eye, lf_col, 0.0), axis=0, keepdims=True)
    sign = 1 - 2 * d
    allowed = (ti - si) * sign >= 0
    allowed_t = (si - ti) * sign >= 0
    b_col = jnp.sum(jnp.where(allowed, lf_row, 0.0), axis=-1, keepdims=True)
    b_row = jnp.sum(jnp.where(allowed_t, lf_col, 0.0), axis=0, keepdims=True)
    total = jnp.sum(lf_col, axis=0, keepdims=True)
    m_prev = m_ref[0:1, 0:1]

    q = q_ref[...]
    k = k_ref[...]
    v = v_ref[...]
    dmat = jnp.where(allowed, b_col - b_row + li_row, _NEG)
    g_col = b_col + m_prev
    m_t = jnp.maximum(g_col, jnp.max(dmat, axis=-1, keepdims=True))
    w = jnp.exp(dmat - m_t) * _dot_nt(q, k)
    w_prev = jnp.exp(g_col - m_t)
    cmat = c_ref[...]
    nvec = n_ref[0:1, :]
    num = _dot(w.astype(_BF), v) + w_prev * _dot(q, cmat.astype(_BF))
    qn = jnp.sum(q.astype(_F32) * nvec, axis=-1, keepdims=True)
    den = jnp.sum(w, axis=-1, keepdims=True) + w_prev * qn
    o_ref[0] = (num / jnp.maximum(jnp.abs(den), jnp.exp(-m_t))).astype(o_ref.dtype)

    lw = total - b_col + li_col
    m_new = jnp.maximum(total + m_prev, jnp.max(lw, axis=0, keepdims=True))
    decay = jnp.exp(total + m_prev - m_new)
    kw = k.astype(_F32) * jnp.exp(lw - m_new)
    c_ref[...] = decay * cmat + _dot_tn(kw.astype(_BF), v)
    n_ref[...] = jnp.broadcast_to(decay * nvec + jnp.sum(kw, axis=0, keepdims=True), n_ref.shape)
    m_ref[...] = jnp.broadcast_to(m_new, m_ref.shape)


def _mlstm_scan(P, G, gate_b, geo, n_heads, dqk, dv, chunk):
    b, s, lc = geo["batch"], geo["seq"], geo["ctx_len"]
    t = P.shape[0]
    nctx, nx = lc // chunk, s // chunk
    cbase = geo["n_x_rows"] // chunk
    qb, vb = dqk // _LANES, dv // _LANES

    def rblk(bi, d, c):
        cc = c - nctx
        in_ctx = cbase + bi * nctx + jnp.where(d == 0, c, nctx - 1 - c)
        in_x = bi * nx + jnp.where(d == 0, cc, nx - 1 - cc)
        return jnp.where(c < nctx, in_ctx, in_x)

    k_off = n_heads * dqk // dqk
    v_off = 2 * n_heads * dqk // dv
    return pl.pallas_call(
        functools.partial(_mlstm_body, n_heads, chunk),
        grid=(b, n_heads, 2, nctx + nx),
        in_specs=[
            pl.BlockSpec((chunk, dqk), lambda bi, h, d, c: (rblk(bi, d, c), h)),
            pl.BlockSpec((chunk, dqk), lambda bi, h, d, c: (rblk(bi, d, c), k_off + h)),
            pl.BlockSpec((chunk, dv), lambda bi, h, d, c: (rblk(bi, d, c), v_off + h)),
            pl.BlockSpec((chunk, _LANES), lambda bi, h, d, c: (rblk(bi, d, c), 0)),
            pl.BlockSpec((1, _LANES), lambda bi, h, d, c: (0, 0)),
        ],
        out_specs=pl.BlockSpec((1, chunk, dv), lambda bi, h, d, c: (d, rblk(bi, d, c), h)),
        out_shape=jax.ShapeDtypeStruct((2, t, n_heads * dv), _F32),
        scratch_shapes=[pltpu.VMEM((dqk, dv), _F32), pltpu.VMEM((8, dqk), _F32), pltpu.VMEM((8, _LANES), _F32)],
        compiler_params=_cp("arbitrary", "arbitrary", "arbitrary", "arbitrary"),
        name="mlstm_scan",
    )(P, P, P, G, gate_b)


def _ml_out_body(n_heads, dv, hs_ref, og_ref, hg_ref, w_ref, x_ref, gt_ref, o_ref):
    hsum = hs_ref[0] + hs_ref[1]
    parts = []
    for hh in range(n_heads):
        a = hsum[:, hh * dv:(hh + 1) * dv]
        hn = a * lax.rsqrt(jnp.mean(a * a, axis=-1, keepdims=True) + _EPS) * hg_ref[:, hh * dv:(hh + 1) * dv]
        og = og_ref[:, hh * dv:(hh + 1) * dv].astype(_F32)
        parts.append((jax.nn.sigmoid(og) * hn).astype(_BF))
    a = jnp.concatenate(parts, axis=-1)
    o_ref[...] = x_ref[...] + gt_ref[0] * _dot(a, w_ref[...])


def _ml_out(Hs, P, head_g, W, X, gate, geo, n_heads, dv, bm):
    t, d = X.shape
    kd = n_heads * dv
    o_off = (P.shape[1] - kd) // kd
    mrow = functools.partial(geo["mrow_bm"], bm)
    return pl.pallas_call(
        functools.partial(_ml_out_body, n_heads, dv),
        grid=(t // bm,),
        in_specs=[
            pl.BlockSpec((2, bm, kd), lambda i: (0, i, 0)),
            pl.BlockSpec((bm, kd), lambda i: (i, o_off)),
            pl.BlockSpec((1, kd), lambda i: (0, 0)),
            pl.BlockSpec((kd, d), lambda i: (0, 0)),
            pl.BlockSpec((bm, d), lambda i: (i, 0)),
            pl.BlockSpec((1, 1, d), lambda i: (mrow(i), 0, 0)),
        ],
        out_specs=pl.BlockSpec((bm, d), lambda i: (i, 0)),
        out_shape=jax.ShapeDtypeStruct((t, d), _F32),
        compiler_params=_cp("arbitrary"),
        name="ml_out",
    )(Hs, P, head_g, W, X, gate)


def _rope_tables(seq, pad_rows):
    tpos = np.arange(seq)
    row = (tpos // _GRID_W).astype(np.float32)
    col = (tpos % _GRID_W).astype(np.float32)
    nf = _LANES // 4
    inv = (_ROPE_BASE ** (-jnp.arange(nf, dtype=_F32) / nf))
    ar = jnp.asarray(row)[:, None] * inv
    ac = jnp.asarray(col)[:, None] * inv
    cr, sr, cc, sc = jnp.cos(ar), jnp.sin(ar), jnp.cos(ac), jnp.sin(ac)
    cs = jnp.concatenate([cr, cr, cc, cc], axis=1)
    sn = jnp.concatenate([-sr, sr, -sc, sc], axis=1)
    cs = jnp.concatenate([cs, jnp.ones((pad_rows, _LANES), _F32)], axis=0)
    sn = jnp.concatenate([sn, jnp.zeros((pad_rows, _LANES), _F32)], axis=0)
    return cs, sn


def _geometry(batch, seq, ctx_len):
    bm = min(1024, batch * ctx_len)
    assert seq % bm == 0 and (batch * ctx_len) % bm == 0
    assert seq & (seq - 1) == 0 and ctx_len & (ctx_len - 1) == 0
    n_x_rows = batch * seq

    def mrow_bm(bm_, i):
        return jnp.minimum((i * bm_) // seq, batch)

    return dict(batch=batch, seq=seq, ctx_len=ctx_len, bm=bm, n_x_rows=n_x_rows,
                n_x_tiles=n_x_rows // bm, tiles_per_seq=seq // bm,
                mrow=functools.partial(mrow_bm, bm), mrow_bm=mrow_bm)


def kernel(x, c, ctx, c_ctx, ada_w, ada_b, norm1_g, norm2_g, ffn_w_in, ffn_conv_w, ffn_conv_b, ffn_w_out,
           na_w_qkv, na_q_g, na_k_g, na_rel_bias, na_w_o,
           swa_w_qkv, swa_q_g, swa_k_g, swa_sinks, swa_w_o,
           ml_w_in, ml_gate_b, ml_head_g, ml_w_o,
           gqa_w_qkv, gqa_q_g, gqa_k_g, gqa_w_o):
    batch, seq, d = x.shape
    ctx_len = ctx.shape[1]
    depth = ada_w.shape[0]
    n_heads = d // _LANES
    geo = _geometry(batch, seq, ctx_len)
    bm = geo["bm"]
    n_x_rows = geo["n_x_rows"]
    n_x_tiles = geo["n_x_tiles"]
    qscale = (_LANES ** -0.5) * _LOG2E

    X = jnp.concatenate([x.reshape(n_x_rows, d), ctx.reshape(batch * ctx_len, d)], axis=0)
    t_all = X.shape[0]
    nrt_all = t_all // bm

    cond = jnp.concatenate([c, c_ctx[None, :], jnp.zeros((8 - batch - 1, d), _F32)], axis=0)
    mods = _adaln(cond, ada_w, ada_b)

    cs, sn = _rope_tables(seq, bm)
    f = ffn_conv_w.shape[2]
    bf = 512
    bm_o = min(512, bm)

    for i in range(depth):
        kind, jl = i % 4, i // 4
        need_ctx = i < depth - 1
        mod = [mods[i, :batch + 1, k * d:(k + 1) * d].reshape(batch + 1, 1, d) for k in range(6)]
        sh1, sc1, g1, sh2, sc2, g2 = mod
        n1 = norm1_g[i].reshape(1, d)
        n2 = norm2_g[i].reshape(1, d)
        nrt_o = t_all // bm_o if need_ctx else n_x_rows // bm_o

        if kind == 0:
            w = na_w_qkv[jl].astype(_BF)
            gq = (na_q_g[jl] * qscale).reshape(1, _LANES)
            gk = na_k_g[jl].reshape(1, _LANES)
            bn = 1024
            QKV = _proj(X, n1, sh1, sc1, w, _qkv_layouts(n_heads, n_heads, bn, False), geo, bn, _BF, gains=(gq, gk))
            rows = seq // _GRID_W
            tab = _na_bias_table(na_rel_bias[jl], rows, 8, 16)
            O = _na_attention(QKV, tab, geo, n_heads)
            if need_ctx:
                O = _ctx_attention(QKV, O, geo, n_heads, n_heads)
            X = _oproj(O, na_w_o[jl].astype(_BF), X, g1, geo, nrt_o, bm_o, d)
        elif kind == 1:
            n_kv = (swa_w_qkv.shape[2] // _LANES - n_heads) // 2
            w = swa_w_qkv[jl].astype(_BF)
            gq = (swa_q_g[jl] * qscale).reshape(1, _LANES)
            gk = swa_k_g[jl].reshape(1, _LANES)
            bn = 512
            QKV = _proj(X, n1, sh1, sc1, w, _qkv_layouts(n_heads, n_kv, bn, True), geo, bn, _BF,
                        rope=(cs, sn), gains=(gq, gk))
            group = n_heads // n_kv
            sink = jnp.repeat(swa_sinks[jl].astype(_F32) * _LOG2E, _LANES).reshape(n_kv, 1, group * _LANES)
            O = _swa_attention(QKV, sink, geo, n_heads, n_kv, 256)
            if need_ctx:
                O = _ctx_attention(QKV, O, geo, n_heads, n_kv, sink)
            X = _oproj(O, swa_w_o[jl].astype(_BF), X, g1, geo, nrt_o, bm_o, d)
        elif kind == 2:
            mh = ml_gate_b.shape[1] // 4
            dv = d // mh
            dqk = dv // 2
            nmain = 2 * mh * dqk + 2 * mh * dv
            w_main = ml_w_in[jl][:, :nmain].astype(_BF)
            w_gate = jnp.pad(ml_w_in[jl][:, nmain:], ((0, 0), (0, _LANES - 4 * mh))).astype(_BF)
            bn = 1024
            segs_by_tile = []
            k0, k1 = mh * dqk, 2 * mh * dqk
            for jt in range(nmain // bn):
                lo, hi = jt * bn, (jt + 1) * bn
                cuts = sorted({lo, hi, min(max(k0, lo), hi), min(max(k1, lo), hi)})
                segs = tuple((a - lo, b_ - lo, "plain", 0, False, dqk ** -0.5 if k0 <= a < k1 else 1.0)
                             for a, b_ in zip(cuts[:-1], cuts[1:]))
                segs_by_tile.append((jt, jt + 1, segs))
            P = _proj(X, n1, sh1, sc1, w_main, tuple(segs_by_tile), geo, bn, _BF)
            G = _proj(X, n1, sh1, sc1, w_gate, ((0, 1, ((0, _LANES, "plain", 0, False, 1.0),)),), geo, _LANES, _F32)
            gb = jnp.pad(ml_gate_b[jl].astype(_F32), (0, _LANES - 4 * mh)).reshape(1, _LANES)
            Hs = _mlstm_scan(P, G, gb, geo, mh, dqk, dv, 256)
            X = _ml_out(Hs, P, ml_head_g[jl].reshape(1, mh * dv), ml_w_o[jl].astype(_BF), X, g1, geo, mh, dv, bm_o)
        else:
            n_kv = (gqa_w_qkv.shape[2] // _LANES - n_heads) // 2
            w = gqa_w_qkv[jl].astype(_BF)
            gq = (gqa_q_g[jl] * qscale).reshape(1, _LANES)
            gk = gqa_k_g[jl].reshape(1, _LANES)
            bn = 512
            QKV = _proj(X, n1, sh1, sc1, w, _qkv_layouts(n_heads, n_kv, bn, True), geo, bn, _BF,
                        rope=(cs, sn), gains=(gq, gk))
            O = _gqa_attention(QKV, geo, n_heads, n_kv, 256, 1024)
            if need_ctx:
                Oc = jnp.zeros((t_all, d), _BF).at[:n_x_rows].set(O)
                O = _ctx_attention(QKV, Oc, geo, n_heads, n_kv)
            X = _oproj(O, gqa_w_o[jl].astype(_BF), X, g1, geo, nrt_o, bm_o, d)

        nrt_f = nrt_all if need_ctx else n_x_tiles
        A = _ffn1(X, n2, sh2, sc2, ffn_w_in[i].astype(_BF), ffn_conv_w[i], ffn_conv_b[i].reshape(1, f), geo, nrt_f, bf)
        X = _oproj(A, ffn_w_out[i].astype(_BF), X, g2, geo, A.shape[0] // bm_o, bm_o, 1024)

    return X[:n_x_rows].reshape(batch, seq, d)
```

```python
import functools
import math

import numpy as np
import jax
import jax.numpy as jnp
from jax import lax
from jax.experimental import pallas as pl
from jax.experimental.pallas import tpu as pltpu

_F32 = jnp.float32
_BF = jnp.bfloat16
_EPS = 1e-6
_NEG = -1e30
_LOG2E = 1.4426950408889634
_GRID_W = 64
_SWA_WINDOW = 128
_ROPE_BASE = 10000.0
_LANES = 128
_HALO = 8
_VMEM_LIMIT = 56 << 20


def _cp(*sem):
    return pltpu.CompilerParams(dimension_semantics=sem, vmem_limit_bytes=_VMEM_LIMIT)


def _dot(a, b):
    return jnp.dot(a, b, preferred_element_type=_F32)


def _dot_nt(a, b):
    return lax.dot_general(a, b, (((1,), (1,)), ((), ())), preferred_element_type=_F32)


def _dot_tn(a, b):
    return lax.dot_general(a, b, (((0,), (0,)), ((), ())), preferred_element_type=_F32)


def _norm_mod(x, g, sh, sc):
    ms = jnp.mean(x * x, axis=-1, keepdims=True)
    return (x * lax.rsqrt(ms + _EPS)) * (g * (1.0 + sc)) + sh


def _adaln_body(c_ref, w_ref, b_ref, o_ref):
    c = c_ref[...]
    s = (c * jax.nn.sigmoid(c)).astype(_BF)
    o_ref[0] = _dot(s, w_ref[0].astype(_BF)) + b_ref[0]


def _adaln(cond, ada_w, ada_b):
    depth, d, n = ada_w.shape
    bn = 1024
    return pl.pallas_call(
        _adaln_body,
        grid=(depth, n // bn),
        in_specs=[
            pl.BlockSpec((8, d), lambda l, j: (0, 0)),
            pl.BlockSpec((1, d, bn), lambda l, j: (l, 0, j)),
            pl.BlockSpec((1, 1, bn), lambda l, j: (l, 0, j)),
        ],
        out_specs=pl.BlockSpec((1, 8, bn), lambda l, j: (l, 0, j)),
        out_shape=jax.ShapeDtypeStruct((depth, 8, n), _F32),
        compiler_params=_cp("arbitrary", "arbitrary"),
        name="adaln",
    )(cond, ada_w, ada_b.reshape(depth, 1, n))


def _proj_body(layouts, has_rope, n_gain, *refs):
    x_ref, g_ref, sh_ref, sc_ref, w_ref = refs[:5]
    pos = 5
    tab_refs = ()
    if has_rope:
        tab_refs = refs[pos:pos + 4]
        pos += 4
    gain_refs = refs[pos:pos + n_gain]
    pos += n_gain
    o_ref, h_ref = refs[pos], refs[pos + 1]
    j = pl.program_id(1)

    @pl.when(j == 0)
    def _():
        h_ref[...] = _norm_mod(x_ref[...], g_ref[...], sh_ref[0], sc_ref[0]).astype(_BF)

    for lo, hi, segs in layouts:
        @pl.when((j >= lo) & (j < hi))
        def _(segs=segs):
            acc = _dot(h_ref[...], w_ref[...])
            for c0, c1, kind, gi, rope, mult in segs:
                if kind == "plain":
                    a = acc[:, c0:c1]
                    if mult != 1.0:
                        a = a * mult
                    o_ref[:, c0:c1] = a.astype(o_ref.dtype)
                    continue
                for c in range(c0, c1, _LANES):
                    a = acc[:, c:c + _LANES]
                    inv = lax.rsqrt(jnp.mean(a * a, axis=-1, keepdims=True) + _EPS)
                    if rope:
                        y = a * tab_refs[2 * gi][...] + pltpu.roll(a, _LANES // 2, 1) * tab_refs[2 * gi + 1][...]
                    else:
                        y = a * gain_refs[gi][...]
                    o_ref[:, c:c + _LANES] = (y * inv).astype(o_ref.dtype)


def _proj(X, g, sh, sc, W, layouts, geo, bn, out_dtype, rope=None, gains=()):
    t, d = X.shape
    n = W.shape[1]
    bm = geo["bm"]
    nrt = t // bm
    mrow = geo["mrow"]
    in_specs = [
        pl.BlockSpec((bm, d), lambda i, j: (i, 0)),
        pl.BlockSpec((1, d), lambda i, j: (0, 0)),
        pl.BlockSpec((1, 1, d), lambda i, j: (mrow(i), 0, 0)),
        pl.BlockSpec((1, 1, d), lambda i, j: (mrow(i), 0, 0)),
        pl.BlockSpec((d, bn), lambda i, j: (0, j)),
    ]
    args = [X, g, sh, sc, W]
    if rope is not None:
        tps, nxt = geo["tiles_per_seq"], geo["n_x_tiles"]
        tab = lambda i, j: (jnp.where(i < nxt, i % tps, tps), 0)
        in_specs += [pl.BlockSpec((bm, _LANES), tab)] * 4
        args += list(rope)
    for gn in gains:
        in_specs.append(pl.BlockSpec((1, _LANES), lambda i, j: (0, 0)))
        args.append(gn)
    return pl.pallas_call(
        functools.partial(_proj_body, layouts, rope is not None, len(gains)),
        grid=(nrt, n // bn),
        in_specs=in_specs,
        out_specs=pl.BlockSpec((bm, bn), lambda i, j: (i, j)),
        out_shape=jax.ShapeDtypeStruct((t, n), out_dtype),
        scratch_shapes=[pltpu.VMEM((bm, d), _BF)],
        compiler_params=_cp("arbitrary", "arbitrary"),
        name="proj",
    )(*args)


def _wide_tile(n, cap=1280):
    return max(b for b in range(_LANES, cap + 1, _LANES) if n % b == 0)


def _qkv_layouts(n_q, n_kv, bn, rope):
    hd = _LANES
    bounds = [(0, n_q * hd, "q"), (n_q * hd, (n_q + n_kv) * hd, "k"), ((n_q + n_kv) * hd, (n_q + 2 * n_kv) * hd, "v")]
    n = (n_q + 2 * n_kv) * hd
    per_tile = []
    for jt in range(n // bn):
        lo, hi = jt * bn, (jt + 1) * bn
        segs = []
        for b0, b1, nm in bounds:
            s0, s1 = max(lo, b0), min(hi, b1)
            if s0 < s1:
                if nm == "v":
                    segs.append((s0 - lo, s1 - lo, "plain", 0, False, 1.0))
                else:
                    segs.append((s0 - lo, s1 - lo, "head", 0 if nm == "q" else 1, rope, 1.0))
        per_tile.append(tuple(segs))
    layouts = []
    for jt, segs in enumerate(per_tile):
        if layouts and layouts[-1][2] == segs and layouts[-1][1] == jt:
            layouts[-1] = (layouts[-1][0], jt + 1, segs)
        else:
            layouts.append((jt, jt + 1, segs))
    return tuple(layouts)


def _oproj_body(a_ref, w_ref, x_ref, gt_ref, o_ref):
    o_ref[...] = x_ref[...] + gt_ref[0] * _dot(a_ref[...], w_ref[...])


def _oproj(A, W, X, gate, geo, nrt, bm, bn):
    k = A.shape[1]
    d = W.shape[1]
    mrow = functools.partial(geo["mrow_bm"], bm)
    return pl.pallas_call(
        _oproj_body,
        grid=(nrt, d // bn),
        in_specs=[
            pl.BlockSpec((bm, k), lambda i, j: (i, 0)),
            pl.BlockSpec((k, bn), lambda i, j: (0, j)),
            pl.BlockSpec((bm, bn), lambda i, j: (i, j)),
            pl.BlockSpec((1, 1, bn), lambda i, j: (mrow(i), 0, j)),
        ],
        out_specs=pl.BlockSpec((bm, bn), lambda i, j: (i, j)),
        out_shape=jax.ShapeDtypeStruct((nrt * bm, d), _F32),
        compiler_params=_cp("arbitrary", "arbitrary"),
        name="oproj",
    )(A, W, X, gate)


def _ffn1_body(bm, n_x_rows, seq, ctx_len, x_ref, xp_ref, xn_ref, g_ref, sh_ref, sc_ref,
               wg_ref, wu_ref, cw_ref, cb_ref, o_ref, h_ref):
    i = pl.program_id(0)
    j = pl.program_id(1)

    @pl.when(j == 0)
    def _():
        g, sh, sc = g_ref[...], sh_ref[0], sc_ref[0]
        h_ref[0:bm, :] = _norm_mod(x_ref[...], g, sh, sc).astype(_BF)
        halo = jnp.concatenate([xp_ref[...], xn_ref[...]], axis=0)
        h_ref[bm:bm + 2 * _HALO, :] = _norm_mod(halo, g, sh, sc).astype(_BF)

    gx = _dot(h_ref[...], wg_ref[...])
    u = _dot(h_ref[0:bm, :], wu_ref[...])
    gm = gx[0:bm]
    g_prev = gx[bm + _HALO - 1:bm + _HALO]
    g_next = gx[bm + _HALO:bm + _HALO + 1]
    row = lax.broadcasted_iota(jnp.int32, (bm, 1), 0)
    tok = i * bm + row
    period = jnp.where(i * bm >= n_x_rows, ctx_len, seq)
    up = jnp.where(row == 0, g_prev, pltpu.roll(gm, 1, 0))
    dn = jnp.where(row == bm - 1, g_next, pltpu.roll(gm, bm - 1, 0))
    up = jnp.where((tok & (period - 1)) != 0, up, 0.0)
    dn = jnp.where(((tok + 1) & (period - 1)) != 0, dn, 0.0)
    cw = cw_ref[...]
    gc = cb_ref[...] + up * cw[0:1] + gm * cw[1:2] + dn * cw[2:3]
    o_ref[...] = (jax.nn.gelu(gc) * u).astype(o_ref.dtype)


def _ffn1(X, g, sh, sc, w_in, conv_w, conv_b, geo, nrt, bf):
    t, d = X.shape
    f = conv_w.shape[1]
    bm = geo["bm"]
    mrow = geo["mrow"]
    hb = bm // _HALO
    last = t // _HALO - 1
    nf = f // bf
    body = functools.partial(_ffn1_body, bm, geo["n_x_rows"], geo["seq"], geo["ctx_len"])
    return pl.pallas_call(
        body,
        grid=(nrt, nf),
        in_specs=[
            pl.BlockSpec((bm, d), lambda i, j: (i, 0)),
            pl.BlockSpec((_HALO, d), lambda i, j: (jnp.maximum(i * hb - 1, 0), 0)),
            pl.BlockSpec((_HALO, d), lambda i, j: (jnp.minimum((i + 1) * hb, last), 0)),
            pl.BlockSpec((1, d), lambda i, j: (0, 0)),
            pl.BlockSpec((1, 1, d), lambda i, j: (mrow(i), 0, 0)),
            pl.BlockSpec((1, 1, d), lambda i, j: (mrow(i), 0, 0)),
            pl.BlockSpec((d, bf), lambda i, j: (0, j)),
            pl.BlockSpec((d, bf), lambda i, j: (0, j + nf)),
            pl.BlockSpec((3, bf), lambda i, j: (0, j)),
            pl.BlockSpec((1, bf), lambda i, j: (0, j)),
        ],
        out_specs=pl.BlockSpec((bm, bf), lambda i, j: (i, j)),
        out_shape=jax.ShapeDtypeStruct((nrt * bm, f), _BF),
        scratch_shapes=[pltpu.VMEM((bm + 2 * _HALO, d), _BF)],
        compiler_params=_cp("arbitrary", "arbitrary"),
        name="ffn1",
    )(X, X, X, g, sh, sc, w_in, w_in, conv_w, conv_b)


def _with_ones(v):
    return jnp.concatenate([v, jnp.ones(v.shape, v.dtype)], axis=1)


def _lane_tiles(x, n):
    return jnp.concatenate([x] * n, axis=1) if n > 1 else x


def _na_body(seq, sub, kwin, hps, q_ref, k_ref, v_ref, kc_ref, vc_ref, ba_ref, bb_ref, o_ref):
    j = pl.program_id(2)
    for hh in range(hps):
        c = hh * _LANES
        kc = kc_ref[:, c:c + _LANES]
        vcx = _with_ones(vc_ref[:, c:c + _LANES])
        for i, b_ref in enumerate((ba_ref, bb_ref)):
            q0 = (2 * j + i) * sub
            ks = pl.multiple_of(jnp.clip(q0 - (kwin - sub) // 2, 0, seq - kwin), 256)
            q = q_ref[i * sub:(i + 1) * sub, c:c + _LANES]
            k = k_ref[pl.ds(ks, kwin), c:c + _LANES]
            v = v_ref[pl.ds(ks, kwin), c:c + _LANES]
            s = _dot_nt(q, k) + b_ref[0, hh]
            sc = _dot_nt(q, kc)
            m = jnp.maximum(jnp.max(s, axis=-1, keepdims=True), jnp.max(sc, axis=-1, keepdims=True))
            p = jnp.exp2(s - m).astype(_BF)
            pc = jnp.exp2(sc - m).astype(_BF)
            acc = _dot(p, _with_ones(v)) + _dot(pc, vcx)
            o_ref[i * sub:(i + 1) * sub, c:c + _LANES] = (acc[:, :_LANES] / acc[:, _LANES:]).astype(o_ref.dtype)


def _na_bias_table(rel_bias, rows, rq, rk):
    h, nr2, nc2 = rel_bias.shape
    na_rows, na_cols = (nr2 + 1) // 2, (nc2 + 1) // 2
    kr = min(na_rows, rows)
    w = _GRID_W
    nblk = rows // rq
    c = np.arange(w)
    c0 = np.clip(c - na_cols // 2, 0, w - na_cols)
    col_ok = (c[None, :] >= c0[:, None]) & (c[None, :] < c0[:, None] + na_cols)
    rbp = jnp.pad(rel_bias.astype(_F32) * _LOG2E, ((0, 0), (0, 0), (w, w)))
    tcol = jnp.stack([rbp[:, :, na_cols - 1 - qc + w:na_cols - 1 - qc + 2 * w] for qc in range(w)], axis=2)
    tcol = jnp.where(jnp.asarray(col_ok)[None, None], tcol, _NEG)
    masked = jnp.full((h, w, w), _NEG, _F32)
    tabs = []
    for jb in (0, min(1, nblk - 1), nblk - 1):
        kb0 = int(np.clip(jb * rq - (rk - rq) // 2, 0, rows - rk))
        bands = []
        for qi in range(rq):
            r = jb * rq + qi
            r0 = int(np.clip(r - kr // 2, 0, rows - kr))
            blocks = []
            for ki in range(rk):
                krow = kb0 + ki
                blocks.append(tcol[:, krow - r + na_rows - 1] if r0 <= krow < r0 + kr else masked)
            bands.append(jnp.concatenate(blocks, axis=-1))
        tabs.append(jnp.concatenate(bands, axis=1))
    return jnp.stack(tabs, axis=0)


def _na_attention(QKV, bias_tab, geo, n_heads):
    b, s, lc = geo["batch"], geo["seq"], geo["ctx_len"]
    t = QKV.shape[0]
    sub, kwin = bias_tab.shape[2], bias_tab.shape[3]
    bq = 2 * sub
    nj = s // bq
    cblk = geo["n_x_rows"] // lc
    hps = 2
    hw = hps * _LANES
    ng = n_heads // hps
    return pl.pallas_call(
        functools.partial(_na_body, s, sub, kwin, hps),
        grid=(b, ng, nj),
        in_specs=[
            pl.BlockSpec((bq, hw), lambda bi, h, j: (bi * nj + j, h)),
            pl.BlockSpec((s, hw), lambda bi, h, j: (bi, ng + h)),
            pl.BlockSpec((s, hw), lambda bi, h, j: (bi, 2 * ng + h)),
            pl.BlockSpec((lc, hw), lambda bi, h, j: (cblk + bi, ng + h)),
            pl.BlockSpec((lc, hw), lambda bi, h, j: (cblk + bi, 2 * ng + h)),
            pl.BlockSpec((1, hps, sub, kwin), lambda bi, h, j: (jnp.where(j == 0, 0, 1), h, 0, 0)),
            pl.BlockSpec((1, hps, sub, kwin), lambda bi, h, j: (jnp.where(j == nj - 1, 2, 1), h, 0, 0)),
        ],
        out_specs=pl.BlockSpec((bq, hw), lambda bi, h, j: (bi * nj + j, h)),
        out_shape=jax.ShapeDtypeStruct((t, n_heads * _LANES), _BF),
        compiler_params=_cp("arbitrary", "arbitrary", "arbitrary"),
        name="na_attn",
    )(QKV, QKV, QKV, QKV, QKV, bias_tab, bias_tab)


def _ctx_attn_body(group, has_sink, *refs):
    if has_sink:
        q_ref, k_ref, v_ref, sk_ref, _, o_ref = refs
    else:
        q_ref, k_ref, v_ref, _, o_ref = refs
    k = k_ref[...]
    vx = _with_ones(v_ref[...])
    for g in range(group):
        c = g * _LANES
        s = _dot_nt(q_ref[:, c:c + _LANES], k)
        m = jnp.max(s, axis=-1, keepdims=True)
        if has_sink:
            sk = sk_ref[0][:, c:c + 1]
            m = jnp.maximum(m, sk)
        acc = _dot(jnp.exp2(s - m).astype(_BF), vx)
        l = acc[:, _LANES:]
        if has_sink:
            l = l + jnp.exp2(sk - m)
        o_ref[:, c:c + _LANES] = (acc[:, :_LANES] / l).astype(o_ref.dtype)


def _ctx_attention(QKV, O, geo, n_heads, n_kv, sink=None):
    b, lc = geo["batch"], geo["ctx_len"]
    group = n_heads // n_kv
    gw = group * _LANES
    cblk = geo["n_x_rows"] // lc
    in_specs = [
        pl.BlockSpec((lc, gw), lambda bi, n: (cblk + bi, n)),
        pl.BlockSpec((lc, _LANES), lambda bi, n: (cblk + bi, n_heads + n)),
        pl.BlockSpec((lc, _LANES), lambda bi, n: (cblk + bi, n_heads + n_kv + n)),
    ]
    args = [QKV, QKV, QKV]
    if sink is not None:
        in_specs.append(pl.BlockSpec((1, 1, gw), lambda bi, n: (n, 0, 0)))
        args.append(sink)
    in_specs.append(pl.BlockSpec(memory_space=pl.ANY))
    args.append(O)
    return pl.pallas_call(
        functools.partial(_ctx_attn_body, group, sink is not None),
        grid=(b, n_kv),
        in_specs=in_specs,
        out_specs=pl.BlockSpec((lc, gw), lambda bi, n: (cblk + bi, n)),
        out_shape=jax.ShapeDtypeStruct(O.shape, O.dtype),
        input_output_aliases={len(args) - 1: 0},
        compiler_params=_cp("arbitrary", "arbitrary"),
        name="ctx_attn",
    )(*args)


def _swa_body(seq, bq, win, group, q_ref, k_ref, v_ref, kc_ref, vc_ref, sk_ref, o_ref):
    t = pl.program_id(2)
    kw = bq + 2 * win
    ks = jnp.clip(t * bq - win, 0, seq - kw)
    ks = pl.multiple_of(ks, _LANES)
    k = k_ref[pl.ds(ks, kw), :]
    v = v_ref[pl.ds(ks, kw), :]
    kc = kc_ref[...]
    vx = _with_ones(v)
    vcx = _with_ones(vc_ref[...])
    qpos = t * bq + lax.broadcasted_iota(jnp.int32, (bq, 1), 0)
    kpos = ks + lax.broadcasted_iota(jnp.int32, (1, kw), 1)
    band = jnp.abs(kpos - qpos) <= win
    for g in range(group):
        c = g * _LANES
        q = q_ref[:, c:c + _LANES]
        s = jnp.where(band, _dot_nt(q, k), _NEG)
        sc = _dot_nt(q, kc)
        sk = sk_ref[0][:, c:c + 1]
        m = jnp.maximum(jnp.maximum(jnp.max(s, axis=-1, keepdims=True), jnp.max(sc, axis=-1, keepdims=True)), sk)
        p = jnp.exp2(s - m).astype(_BF)
        pc = jnp.exp2(sc - m).astype(_BF)
        acc = _dot(p, vx) + _dot(pc, vcx)
        l = acc[:, _LANES:] + jnp.exp2(sk - m)
        o_ref[:, c:c + _LANES] = (acc[:, :_LANES] / l).astype(o_ref.dtype)


def _swa_attention(QKV, sink, geo, n_heads, n_kv, bq):
    b, s, lc = geo["batch"], geo["seq"], geo["ctx_len"]
    t = QKV.shape[0]
    group = n_heads // n_kv
    gw = group * _LANES
    nq = s // bq
    cblk = geo["n_x_rows"] // lc
    return pl.pallas_call(
        functools.partial(_swa_body, s, bq, _SWA_WINDOW, group),
        grid=(b, n_kv, nq),
        in_specs=[
            pl.BlockSpec((bq, gw), lambda bi, n, j: (bi * nq + j, n)),
            pl.BlockSpec((s, _LANES), lambda bi, n, j: (bi, n_heads + n)),
            pl.BlockSpec((s, _LANES), lambda bi, n, j: (bi, n_heads + n_kv + n)),
            pl.BlockSpec((lc, _LANES), lambda bi, n, j: (cblk + bi, n_heads + n)),
            pl.BlockSpec((lc, _LANES), lambda bi, n, j: (cblk + bi, n_heads + n_kv + n)),
            pl.BlockSpec((1, 1, gw), lambda bi, n, j: (n, 0, 0)),
        ],
        out_specs=pl.BlockSpec((bq, gw), lambda bi, n, j: (bi * nq + j, n)),
        out_shape=jax.ShapeDtypeStruct((t, n_heads * _LANES), _BF),
        compiler_params=_cp("arbitrary", "arbitrary", "arbitrary"),
        name="swa_attn",
    )(QKV, QKV, QKV, QKV, QKV, sink)


def _gqa_body(seq, ck, group, q_ref, k_ref, v_ref, kc_ref, vc_ref, o_ref, m_ref, acc_ref):
    m_ref[...] = jnp.full(m_ref.shape, _NEG, _F32)
    acc_ref[...] = jnp.zeros(acc_ref.shape, _F32)

    def step(k, v):
        width = k.shape[0]
        vx = jnp.concatenate([v, jnp.ones((width, _LANES), v.dtype)], axis=1)
        for g in range(group):
            s = _dot_nt(q_ref[:, g * _LANES:(g + 1) * _LANES], k)
            m_old = m_ref[g]
            m_new = jnp.maximum(m_old, jnp.max(s, axis=-1, keepdims=True))
            alpha = jnp.exp2(m_old - m_new)
            p = jnp.exp2(s - _lane_tiles(m_new, width // _LANES)).astype(_BF)
            acc_ref[g] = _lane_tiles(alpha, 2) * acc_ref[g] + _dot(p, vx)
            m_ref[g] = m_new

    def chunk(c, carry):
        off = pl.multiple_of(c * ck, ck)
        step(k_ref[pl.ds(off, ck), :], v_ref[pl.ds(off, ck), :])
        return carry

    lax.fori_loop(0, seq // ck, chunk, 0)
    step(kc_ref[...], vc_ref[...])
    for g in range(group):
        acc = acc_ref[g]
        o_ref[:, g * _LANES:(g + 1) * _LANES] = (acc[:, :_LANES] / acc[:, _LANES:]).astype(o_ref.dtype)


def _gqa_attention(QKV, geo, n_heads, n_kv, bq, ck):
    b, s, lc = geo["batch"], geo["seq"], geo["ctx_len"]
    group = n_heads // n_kv
    gw = group * _LANES
    nq = s // bq
    cblk = geo["n_x_rows"] // lc
    return pl.pallas_call(
        functools.partial(_gqa_body, s, ck, group),
        grid=(b, n_kv, nq),
        in_specs=[
            pl.BlockSpec((bq, gw), lambda bi, n, j: (bi * nq + j, n)),
            pl.BlockSpec((s, _LANES), lambda bi, n, j: (bi, n_heads + n)),
            pl.BlockSpec((s, _LANES), lambda bi, n, j: (bi, n_heads + n_kv + n)),
            pl.BlockSpec((lc, _LANES), lambda bi, n, j: (cblk + bi, n_heads + n)),
            pl.BlockSpec((lc, _LANES), lambda bi, n, j: (cblk + bi, n_heads + n_kv + n)),
        ],
        out_specs=pl.BlockSpec((bq, gw), lambda bi, n, j: (bi * nq + j, n)),
        out_shape=jax.ShapeDtypeStruct((geo["n_x_rows"], n_heads * _LANES), _BF),
        scratch_shapes=[pltpu.VMEM((group, bq, _LANES), _F32), pltpu.VMEM((group, bq, 2 * _LANES), _F32)],
        compiler_params=_cp("arbitrary", "arbitrary", "arbitrary"),
        name="gqa_attn",
    )(QKV, QKV, QKV, QKV, QKV)


def _log_sigmoid(x):
    return jnp.minimum(x, 0.0) - jnp.log1p(jnp.exp(-jnp.abs(x)))


def _mlstm_body(n_heads, chunk, q_ref, k_ref, v_ref, gt_ref, gb_ref, o_ref, c_ref, n_ref, m_ref):
    h = pl.program_id(1)
    d = pl.program_id(2)
    c = pl.program_id(3)

    @pl.when(c == 0)
    def _():
        c_ref[...] = jnp.zeros(c_ref.shape, _F32)
        n_ref[...] = jnp.zeros(n_ref.shape, _F32)
        m_ref[...] = jnp.zeros(m_ref.shape, _F32)

    L = chunk
    gates = gt_ref[...] + gb_ref[...]
    lane = lax.broadcasted_iota(jnp.int32, (1, _LANES), 1)
    li_col = jnp.sum(jnp.where(lane == (2 * d) * n_heads + h, gates, 0.0), axis=-1, keepdims=True)
    lf_pre = jnp.sum(jnp.where(lane == (2 * d + 1) * n_heads + h, gates, 0.0), axis=-1, keepdims=True)
    lf_col = _log_sigmoid(lf_pre)
    ti = lax.broadcasted_iota(jnp.int32, (L, L), 0)
    si = lax.broadcasted_iota(jnp.int32, (L, L), 1)
    eye = ti == si
    li_row = jnp.sum(jnp.where(eye, li_col, 0.0), axis=0, keepdims=True)
    lf_row = jnp.sum(jnp.where(eye, lf_col, 0.0), axis=0, keepdims=True)
    sign = 1 - 2 * d
    allowed = (ti - si) * sign >= 0
    allowed_t = (si - ti) * sign >= 0
    b_col = jnp.sum(jnp.where(allowed, lf_row, 0.0), axis=-1, keepdims=True)
    b_row = jnp.sum(jnp.where(allowed_t, lf_col, 0.0), axis=0, keepdims=True)
    total = jnp.sum(lf_col, axis=0, keepdims=True)
    m_prev = m_ref[0:1, 0:1]

    q = q_ref[...]
    k = k_ref[...]
    v = v_ref[...]
    dmat = jnp.where(allowed, b_col - b_row + li_row, _NEG)
    g_col = b_col + m_prev
    m_t = jnp.maximum(g_col, jnp.max(dmat, axis=-1, keepdims=True))
    w = jnp.exp(dmat - m_t) * _dot_nt(q, k)
    w_prev = jnp.exp(g_col - m_t)
    cmat = c_ref[...]
    nvec = n_ref[0:1, :]
    num = _dot(w.astype(_BF), v) + w_prev * _dot(q, cmat.astype(_BF))
    qn = jnp.sum(q.astype(_F32) * nvec, axis=-1, keepdims=True)
    den = jnp.sum(w, axis=-1, keepdims=True) + w_prev * qn
    o_ref[0] = (num / jnp.maximum(jnp.abs(den), jnp.exp(-m_t))).astype(o_ref.dtype)

    lw = total - b_col + li_col
    m_new = jnp.maximum(total + m_prev, jnp.max(lw, axis=0, keepdims=True))
    decay = jnp.exp(total + m_prev - m_new)
    kw = k.astype(_F32) * jnp.exp(lw - m_new)
    c_ref[...] = decay * cmat + _dot_tn(kw.astype(_BF), v)
    n_ref[...] = jnp.broadcast_to(decay * nvec + jnp.sum(kw, axis=0, keepdims=True), n_ref.shape)
    m_ref[...] = jnp.broadcast_to(m_new, m_ref.shape)


def _mlstm_scan(P, G, gate_b, geo, n_heads, dqk, dv, chunk):
    b, s, lc = geo["batch"], geo["seq"], geo["ctx_len"]
    t = P.shape[0]
    nctx, nx = lc // chunk, s // chunk
    cbase = geo["n_x_rows"] // chunk
    qb, vb = dqk // _LANES, dv // _LANES

    def rblk(bi, d, c):
        cc = c - nctx
        in_ctx = cbase + bi * nctx + jnp.where(d == 0, c, nctx - 1 - c)
        in_x = bi * nx + jnp.where(d == 0, cc, nx - 1 - cc)
        return jnp.where(c < nctx, in_ctx, in_x)

    k_off = n_heads * dqk // dqk
    v_off = 2 * n_heads * dqk // dv
    return pl.pallas_call(
        functools.partial(_mlstm_body, n_heads, chunk),
        grid=(b, n_heads, 2, nctx + nx),
        in_specs=[
            pl.BlockSpec((chunk, dqk), lambda bi, h, d, c: (rblk(bi, d, c), h)),
            pl.BlockSpec((chunk, dqk), lambda bi, h, d, c: (rblk(bi, d, c), k_off + h)),
            pl.BlockSpec((chunk, dv), lambda bi, h, d, c: (rblk(bi, d, c), v_off + h)),
            pl.BlockSpec((chunk, _LANES), lambda bi, h, d, c: (rblk(bi, d, c), 0)),
            pl.BlockSpec((1, _LANES), lambda bi, h, d, c: (0, 0)),
        ],
        out_specs=pl.BlockSpec((1, chunk, dv), lambda bi, h, d, c: (d, rblk(bi, d, c), h)),
        out_shape=jax.ShapeDtypeStruct((2, t, n_heads * dv), _F32),
        scratch_shapes=[pltpu.VMEM((dqk, dv), _F32), pltpu.VMEM((8, dqk), _F32), pltpu.VMEM((8, _LANES), _F32)],
        compiler_params=_cp("arbitrary", "arbitrary", "arbitrary", "arbitrary"),
        name="mlstm_scan",
    )(P, P, P, G, gate_b)


def _ml_out_body(n_heads, dv, hs_ref, og_ref, hg_ref, w_ref, x_ref, gt_ref, o_ref):
    hsum = hs_ref[0] + hs_ref[1]
    parts = []
    for hh in range(n_heads):
        a = hsum[:, hh * dv:(hh + 1) * dv]
        hn = a * lax.rsqrt(jnp.mean(a * a, axis=-1, keepdims=True) + _EPS) * hg_ref[:, hh * dv:(hh + 1) * dv]
        og = og_ref[:, hh * dv:(hh + 1) * dv].astype(_F32)
        parts.append((jax.nn.sigmoid(og) * hn).astype(_BF))
    a = jnp.concatenate(parts, axis=-1)
    o_ref[...] = x_ref[...] + gt_ref[0] * _dot(a, w_ref[...])


def _ml_out(Hs, P, head_g, W, X, gate, geo, n_heads, dv, bm):
    t, d = X.shape
    kd = n_heads * dv
    o_off = (P.shape[1] - kd) // kd
    mrow = functools.partial(geo["mrow_bm"], bm)
    return pl.pallas_call(
        functools.partial(_ml_out_body, n_heads, dv),
        grid=(t // bm,),
        in_specs=[
            pl.BlockSpec((2, bm, kd), lambda i: (0, i, 0)),
            pl.BlockSpec((bm, kd), lambda i: (i, o_off)),
            pl.BlockSpec((1, kd), lambda i: (0, 0)),
            pl.BlockSpec((kd, d), lambda i: (0, 0)),
            pl.BlockSpec((bm, d), lambda i: (i, 0)),
            pl.BlockSpec((1, 1, d), lambda i: (mrow(i), 0, 0)),
        ],
        out_specs=pl.BlockSpec((bm, d), lambda i: (i, 0)),
        out_shape=jax.ShapeDtypeStruct((t, d), _F32),
        compiler_params=_cp("arbitrary"),
        name="ml_out",
    )(Hs, P, head_g, W, X, gate)


def _rope_tables(seq, pad_rows):
    tpos = np.arange(seq)
    row = (tpos // _GRID_W).astype(np.float32)
    col = (tpos % _GRID_W).astype(np.float32)
    nf = _LANES // 4
    inv = (_ROPE_BASE ** (-jnp.arange(nf, dtype=_F32) / nf))
    ar = jnp.asarray(row)[:, None] * inv
    ac = jnp.asarray(col)[:, None] * inv
    cr, sr, cc, sc = jnp.cos(ar), jnp.sin(ar), jnp.cos(ac), jnp.sin(ac)
    cs = jnp.concatenate([cr, cc, cr, cc], axis=1)
    sn = jnp.concatenate([-sr, -sc, sr, sc], axis=1)
    cs = jnp.concatenate([cs, jnp.ones((pad_rows, _LANES), _F32)], axis=0)
    sn = jnp.concatenate([sn, jnp.zeros((pad_rows, _LANES), _F32)], axis=0)
    return cs, sn


def _pair_major(v):
    q = _LANES // 4
    return v.reshape(v.shape[:-1] + (v.shape[-1] // _LANES, 2, 2, q)).swapaxes(-3, -2).reshape(v.shape)


def _rope_operands(w_qkv, q_gain, k_gain, n_qk_heads, cs, sn):
    nqk = n_qk_heads * _LANES
    w = jnp.concatenate([_pair_major(w_qkv[:, :nqk]), w_qkv[:, nqk:]], axis=1).astype(_BF)
    tabs = []
    for gain in (q_gain, k_gain):
        gp = _pair_major(gain.astype(_F32))
        tabs += [cs * gp[None, :], sn * jnp.roll(gp, _LANES // 2)[None, :]]
    return w, tuple(tabs)


def _geometry(batch, seq, ctx_len):
    bm = min(1024, batch * ctx_len)
    assert seq % bm == 0 and (batch * ctx_len) % bm == 0
    assert seq & (seq - 1) == 0 and ctx_len & (ctx_len - 1) == 0
    n_x_rows = batch * seq

    def mrow_bm(bm_, i):
        return jnp.minimum((i * bm_) // seq, batch)

    return dict(batch=batch, seq=seq, ctx_len=ctx_len, bm=bm, n_x_rows=n_x_rows,
                n_x_tiles=n_x_rows // bm, tiles_per_seq=seq // bm,
                mrow=functools.partial(mrow_bm, bm), mrow_bm=mrow_bm)


def kernel(x, c, ctx, c_ctx, ada_w, ada_b, norm1_g, norm2_g, ffn_w_in, ffn_conv_w, ffn_conv_b, ffn_w_out,
           na_w_qkv, na_q_g, na_k_g, na_rel_bias, na_w_o,
           swa_w_qkv, swa_q_g, swa_k_g, swa_sinks, swa_w_o,
           ml_w_in, ml_gate_b, ml_head_g, ml_w_o,
           gqa_w_qkv, gqa_q_g, gqa_k_g, gqa_w_o):
    batch, seq, d = x.shape
    ctx_len = ctx.shape[1]
    depth = ada_w.shape[0]
    n_heads = d // _LANES
    geo = _geometry(batch, seq, ctx_len)
    bm = geo["bm"]
    n_x_rows = geo["n_x_rows"]
    n_x_tiles = geo["n_x_tiles"]
    qscale = (_LANES ** -0.5) * _LOG2E

    X = jnp.concatenate([x.reshape(n_x_rows, d), ctx.reshape(batch * ctx_len, d)], axis=0)
    t_all = X.shape[0]
    nrt_all = t_all // bm

    cond = jnp.concatenate([c, c_ctx[None, :], jnp.zeros((8 - batch - 1, d), _F32)], axis=0)
    mods = _adaln(cond, ada_w, ada_b)

    cs, sn = _rope_tables(seq, bm)
    f = ffn_conv_w.shape[2]
    bf = 512
    bm_o = min(512, bm)

    for i in range(depth):
        kind, jl = i % 4, i // 4
        need_ctx = i < depth - 1
        mod = [mods[i, :batch + 1, k * d:(k + 1) * d].reshape(batch + 1, 1, d) for k in range(6)]
        sh1, sc1, g1, sh2, sc2, g2 = mod
        n1 = norm1_g[i].reshape(1, d)
        n2 = norm2_g[i].reshape(1, d)
        nrt_o = t_all // bm_o if need_ctx else n_x_rows // bm_o

        if kind == 0:
            w = na_w_qkv[jl].astype(_BF)
            gq = (na_q_g[jl] * qscale).reshape(1, _LANES)
            gk = na_k_g[jl].reshape(1, _LANES)
            bn = 1024
            QKV = _proj(X, n1, sh1, sc1, w, _qkv_layouts(n_heads, n_heads, bn, False), geo, bn, _BF, gains=(gq, gk))
            rows = seq // _GRID_W
            tab = _na_bias_table(na_rel_bias[jl], rows, 4, 12)
            O = _na_attention(QKV, tab, geo, n_heads)
            if need_ctx:
                O = _ctx_attention(QKV, O, geo, n_heads, n_heads)
            X = _oproj(O, na_w_o[jl].astype(_BF), X, g1, geo, nrt_o, bm_o, d)
        elif kind == 1:
            n_kv = (swa_w_qkv.shape[2] // _LANES - n_heads) // 2
            w, tabs = _rope_operands(swa_w_qkv[jl], swa_q_g[jl] * qscale, swa_k_g[jl], n_heads + n_kv, cs, sn)
            bn = _wide_tile(w.shape[1])
            QKV = _proj(X, n1, sh1, sc1, w, _qkv_layouts(n_heads, n_kv, bn, True), geo, bn, _BF, rope=tabs)
            group = n_heads // n_kv
            sink = jnp.repeat(swa_sinks[jl].astype(_F32) * _LOG2E, _LANES).reshape(n_kv, 1, group * _LANES)
            O = _swa_attention(QKV, sink, geo, n_heads, n_kv, 256)
            if need_ctx:
                O = _ctx_attention(QKV, O, geo, n_heads, n_kv, sink)
            X = _oproj(O, swa_w_o[jl].astype(_BF), X, g1, geo, nrt_o, bm_o, d)
        elif kind == 2:
            mh = ml_gate_b.shape[1] // 4
            dv = d // mh
            dqk = dv // 2
            nmain = 2 * mh * dqk + 2 * mh * dv
            w_main = ml_w_in[jl][:, :nmain].astype(_BF)
            w_gate = jnp.pad(ml_w_in[jl][:, nmain:], ((0, 0), (0, _LANES - 4 * mh))).astype(_BF)
            bn = 1024
            segs_by_tile = []
            k0, k1 = mh * dqk, 2 * mh * dqk
            for jt in range(nmain // bn):
                lo, hi = jt * bn, (jt + 1) * bn
                cuts = sorted({lo, hi, min(max(k0, lo), hi), min(max(k1, lo), hi)})
                segs = tuple((a - lo, b_ - lo, "plain", 0, False, dqk ** -0.5 if k0 <= a < k1 else 1.0)
                             for a, b_ in zip(cuts[:-1], cuts[1:]))
                segs_by_tile.append((jt, jt + 1, segs))
            P = _proj(X, n1, sh1, sc1, w_main, tuple(segs_by_tile), geo, bn, _BF)
            G = _proj(X, n1, sh1, sc1, w_gate, ((0, 1, ((0, _LANES, "plain", 0, False, 1.0),)),), geo, _LANES, _F32)
            gb = jnp.pad(ml_gate_b[jl].astype(_F32), (0, _LANES - 4 * mh)).reshape(1, _LANES)
            Hs = _mlstm_scan(P, G, gb, geo, mh, dqk, dv, 256)
            X = _ml_out(Hs, P, ml_head_g[jl].reshape(1, mh * dv), ml_w_o[jl].astype(_BF), X, g1, geo, mh, dv, bm_o)
        else:
            n_kv = (gqa_w_qkv.shape[2] // _LANES - n_heads) // 2
            w, tabs = _rope_operands(gqa_w_qkv[jl], gqa_q_g[jl] * qscale, gqa_k_g[jl], n_heads + n_kv, cs, sn)
            bn = _wide_tile(w.shape[1])
            QKV = _proj(X, n1, sh1, sc1, w, _qkv_layouts(n_heads, n_kv, bn, True), geo, bn, _BF, rope=tabs)
            O = _gqa_attention(QKV, geo, n_heads, n_kv, min(1024, seq), min(1024, seq))
            if need_ctx:
                Oc = jnp.zeros((t_all, d), _BF).at[:n_x_rows].set(O)
                O = _ctx_attention(QKV, Oc, geo, n_heads, n_kv)
            X = _oproj(O, gqa_w_o[jl].astype(_BF), X, g1, geo, nrt_o, bm_o, d)

        nrt_f = nrt_all if need_ctx else n_x_tiles
        A = _ffn1(X, n2, sh2, sc2, ffn_w_in[i].astype(_BF), ffn_conv_w[i], ffn_conv_b[i].reshape(1, f), geo, nrt_f, bf)
        X = _oproj(A, ffn_w_out[i].astype(_BF), X, g2, geo, A.shape[0] // bm_o, bm_o, 1024)

    return X[:n_x_rows].reshape(batch, seq, d)
```

```python
import functools
import math

import numpy as np
import jax
import jax.numpy as jnp
from jax import lax
from jax.experimental import pallas as pl
from jax.experimental.pallas import tpu as pltpu

_F32 = jnp.float32
_BF = jnp.bfloat16
_EPS = 1e-6
_NEG = -1e30
_LOG2E = 1.4426950408889634
_GRID_W = 64
_SWA_WINDOW = 128
_ROPE_BASE = 10000.0
_LANES = 128
_HALO = 8
_VMEM_LIMIT = 56 << 20


def _cp(*sem):
    return pltpu.CompilerParams(dimension_semantics=sem, vmem_limit_bytes=_VMEM_LIMIT)


def _dot(a, b):
    return jnp.dot(a, b, preferred_element_type=_F32)


def _dot_nt(a, b):
    return lax.dot_general(a, b, (((1,), (1,)), ((), ())), preferred_element_type=_F32)


def _dot_tn(a, b):
    return lax.dot_general(a, b, (((0,), (0,)), ((), ())), preferred_element_type=_F32)


def _norm_mod(x, g, sh, sc):
    ms = jnp.mean(x * x, axis=-1, keepdims=True)
    return (x * lax.rsqrt(ms + _EPS)) * (g * (1.0 + sc)) + sh


def _adaln_body(c_ref, w_ref, b_ref, o_ref):
    c = c_ref[...]
    s = (c * jax.nn.sigmoid(c)).astype(_BF)
    o_ref[0] = _dot(s, w_ref[0].astype(_BF)) + b_ref[0]


def _adaln(cond, ada_w, ada_b):
    depth, d, n = ada_w.shape
    bn = 1024
    return pl.pallas_call(
        _adaln_body,
        grid=(depth, n // bn),
        in_specs=[
            pl.BlockSpec((8, d), lambda l, j: (0, 0)),
            pl.BlockSpec((1, d, bn), lambda l, j: (l, 0, j)),
            pl.BlockSpec((1, 1, bn), lambda l, j: (l, 0, j)),
        ],
        out_specs=pl.BlockSpec((1, 8, bn), lambda l, j: (l, 0, j)),
        out_shape=jax.ShapeDtypeStruct((depth, 8, n), _F32),
        compiler_params=_cp("arbitrary", "arbitrary"),
        name="adaln",
    )(cond, ada_w, ada_b.reshape(depth, 1, n))


def _proj_body(layouts, has_rope, n_gain, *refs):
    x_ref, g_ref, sh_ref, sc_ref, w_ref = refs[:5]
    pos = 5
    tab_refs = ()
    if has_rope:
        tab_refs = refs[pos:pos + 4]
        pos += 4
    gain_refs = refs[pos:pos + n_gain]
    pos += n_gain
    o_ref, h_ref = refs[pos], refs[pos + 1]
    j = pl.program_id(1)

    @pl.when(j == 0)
    def _():
        h_ref[...] = _norm_mod(x_ref[...], g_ref[...], sh_ref[0], sc_ref[0]).astype(_BF)

    for lo, hi, segs in layouts:
        @pl.when((j >= lo) & (j < hi))
        def _(segs=segs):
            acc = _dot(h_ref[...], w_ref[...])
            for c0, c1, kind, gi, rope, mult in segs:
                if kind == "plain":
                    a = acc[:, c0:c1]
                    if mult != 1.0:
                        a = a * mult
                    o_ref[:, c0:c1] = a.astype(o_ref.dtype)
                    continue
                for c in range(c0, c1, _LANES):
                    a = acc[:, c:c + _LANES]
                    inv = lax.rsqrt(jnp.mean(a * a, axis=-1, keepdims=True) + _EPS)
                    if rope:
                        y = a * tab_refs[2 * gi][...] + pltpu.roll(a, _LANES // 2, 1) * tab_refs[2 * gi + 1][...]
                    else:
                        y = a * gain_refs[gi][...]
                    o_ref[:, c:c + _LANES] = (y * inv).astype(o_ref.dtype)


def _proj(X, g, sh, sc, W, layouts, geo, bn, out_dtype, rope=None, gains=()):
    t, d = X.shape
    n = W.shape[1]
    bm = geo["bm"]
    nrt = t // bm
    mrow = geo["mrow"]
    in_specs = [
        pl.BlockSpec((bm, d), lambda i, j: (i, 0)),
        pl.BlockSpec((1, d), lambda i, j: (0, 0)),
        pl.BlockSpec((1, 1, d), lambda i, j: (mrow(i), 0, 0)),
        pl.BlockSpec((1, 1, d), lambda i, j: (mrow(i), 0, 0)),
        pl.BlockSpec((d, bn), lambda i, j: (0, j)),
    ]
    args = [X, g, sh, sc, W]
    if rope is not None:
        tps, nxt = geo["tiles_per_seq"], geo["n_x_tiles"]
        tab = lambda i, j: (jnp.where(i < nxt, i % tps, tps), 0)
        in_specs += [pl.BlockSpec((bm, _LANES), tab)] * 4
        args += list(rope)
    for gn in gains:
        in_specs.append(pl.BlockSpec((1, _LANES), lambda i, j: (0, 0)))
        args.append(gn)
    return pl.pallas_call(
        functools.partial(_proj_body, layouts, rope is not None, len(gains)),
        grid=(nrt, n // bn),
        in_specs=in_specs,
        out_specs=pl.BlockSpec((bm, bn), lambda i, j: (i, j)),
        out_shape=jax.ShapeDtypeStruct((t, n), out_dtype),
        scratch_shapes=[pltpu.VMEM((bm, d), _BF)],
        compiler_params=_cp("arbitrary", "arbitrary"),
        name="proj",
    )(*args)


def _wide_tile(n, cap=1280):
    return max(b for b in range(_LANES, cap + 1, _LANES) if n % b == 0)


def _qkv_layouts(n_q, n_kv, bn, rope):
    hd = _LANES
    bounds = [(0, n_q * hd, "q"), (n_q * hd, (n_q + n_kv) * hd, "k"), ((n_q + n_kv) * hd, (n_q + 2 * n_kv) * hd, "v")]
    n = (n_q + 2 * n_kv) * hd
    per_tile = []
    for jt in range(n // bn):
        lo, hi = jt * bn, (jt + 1) * bn
        segs = []
        for b0, b1, nm in bounds:
            s0, s1 = max(lo, b0), min(hi, b1)
            if s0 < s1:
                if nm == "v":
                    segs.append((s0 - lo, s1 - lo, "plain", 0, False, 1.0))
                else:
                    segs.append((s0 - lo, s1 - lo, "head", 0 if nm == "q" else 1, rope, 1.0))
        per_tile.append(tuple(segs))
    layouts = []
    for jt, segs in enumerate(per_tile):
        if layouts and layouts[-1][2] == segs and layouts[-1][1] == jt:
            layouts[-1] = (layouts[-1][0], jt + 1, segs)
        else:
            layouts.append((jt, jt + 1, segs))
    return tuple(layouts)


def _oproj_body(a_ref, w_ref, x_ref, gt_ref, o_ref):
    o_ref[...] = x_ref[...] + gt_ref[0] * _dot(a_ref[...], w_ref[...])


def _oproj(A, W, X, gate, geo, nrt, bm, bn):
    k = A.shape[1]
    d = W.shape[1]
    mrow = functools.partial(geo["mrow_bm"], bm)
    return pl.pallas_call(
        _oproj_body,
        grid=(nrt, d // bn),
        in_specs=[
            pl.BlockSpec((bm, k), lambda i, j: (i, 0)),
            pl.BlockSpec((k, bn), lambda i, j: (0, j)),
            pl.BlockSpec((bm, bn), lambda i, j: (i, j)),
            pl.BlockSpec((1, 1, bn), lambda i, j: (mrow(i), 0, j)),
        ],
        out_specs=pl.BlockSpec((bm, bn), lambda i, j: (i, j)),
        out_shape=jax.ShapeDtypeStruct((nrt * bm, d), _F32),
        compiler_params=_cp("arbitrary", "arbitrary"),
        name="oproj",
    )(A, W, X, gate)


def _ffn1_body(bm, n_x_rows, seq, ctx_len, x_ref, xp_ref, xn_ref, g_ref, sh_ref, sc_ref,
               wg_ref, wu_ref, cw_ref, cb_ref, o_ref, h_ref):
    i = pl.program_id(0)
    j = pl.program_id(1)

    @pl.when(j == 0)
    def _():
        g, sh, sc = g_ref[...], sh_ref[0], sc_ref[0]
        h_ref[0:bm, :] = _norm_mod(x_ref[...], g, sh, sc).astype(_BF)
        halo = jnp.concatenate([xp_ref[...], xn_ref[...]], axis=0)
        h_ref[bm:bm + 2 * _HALO, :] = _norm_mod(halo, g, sh, sc).astype(_BF)

    gx = _dot(h_ref[...], wg_ref[...])
    u = _dot(h_ref[0:bm, :], wu_ref[...])
    gm = gx[0:bm]
    g_prev = gx[bm + _HALO - 1:bm + _HALO]
    g_next = gx[bm + _HALO:bm + _HALO + 1]
    row = lax.broadcasted_iota(jnp.int32, (bm, 1), 0)
    tok = i * bm + row
    period = jnp.where(i * bm >= n_x_rows, ctx_len, seq)
    up = jnp.where(row == 0, g_prev, pltpu.roll(gm, 1, 0))
    dn = jnp.where(row == bm - 1, g_next, pltpu.roll(gm, bm - 1, 0))
    up = jnp.where((tok & (period - 1)) != 0, up, 0.0)
    dn = jnp.where(((tok + 1) & (period - 1)) != 0, dn, 0.0)
    cw = cw_ref[...]
    gc = cb_ref[...] + up * cw[0:1] + gm * cw[1:2] + dn * cw[2:3]
    o_ref[...] = (jax.nn.gelu(gc) * u).astype(o_ref.dtype)


def _ffn1(X, g, sh, sc, w_in, conv_w, conv_b, geo, nrt, bf):
    t, d = X.shape
    f = conv_w.shape[1]
    bm = geo["bm"]
    mrow = geo["mrow"]
    hb = bm // _HALO
    last = t // _HALO - 1
    nf = f // bf
    body = functools.partial(_ffn1_body, bm, geo["n_x_rows"], geo["seq"], geo["ctx_len"])
    return pl.pallas_call(
        body,
        grid=(nrt, nf),
        in_specs=[
            pl.BlockSpec((bm, d), lambda i, j: (i, 0)),
            pl.BlockSpec((_HALO, d), lambda i, j: (jnp.maximum(i * hb - 1, 0), 0)),
            pl.BlockSpec((_HALO, d), lambda i, j: (jnp.minimum((i + 1) * hb, last), 0)),
            pl.BlockSpec((1, d), lambda i, j: (0, 0)),
            pl.BlockSpec((1, 1, d), lambda i, j: (mrow(i), 0, 0)),
            pl.BlockSpec((1, 1, d), lambda i, j: (mrow(i), 0, 0)),
            pl.BlockSpec((d, bf), lambda i, j: (0, j)),
            pl.BlockSpec((d, bf), lambda i, j: (0, j + nf)),
            pl.BlockSpec((3, bf), lambda i, j: (0, j)),
            pl.BlockSpec((1, bf), lambda i, j: (0, j)),
        ],
        out_specs=pl.BlockSpec((bm, bf), lambda i, j: (i, j)),
        out_shape=jax.ShapeDtypeStruct((nrt * bm, f), _BF),
        scratch_shapes=[pltpu.VMEM((bm + 2 * _HALO, d), _BF)],
        compiler_params=_cp("arbitrary", "arbitrary"),
        name="ffn1",
    )(X, X, X, g, sh, sc, w_in, w_in, conv_w, conv_b)


def _with_ones(v):
    return jnp.concatenate([v, jnp.ones(v.shape, v.dtype)], axis=1)


def _lane_tiles(x, n):
    return jnp.concatenate([x] * n, axis=1) if n > 1 else x


def _na_body(seq, sub, kwin, hps, q_ref, k_ref, v_ref, kc_ref, vc_ref, ba_ref, bb_ref, o_ref):
    j = pl.program_id(2)
    for hh in range(hps):
        c = hh * _LANES
        kc = kc_ref[:, c:c + _LANES]
        vcx = _with_ones(vc_ref[:, c:c + _LANES])
        for i, b_ref in enumerate((ba_ref, bb_ref)):
            q0 = (2 * j + i) * sub
            ks = pl.multiple_of(jnp.clip(q0 - (kwin - sub) // 2, 0, seq - kwin), 256)
            q = q_ref[i * sub:(i + 1) * sub, c:c + _LANES]
            k = k_ref[pl.ds(ks, kwin), c:c + _LANES]
            v = v_ref[pl.ds(ks, kwin), c:c + _LANES]
            s = _dot_nt(q, k) + b_ref[0, hh]
            sc = _dot_nt(q, kc)
            m = jnp.maximum(jnp.max(s, axis=-1, keepdims=True), jnp.max(sc, axis=-1, keepdims=True))
            p = jnp.exp2(s - m).astype(_BF)
            pc = jnp.exp2(sc - m).astype(_BF)
            acc = _dot(p, _with_ones(v)) + _dot(pc, vcx)
            o_ref[i * sub:(i + 1) * sub, c:c + _LANES] = (acc[:, :_LANES] / acc[:, _LANES:]).astype(o_ref.dtype)


def _na_bias_table(rel_bias, rows, rq, rk):
    h, nr2, nc2 = rel_bias.shape
    na_rows, na_cols = (nr2 + 1) // 2, (nc2 + 1) // 2
    kr = min(na_rows, rows)
    w = _GRID_W
    nblk = rows // rq
    c = np.arange(w)
    c0 = np.clip(c - na_cols // 2, 0, w - na_cols)
    col_ok = (c[None, :] >= c0[:, None]) & (c[None, :] < c0[:, None] + na_cols)
    rbp = jnp.pad(rel_bias.astype(_F32) * _LOG2E, ((0, 0), (0, 0), (w, w)))
    tcol = jnp.stack([rbp[:, :, na_cols - 1 - qc + w:na_cols - 1 - qc + 2 * w] for qc in range(w)], axis=2)
    tcol = jnp.where(jnp.asarray(col_ok)[None, None], tcol, _NEG)
    masked = jnp.full((h, w, w), _NEG, _F32)
    tabs = []
    for jb in (0, min(1, nblk - 1), nblk - 1):
        kb0 = int(np.clip(jb * rq - (rk - rq) // 2, 0, rows - rk))
        bands = []
        for qi in range(rq):
            r = jb * rq + qi
            r0 = int(np.clip(r - kr // 2, 0, rows - kr))
            blocks = []
            for ki in range(rk):
                krow = kb0 + ki
                blocks.append(tcol[:, krow - r + na_rows - 1] if r0 <= krow < r0 + kr else masked)
            bands.append(jnp.concatenate(blocks, axis=-1))
        tabs.append(jnp.concatenate(bands, axis=1))
    return jnp.stack(tabs, axis=0)


def _na_attention(QKV, bias_tab, geo, n_heads):
    b, s, lc = geo["batch"], geo["seq"], geo["ctx_len"]
    t = QKV.shape[0]
    sub, kwin = bias_tab.shape[2], bias_tab.shape[3]
    bq = 2 * sub
    nj = s // bq
    cblk = geo["n_x_rows"] // lc
    hps = 2
    hw = hps * _LANES
    ng = n_heads // hps
    return pl.pallas_call(
        functools.partial(_na_body, s, sub, kwin, hps),
        grid=(b, ng, nj),
        in_specs=[
            pl.BlockSpec((bq, hw), lambda bi, h, j: (bi * nj + j, h)),
            pl.BlockSpec((s, hw), lambda bi, h, j: (bi, ng + h)),
            pl.BlockSpec((s, hw), lambda bi, h, j: (bi, 2 * ng + h)),
            pl.BlockSpec((lc, hw), lambda bi, h, j: (cblk + bi, ng + h)),
            pl.BlockSpec((lc, hw), lambda bi, h, j: (cblk + bi, 2 * ng + h)),
            pl.BlockSpec((1, hps, sub, kwin), lambda bi, h, j: (jnp.where(j == 0, 0, 1), h, 0, 0)),
            pl.BlockSpec((1, hps, sub, kwin), lambda bi, h, j: (jnp.where(j == nj - 1, 2, 1), h, 0, 0)),
        ],
        out_specs=pl.BlockSpec((bq, hw), lambda bi, h, j: (bi * nj + j, h)),
        out_shape=jax.ShapeDtypeStruct((t, n_heads * _LANES), _BF),
        compiler_params=_cp("arbitrary", "arbitrary", "arbitrary"),
        name="na_attn",
    )(QKV, QKV, QKV, QKV, QKV, bias_tab, bias_tab)


def _ctx_attn_body(group, has_sink, *refs):
    if has_sink:
        q_ref, k_ref, v_ref, sk_ref, _, o_ref = refs
    else:
        q_ref, k_ref, v_ref, _, o_ref = refs
    k = k_ref[...]
    vx = _with_ones(v_ref[...])
    for g in range(group):
        c = g * _LANES
        s = _dot_nt(q_ref[:, c:c + _LANES], k)
        m = jnp.max(s, axis=-1, keepdims=True)
        if has_sink:
            sk = sk_ref[0][:, c:c + 1]
            m = jnp.maximum(m, sk)
        acc = _dot(jnp.exp2(s - m).astype(_BF), vx)
        l = acc[:, _LANES:]
        if has_sink:
            l = l + jnp.exp2(sk - m)
        o_ref[:, c:c + _LANES] = (acc[:, :_LANES] / l).astype(o_ref.dtype)


def _ctx_attention(QKV, O, geo, n_heads, n_kv, sink=None):
    b, lc = geo["batch"], geo["ctx_len"]
    group = n_heads // n_kv
    gw = group * _LANES
    cblk = geo["n_x_rows"] // lc
    in_specs = [
        pl.BlockSpec((lc, gw), lambda bi, n: (cblk + bi, n)),
        pl.BlockSpec((lc, _LANES), lambda bi, n: (cblk + bi, n_heads + n)),
        pl.BlockSpec((lc, _LANES), lambda bi, n: (cblk + bi, n_heads + n_kv + n)),
    ]
    args = [QKV, QKV, QKV]
    if sink is not None:
        in_specs.append(pl.BlockSpec((1, 1, gw), lambda bi, n: (n, 0, 0)))
        args.append(sink)
    in_specs.append(pl.BlockSpec(memory_space=pl.ANY))
    args.append(O)
    return pl.pallas_call(
        functools.partial(_ctx_attn_body, group, sink is not None),
        grid=(b, n_kv),
        in_specs=in_specs,
        out_specs=pl.BlockSpec((lc, gw), lambda bi, n: (cblk + bi, n)),
        out_shape=jax.ShapeDtypeStruct(O.shape, O.dtype),
        input_output_aliases={len(args) - 1: 0},
        compiler_params=_cp("arbitrary", "arbitrary"),
        name="ctx_attn",
    )(*args)


def _swa_body(seq, bq, win, group, q_ref, k_ref, v_ref, kc_ref, vc_ref, sk_ref, o_ref):
    t = pl.program_id(2)
    kw = bq + 2 * win
    ks = jnp.clip(t * bq - win, 0, seq - kw)
    ks = pl.multiple_of(ks, _LANES)
    k = k_ref[pl.ds(ks, kw), :]
    v = v_ref[pl.ds(ks, kw), :]
    kc = kc_ref[...]
    vx = _with_ones(v)
    vcx = _with_ones(vc_ref[...])
    qpos = t * bq + lax.broadcasted_iota(jnp.int32, (bq, 1), 0)
    kpos = ks + lax.broadcasted_iota(jnp.int32, (1, kw), 1)
    band = jnp.abs(kpos - qpos) <= win
    for g in range(group):
        c = g * _LANES
        q = q_ref[:, c:c + _LANES]
        s = jnp.where(band, _dot_nt(q, k), _NEG)
        sc = _dot_nt(q, kc)
        sk = sk_ref[0][:, c:c + 1]
        m = jnp.maximum(jnp.maximum(jnp.max(s, axis=-1, keepdims=True), jnp.max(sc, axis=-1, keepdims=True)), sk)
        p = jnp.exp2(s - m).astype(_BF)
        pc = jnp.exp2(sc - m).astype(_BF)
        acc = _dot(p, vx) + _dot(pc, vcx)
        l = acc[:, _LANES:] + jnp.exp2(sk - m)
        o_ref[:, c:c + _LANES] = (acc[:, :_LANES] / l).astype(o_ref.dtype)


def _swa_attention(QKV, sink, geo, n_heads, n_kv, bq):
    b, s, lc = geo["batch"], geo["seq"], geo["ctx_len"]
    t = QKV.shape[0]
    group = n_heads // n_kv
    gw = group * _LANES
    nq = s // bq
    cblk = geo["n_x_rows"] // lc
    return pl.pallas_call(
        functools.partial(_swa_body, s, bq, _SWA_WINDOW, group),
        grid=(b, n_kv, nq),
        in_specs=[
            pl.BlockSpec((bq, gw), lambda bi, n, j: (bi * nq + j, n)),
            pl.BlockSpec((s, _LANES), lambda bi, n, j: (bi, n_heads + n)),
            pl.BlockSpec((s, _LANES), lambda bi, n, j: (bi, n_heads + n_kv + n)),
            pl.BlockSpec((lc, _LANES), lambda bi, n, j: (cblk + bi, n_heads + n)),
            pl.BlockSpec((lc, _LANES), lambda bi, n, j: (cblk + bi, n_heads + n_kv + n)),
            pl.BlockSpec((1, 1, gw), lambda bi, n, j: (n, 0, 0)),
        ],
        out_specs=pl.BlockSpec((bq, gw), lambda bi, n, j: (bi * nq + j, n)),
        out_shape=jax.ShapeDtypeStruct((t, n_heads * _LANES), _BF),
        compiler_params=_cp("arbitrary", "arbitrary", "arbitrary"),
        name="swa_attn",
    )(QKV, QKV, QKV, QKV, QKV, sink)


def _gqa_body(seq, ck, group, q_ref, k_ref, v_ref, kc_ref, vc_ref, o_ref, m_ref, acc_ref):
    m_ref[...] = jnp.full(m_ref.shape, _NEG, _F32)
    acc_ref[...] = jnp.zeros(acc_ref.shape, _F32)

    def step(k, v):
        width = k.shape[0]
        vx = jnp.concatenate([v, jnp.ones((width, _LANES), v.dtype)], axis=1)
        for g in range(group):
            s = _dot_nt(q_ref[:, g * _LANES:(g + 1) * _LANES], k)
            m_old = m_ref[g]
            m_new = jnp.maximum(m_old, jnp.max(s, axis=-1, keepdims=True))
            alpha = jnp.exp2(m_old - m_new)
            p = jnp.exp2(s - _lane_tiles(m_new, width // _LANES)).astype(_BF)
            acc_ref[g] = _lane_tiles(alpha, 2) * acc_ref[g] + _dot(p, vx)
            m_ref[g] = m_new

    def chunk(c, carry):
        off = pl.multiple_of(c * ck, ck)
        step(k_ref[pl.ds(off, ck), :], v_ref[pl.ds(off, ck), :])
        return carry

    lax.fori_loop(0, seq // ck, chunk, 0, unroll=min(4, seq // ck))
    step(kc_ref[...], vc_ref[...])
    for g in range(group):
        acc = acc_ref[g]
        o_ref[:, g * _LANES:(g + 1) * _LANES] = (acc[:, :_LANES] / acc[:, _LANES:]).astype(o_ref.dtype)


def _gqa_attention(QKV, geo, n_heads, n_kv, bq, ck):
    b, s, lc = geo["batch"], geo["seq"], geo["ctx_len"]
    group = n_heads // n_kv
    gw = group * _LANES
    nq = s // bq
    cblk = geo["n_x_rows"] // lc
    return pl.pallas_call(
        functools.partial(_gqa_body, s, ck, group),
        grid=(b, n_kv, nq),
        in_specs=[
            pl.BlockSpec((bq, gw), lambda bi, n, j: (bi * nq + j, n)),
            pl.BlockSpec((s, _LANES), lambda bi, n, j: (bi, n_heads + n)),
            pl.BlockSpec((s, _LANES), lambda bi, n, j: (bi, n_heads + n_kv + n)),
            pl.BlockSpec((lc, _LANES), lambda bi, n, j: (cblk + bi, n_heads + n)),
            pl.BlockSpec((lc, _LANES), lambda bi, n, j: (cblk + bi, n_heads + n_kv + n)),
        ],
        out_specs=pl.BlockSpec((bq, gw), lambda bi, n, j: (bi * nq + j, n)),
        out_shape=jax.ShapeDtypeStruct((geo["n_x_rows"], n_heads * _LANES), _BF),
        scratch_shapes=[pltpu.VMEM((group, bq, _LANES), _F32), pltpu.VMEM((group, bq, 2 * _LANES), _F32)],
        compiler_params=_cp("arbitrary", "arbitrary", "arbitrary"),
        name="gqa_attn",
    )(QKV, QKV, QKV, QKV, QKV)


def _log_sigmoid(x):
    return jnp.minimum(x, 0.0) - jnp.log1p(jnp.exp(-jnp.abs(x)))


def _mlstm_chunk(d, n_heads, L, h, q_ref, k_ref, v_ref, gt_ref, gb_ref, o_ref, c_ref, n_ref, m_ref):
    gates = gt_ref[...] + gb_ref[...]
    lane = lax.broadcasted_iota(jnp.int32, (1, _LANES), 1)
    li_col = jnp.sum(jnp.where(lane == (2 * d) * n_heads + h, gates, 0.0), axis=-1, keepdims=True)
    lf_pre = jnp.sum(jnp.where(lane == (2 * d + 1) * n_heads + h, gates, 0.0), axis=-1, keepdims=True)
    lf_col = _log_sigmoid(lf_pre)
    ti = lax.broadcasted_iota(jnp.int32, (L, L), 0)
    si = lax.broadcasted_iota(jnp.int32, (L, L), 1)
    eye = ti == si
    li_row = jnp.sum(jnp.where(eye, li_col, 0.0), axis=0, keepdims=True)
    lf_row = jnp.sum(jnp.where(eye, lf_col, 0.0), axis=0, keepdims=True)
    allowed = (ti >= si) if d == 0 else (ti <= si)
    allowed_t = (si >= ti) if d == 0 else (si <= ti)
    b_col = jnp.sum(jnp.where(allowed, lf_row, 0.0), axis=-1, keepdims=True)
    b_row = jnp.sum(jnp.where(allowed_t, lf_col, 0.0), axis=0, keepdims=True)
    total = jnp.sum(lf_col, axis=0, keepdims=True)
    m_prev = m_ref[d, 0:1, 0:1]

    q = q_ref[...]
    k = k_ref[...]
    v = v_ref[...]
    dmat = jnp.where(allowed, b_col - b_row + li_row, _NEG)
    g_col = b_col + m_prev
    m_t = jnp.maximum(g_col, jnp.max(dmat, axis=-1, keepdims=True))
    w = jnp.exp(dmat - m_t) * _dot_nt(q, k)
    w_prev = jnp.exp(g_col - m_t)
    cmat = c_ref[d]
    nvec = n_ref[d, 0:1, :]
    num = _dot(w.astype(_BF), v) + w_prev * _dot(q, cmat.astype(_BF))
    qn = jnp.sum(q.astype(_F32) * nvec, axis=-1, keepdims=True)
    den = jnp.sum(w, axis=-1, keepdims=True) + w_prev * qn
    o_ref[...] = (num / jnp.maximum(jnp.abs(den), jnp.exp(-m_t))).astype(o_ref.dtype)

    lw = total - b_col + li_col
    m_new = jnp.maximum(total + m_prev, jnp.max(lw, axis=0, keepdims=True))
    decay = jnp.exp(total + m_prev - m_new)
    kw = k.astype(_F32) * jnp.exp(lw - m_new)
    c_ref[d] = decay * cmat + _dot_tn(kw.astype(_BF), v)
    n_ref[d] = jnp.broadcast_to(decay * nvec + jnp.sum(kw, axis=0, keepdims=True), n_ref.shape[1:])
    m_ref[d] = jnp.broadcast_to(m_new, m_ref.shape[1:])


def _mlstm_body(n_heads, chunk, qf_ref, kf_ref, vf_ref, gf_ref, qb_ref, kb_ref, vb_ref, gbk_ref, gb_ref,
                of_ref, ob_ref, c_ref, n_ref, m_ref):
    h = pl.program_id(1)

    @pl.when(pl.program_id(2) == 0)
    def _():
        c_ref[...] = jnp.zeros(c_ref.shape, _F32)
        n_ref[...] = jnp.zeros(n_ref.shape, _F32)
        m_ref[...] = jnp.zeros(m_ref.shape, _F32)

    _mlstm_chunk(0, n_heads, chunk, h, qf_ref, kf_ref, vf_ref, gf_ref, gb_ref, of_ref, c_ref, n_ref, m_ref)
    _mlstm_chunk(1, n_heads, chunk, h, qb_ref, kb_ref, vb_ref, gbk_ref, gb_ref, ob_ref, c_ref, n_ref, m_ref)


def _mlstm_scan(P, G, gate_b, geo, n_heads, dqk, dv, chunk):
    b, s, lc = geo["batch"], geo["seq"], geo["ctx_len"]
    t = P.shape[0]
    nctx, nx = lc // chunk, s // chunk
    cbase = geo["n_x_rows"] // chunk

    def rblk(d, bi, c):
        cc = c - nctx
        in_ctx = cbase + bi * nctx + (c if d == 0 else nctx - 1 - c)
        in_x = bi * nx + (cc if d == 0 else nx - 1 - cc)
        return jnp.where(c < nctx, in_ctx, in_x)

    k_off = n_heads
    v_off = 2 * n_heads * dqk // dv
    in_specs, args = [], []
    for d in (0, 1):
        in_specs += [
            pl.BlockSpec((chunk, dqk), lambda bi, h, c, d=d: (rblk(d, bi, c), h)),
            pl.BlockSpec((chunk, dqk), lambda bi, h, c, d=d: (rblk(d, bi, c), k_off + h)),
            pl.BlockSpec((chunk, dv), lambda bi, h, c, d=d: (rblk(d, bi, c), v_off + h)),
            pl.BlockSpec((chunk, _LANES), lambda bi, h, c, d=d: (rblk(d, bi, c), 0)),
        ]
        args += [P, P, P, G]
    in_specs.append(pl.BlockSpec((1, _LANES), lambda bi, h, c: (0, 0)))
    args.append(gate_b)
    out_sds = jax.ShapeDtypeStruct((t, n_heads * dv), _F32)
    return pl.pallas_call(
        functools.partial(_mlstm_body, n_heads, chunk),
        grid=(b, n_heads, nctx + nx),
        in_specs=in_specs,
        out_specs=[pl.BlockSpec((chunk, dv), lambda bi, h, c, d=d: (rblk(d, bi, c), h)) for d in (0, 1)],
        out_shape=[out_sds, out_sds],
        scratch_shapes=[pltpu.VMEM((2, dqk, dv), _F32), pltpu.VMEM((2, 8, dqk), _F32), pltpu.VMEM((2, 8, _LANES), _F32)],
        compiler_params=_cp("arbitrary", "arbitrary", "arbitrary"),
        name="mlstm_scan",
    )(*args)


def _ml_out_body(n_heads, dv, hf_ref, hb_ref, og_ref, hg_ref, w_ref, x_ref, gt_ref, o_ref):
    hsum = hf_ref[...] + hb_ref[...]
    parts = []
    for hh in range(n_heads):
        a = hsum[:, hh * dv:(hh + 1) * dv]
        hn = a * lax.rsqrt(jnp.mean(a * a, axis=-1, keepdims=True) + _EPS) * hg_ref[:, hh * dv:(hh + 1) * dv]
        og = og_ref[:, hh * dv:(hh + 1) * dv].astype(_F32)
        parts.append((jax.nn.sigmoid(og) * hn).astype(_BF))
    a = jnp.concatenate(parts, axis=-1)
    o_ref[...] = x_ref[...] + gt_ref[0] * _dot(a, w_ref[...])


def _ml_out(Hf, Hb, P, head_g, W, X, gate, geo, n_heads, dv, bm):
    t, d = X.shape
    kd = n_heads * dv
    o_off = (P.shape[1] - kd) // kd
    mrow = functools.partial(geo["mrow_bm"], bm)
    return pl.pallas_call(
        functools.partial(_ml_out_body, n_heads, dv),
        grid=(t // bm,),
        in_specs=[
            pl.BlockSpec((bm, kd), lambda i: (i, 0)),
            pl.BlockSpec((bm, kd), lambda i: (i, 0)),
            pl.BlockSpec((bm, kd), lambda i: (i, o_off)),
            pl.BlockSpec((1, kd), lambda i: (0, 0)),
            pl.BlockSpec((kd, d), lambda i: (0, 0)),
            pl.BlockSpec((bm, d), lambda i: (i, 0)),
            pl.BlockSpec((1, 1, d), lambda i: (mrow(i), 0, 0)),
        ],
        out_specs=pl.BlockSpec((bm, d), lambda i: (i, 0)),
        out_shape=jax.ShapeDtypeStruct((t, d), _F32),
        compiler_params=_cp("arbitrary"),
        name="ml_out",
    )(Hf, Hb, P, head_g, W, X, gate)


def _rope_tables(seq, pad_rows):
    tpos = np.arange(seq)
    row = (tpos // _GRID_W).astype(np.float32)
    col = (tpos % _GRID_W).astype(np.float32)
    nf = _LANES // 4
    inv = (_ROPE_BASE ** (-jnp.arange(nf, dtype=_F32) / nf))
    ar = jnp.asarray(row)[:, None] * inv
    ac = jnp.asarray(col)[:, None] * inv
    cr, sr, cc, sc = jnp.cos(ar), jnp.sin(ar), jnp.cos(ac), jnp.sin(ac)
    cs = jnp.concatenate([cr, cc, cr, cc], axis=1)
    sn = jnp.concatenate([-sr, -sc, sr, sc], axis=1)
    cs = jnp.concatenate([cs, jnp.ones((pad_rows, _LANES), _F32)], axis=0)
    sn = jnp.concatenate([sn, jnp.zeros((pad_rows, _LANES), _F32)], axis=0)
    return cs, sn


def _pair_major(v):
    q = _LANES // 4
    return v.reshape(v.shape[:-1] + (v.shape[-1] // _LANES, 2, 2, q)).swapaxes(-3, -2).reshape(v.shape)


def _rope_operands(w_qkv, q_gain, k_gain, n_qk_heads, cs, sn):
    nqk = n_qk_heads * _LANES
    w = jnp.concatenate([_pair_major(w_qkv[:, :nqk]), w_qkv[:, nqk:]], axis=1).astype(_BF)
    tabs = []
    for gain in (q_gain, k_gain):
        gp = _pair_major(gain.astype(_F32))
        tabs += [cs * gp[None, :], sn * jnp.roll(gp, _LANES // 2)[None, :]]
    return w, tuple(tabs)


def _geometry(batch, seq, ctx_len):
    bm = min(1024, batch * ctx_len)
    assert seq % bm == 0 and (batch * ctx_len) % bm == 0
    assert seq & (seq - 1) == 0 and ctx_len & (ctx_len - 1) == 0
    n_x_rows = batch * seq

    def mrow_bm(bm_, i):
        return jnp.minimum((i * bm_) // seq, batch)

    return dict(batch=batch, seq=seq, ctx_len=ctx_len, bm=bm, n_x_rows=n_x_rows,
                n_x_tiles=n_x_rows // bm, tiles_per_seq=seq // bm,
                mrow=functools.partial(mrow_bm, bm), mrow_bm=mrow_bm)


def kernel(x, c, ctx, c_ctx, ada_w, ada_b, norm1_g, norm2_g, ffn_w_in, ffn_conv_w, ffn_conv_b, ffn_w_out,
           na_w_qkv, na_q_g, na_k_g, na_rel_bias, na_w_o,
           swa_w_qkv, swa_q_g, swa_k_g, swa_sinks, swa_w_o,
           ml_w_in, ml_gate_b, ml_head_g, ml_w_o,
           gqa_w_qkv, gqa_q_g, gqa_k_g, gqa_w_o):
    batch, seq, d = x.shape
    ctx_len = ctx.shape[1]
    depth = ada_w.shape[0]
    n_heads = d // _LANES
    geo = _geometry(batch, seq, ctx_len)
    bm = geo["bm"]
    n_x_rows = geo["n_x_rows"]
    n_x_tiles = geo["n_x_tiles"]
    qscale = (_LANES ** -0.5) * _LOG2E

    X = jnp.concatenate([x.reshape(n_x_rows, d), ctx.reshape(batch * ctx_len, d)], axis=0)
    t_all = X.shape[0]
    nrt_all = t_all // bm

    cond = jnp.concatenate([c, c_ctx[None, :], jnp.zeros((8 - batch - 1, d), _F32)], axis=0)
    mods = _adaln(cond, ada_w, ada_b)

    cs, sn = _rope_tables(seq, bm)
    f = ffn_conv_w.shape[2]
    bf = 512
    bm_o = min(512, bm)

    for i in range(depth):
        kind, jl = i % 4, i // 4
        need_ctx = i < depth - 1
        mod = [mods[i, :batch + 1, k * d:(k + 1) * d].reshape(batch + 1, 1, d) for k in range(6)]
        sh1, sc1, g1, sh2, sc2, g2 = mod
        n1 = norm1_g[i].reshape(1, d)
        n2 = norm2_g[i].reshape(1, d)
        nrt_o = t_all // bm_o if need_ctx else n_x_rows // bm_o

        if kind == 0:
            w = na_w_qkv[jl].astype(_BF)
            gq = (na_q_g[jl] * qscale).reshape(1, _LANES)
            gk = na_k_g[jl].reshape(1, _LANES)
            bn = 1024
            QKV = _proj(X, n1, sh1, sc1, w, _qkv_layouts(n_heads, n_heads, bn, False), geo, bn, _BF, gains=(gq, gk))
            rows = seq // _GRID_W
            tab = _na_bias_table(na_rel_bias[jl], rows, 4, 12)
            O = _na_attention(QKV, tab, geo, n_heads)
            if need_ctx:
                O = _ctx_attention(QKV, O, geo, n_heads, n_heads)
            X = _oproj(O, na_w_o[jl].astype(_BF), X, g1, geo, nrt_o, bm_o, d)
        elif kind == 1:
            n_kv = (swa_w_qkv.shape[2] // _LANES - n_heads) // 2
            w, tabs = _rope_operands(swa_w_qkv[jl], swa_q_g[jl] * qscale, swa_k_g[jl], n_heads + n_kv, cs, sn)
            bn = _wide_tile(w.shape[1])
            QKV = _proj(X, n1, sh1, sc1, w, _qkv_layouts(n_heads, n_kv, bn, True), geo, bn, _BF, rope=tabs)
            group = n_heads // n_kv
            sink = jnp.repeat(swa_sinks[jl].astype(_F32) * _LOG2E, _LANES).reshape(n_kv, 1, group * _LANES)
            O = _swa_attention(QKV, sink, geo, n_heads, n_kv, 256)
            if need_ctx:
                O = _ctx_attention(QKV, O, geo, n_heads, n_kv, sink)
            X = _oproj(O, swa_w_o[jl].astype(_BF), X, g1, geo, nrt_o, bm_o, d)
        elif kind == 2:
            mh = ml_gate_b.shape[1] // 4
            dv = d // mh
            dqk = dv // 2
            nmain = 2 * mh * dqk + 2 * mh * dv
            w_main = ml_w_in[jl][:, :nmain].astype(_BF)
            w_gate = jnp.pad(ml_w_in[jl][:, nmain:], ((0, 0), (0, _LANES - 4 * mh))).astype(_BF)
            bn = 1024
            segs_by_tile = []
            k0, k1 = mh * dqk, 2 * mh * dqk
            for jt in range(nmain // bn):
                lo, hi = jt * bn, (jt + 1) * bn
                cuts = sorted({lo, hi, min(max(k0, lo), hi), min(max(k1, lo), hi)})
                segs = tuple((a - lo, b_ - lo, "plain", 0, False, dqk ** -0.5 if k0 <= a < k1 else 1.0)
                             for a, b_ in zip(cuts[:-1], cuts[1:]))
                segs_by_tile.append((jt, jt + 1, segs))
            P = _proj(X, n1, sh1, sc1, w_main, tuple(segs_by_tile), geo, bn, _BF)
            G = _proj(X, n1, sh1, sc1, w_gate, ((0, 1, ((0, _LANES, "plain", 0, False, 1.0),)),), geo, _LANES, _F32)
            gb = jnp.pad(ml_gate_b[jl].astype(_F32), (0, _LANES - 4 * mh)).reshape(1, _LANES)
            Hf, Hb = _mlstm_scan(P, G, gb, geo, mh, dqk, dv, 256)
            X = _ml_out(Hf, Hb, P, ml_head_g[jl].reshape(1, mh * dv), ml_w_o[jl].astype(_BF), X, g1, geo, mh, dv, bm_o)
        else:
            n_kv = (gqa_w_qkv.shape[2] // _LANES - n_heads) // 2
            w, tabs = _rope_operands(gqa_w_qkv[jl], gqa_q_g[jl] * qscale, gqa_k_g[jl], n_heads + n_kv, cs, sn)
            bn = _wide_tile(w.shape[1])
            QKV = _proj(X, n1, sh1, sc1, w, _qkv_layouts(n_heads, n_kv, bn, True), geo, bn, _BF, rope=tabs)
            O = _gqa_attention(QKV, geo, n_heads, n_kv, min(1024, seq), 512)
            if need_ctx:
                Oc = jnp.zeros((t_all, d), _BF).at[:n_x_rows].set(O)
                O = _ctx_attention(QKV, Oc, geo, n_heads, n_kv)
            X = _oproj(O, gqa_w_o[jl].astype(_BF), X, g1, geo, nrt_o, bm_o, d)

        nrt_f = nrt_all if need_ctx else n_x_tiles
        A = _ffn1(X, n2, sh2, sc2, ffn_w_in[i].astype(_BF), ffn_conv_w[i], ffn_conv_b[i].reshape(1, f), geo, nrt_f, bf)
        X = _oproj(A, ffn_w_out[i].astype(_BF), X, g2, geo, A.shape[0] // bm, bm, 512)

    return X[:n_x_rows].reshape(batch, seq, d)
```

```python
import functools
import math

import numpy as np
import jax
import jax.numpy as jnp
from jax import lax
from jax.experimental import pallas as pl
from jax.experimental.pallas import tpu as pltpu

_F32 = jnp.float32
_BF = jnp.bfloat16
_EPS = 1e-6
_NEG = -1e30
_LOG2E = 1.4426950408889634
_GRID_W = 64
_SWA_WINDOW = 128
_ROPE_BASE = 10000.0
_LANES = 128
_HALO = 8
_VMEM_LIMIT = 56 << 20


def _cp(*sem):
    return pltpu.CompilerParams(dimension_semantics=sem, vmem_limit_bytes=_VMEM_LIMIT)


def _dot(a, b):
    return jnp.dot(a, b, preferred_element_type=_F32)


def _dot_nt(a, b):
    return lax.dot_general(a, b, (((1,), (1,)), ((), ())), preferred_element_type=_F32)


def _dot_tn(a, b):
    return lax.dot_general(a, b, (((0,), (0,)), ((), ())), preferred_element_type=_F32)


def _norm_mod(x, g, sh, sc):
    ms = jnp.mean(x * x, axis=-1, keepdims=True)
    return (x * lax.rsqrt(ms + _EPS)) * (g * (1.0 + sc)) + sh


def _adaln_body(c_ref, w_ref, b_ref, o_ref):
    c = c_ref[...]
    s = (c * jax.nn.sigmoid(c)).astype(_BF)
    o_ref[0] = _dot(s, w_ref[0].astype(_BF)) + b_ref[0]


def _adaln(cond, ada_w, ada_b):
    depth, d, n = ada_w.shape
    bn = 1024
    return pl.pallas_call(
        _adaln_body,
        grid=(depth, n // bn),
        in_specs=[
            pl.BlockSpec((8, d), lambda l, j: (0, 0)),
            pl.BlockSpec((1, d, bn), lambda l, j: (l, 0, j)),
            pl.BlockSpec((1, 1, bn), lambda l, j: (l, 0, j)),
        ],
        out_specs=pl.BlockSpec((1, 8, bn), lambda l, j: (l, 0, j)),
        out_shape=jax.ShapeDtypeStruct((depth, 8, n), _F32),
        compiler_params=_cp("arbitrary", "arbitrary"),
        name="adaln",
    )(cond, ada_w, ada_b.reshape(depth, 1, n))


def _next_rows_chunks(bm, nj):
    nc = 1
    while nc * 2 <= nj and bm // (nc * 2) >= _LANES:
        nc *= 2
    return nc, bm // nc


def _proj_body(layouts, has_rope, n_gain, *refs):
    x_ref, g_ref, sh_ref, sc_ref, w_ref = refs[:5]
    pos = 5
    tab_refs = ()
    if has_rope:
        tab_refs = refs[pos:pos + 4]
        pos += 4
    gain_refs = refs[pos:pos + n_gain]
    pos += n_gain
    o_ref, h_ref = refs[pos], refs[pos + 1]
    j = pl.program_id(1)

    @pl.when(j == 0)
    def _():
        h_ref[...] = _norm_mod(x_ref[...], g_ref[...], sh_ref[0], sc_ref[0]).astype(_BF)

    for lo, hi, segs in layouts:
        @pl.when((j >= lo) & (j < hi))
        def _(segs=segs):
            acc = _dot(h_ref[...], w_ref[...])
            for c0, c1, kind, gi, rope, mult in segs:
                if kind == "plain":
                    a = acc[:, c0:c1]
                    if mult != 1.0:
                        a = a * mult
                    o_ref[:, c0:c1] = a.astype(o_ref.dtype)
                    continue
                for c in range(c0, c1, _LANES):
                    a = acc[:, c:c + _LANES]
                    inv = lax.rsqrt(jnp.mean(a * a, axis=-1, keepdims=True) + _EPS)
                    if rope:
                        y = a * tab_refs[2 * gi][...] + pltpu.roll(a, _LANES // 2, 1) * tab_refs[2 * gi + 1][...]
                    else:
                        y = a * gain_refs[gi][...]
                    o_ref[:, c:c + _LANES] = (y * inv).astype(o_ref.dtype)


def _proj(X, g, sh, sc, W, layouts, geo, bn, out_dtype, rope=None, gains=()):
    t, d = X.shape
    n = W.shape[1]
    bm = geo["bm"]
    nrt = t // bm
    nj = n // bn
    mrow = geo["mrow"]
    in_specs = [
        pl.BlockSpec((bm, d), lambda i, j: (i, 0)),
        pl.BlockSpec((1, d), lambda i, j: (0, 0)),
        pl.BlockSpec((1, 1, d), lambda i, j: (mrow(i), 0, 0)),
        pl.BlockSpec((1, 1, d), lambda i, j: (mrow(i), 0, 0)),
        pl.BlockSpec((d, bn), lambda i, j: (0, j)),
    ]
    args = [X, g, sh, sc, W]
    if rope is not None:
        tps, nxt = geo["tiles_per_seq"], geo["n_x_tiles"]
        tab = lambda i, j: (jnp.where(i < nxt, i % tps, tps), 0)
        in_specs += [pl.BlockSpec((bm, _LANES), tab)] * 4
        args += list(rope)
    for gn in gains:
        in_specs.append(pl.BlockSpec((1, _LANES), lambda i, j: (0, 0)))
        args.append(gn)
    return pl.pallas_call(
        functools.partial(_proj_body, layouts, rope is not None, len(gains)),
        grid=(nrt, nj),
        in_specs=in_specs,
        out_specs=pl.BlockSpec((bm, bn), lambda i, j: (i, j)),
        out_shape=jax.ShapeDtypeStruct((t, n), out_dtype),
        scratch_shapes=[pltpu.VMEM((bm, d), _BF)],
        compiler_params=_cp("arbitrary", "arbitrary"),
        name="proj",
    )(*args)


def _wide_tile(n, cap=1280):
    return max(b for b in range(_LANES, cap + 1, _LANES) if n % b == 0)


def _qkv_layouts(n_q, n_kv, bn, rope):
    hd = _LANES
    bounds = [(0, n_q * hd, "q"), (n_q * hd, (n_q + n_kv) * hd, "k"), ((n_q + n_kv) * hd, (n_q + 2 * n_kv) * hd, "v")]
    n = (n_q + 2 * n_kv) * hd
    per_tile = []
    for jt in range(n // bn):
        lo, hi = jt * bn, (jt + 1) * bn
        segs = []
        for b0, b1, nm in bounds:
            s0, s1 = max(lo, b0), min(hi, b1)
            if s0 < s1:
                if nm == "v":
                    segs.append((s0 - lo, s1 - lo, "plain", 0, False, 1.0))
                else:
                    segs.append((s0 - lo, s1 - lo, "head", 0 if nm == "q" else 1, rope, 1.0))
        per_tile.append(tuple(segs))
    layouts = []
    for jt, segs in enumerate(per_tile):
        if layouts and layouts[-1][2] == segs and layouts[-1][1] == jt:
            layouts[-1] = (layouts[-1][0], jt + 1, segs)
        else:
            layouts.append((jt, jt + 1, segs))
    return tuple(layouts)


def _oproj_body(a_ref, w_ref, x_ref, gt_ref, o_ref):
    o_ref[...] = x_ref[...] + gt_ref[0] * _dot(a_ref[...], w_ref[...])


def _oproj(A, W, X, gate, geo, nrt, bm, bn, layer=None):
    k = A.shape[1]
    d = W.shape[-1]
    mrow = functools.partial(geo["mrow_bm"], bm)
    if layer is None:
        w_spec = pl.BlockSpec((k, bn), lambda i, j: (0, j))
    else:
        w_spec = pl.BlockSpec((None, k, bn), lambda i, j: (layer, 0, j))
    return pl.pallas_call(
        _oproj_body,
        grid=(nrt, d // bn),
        in_specs=[
            pl.BlockSpec((bm, k), lambda i, j: (i, 0)),
            w_spec,
            pl.BlockSpec((bm, bn), lambda i, j: (i, j)),
            pl.BlockSpec((1, 1, bn), lambda i, j: (mrow(i), 0, j)),
        ],
        out_specs=pl.BlockSpec((bm, bn), lambda i, j: (i, j)),
        out_shape=jax.ShapeDtypeStruct((nrt * bm, d), _F32),
        compiler_params=_cp("arbitrary", "arbitrary"),
        name="oproj",
    )(A, W, X, gate)


def _ffn1_body(bm, n_x_rows, seq, ctx_len, nc, rc, x0_ref, x0p_ref, x0n_ref, xn_ref, xnp_ref, xnn_ref,
               g_ref, sh_ref, sc_ref, shn_ref, scn_ref, wg_ref, wu_ref, cw_ref, cb_ref, o_ref, ha_ref, hb_ref):
    i = pl.program_id(0)
    j = pl.program_id(1)
    h_refs = (ha_ref, hb_ref)

    @pl.when((i == 0) & (j == 0))
    def _():
        g, sh, sc = g_ref[...], sh_ref[0], sc_ref[0]
        ha_ref[0:bm, :] = _norm_mod(x0_ref[...], g, sh, sc).astype(_BF)
        halo = jnp.concatenate([x0p_ref[...], x0n_ref[...]], axis=0)
        ha_ref[bm:bm + 2 * _HALO, :] = _norm_mod(halo, g, sh, sc).astype(_BF)

    for par in (0, 1):
        @pl.when(i % 2 == par)
        def _(par=par):
            h_ref, hn_ref = h_refs[par], h_refs[1 - par]
            gx = _dot(h_ref[...], wg_ref[...])
            u = _dot(h_ref[0:bm, :], wu_ref[...])
            g, shn, scn = g_ref[...], shn_ref[0], scn_ref[0]
            r0 = pl.multiple_of(jnp.minimum(j, nc - 1) * rc, rc)
            hn_ref[pl.ds(r0, rc), :] = _norm_mod(xn_ref[...], g, shn, scn).astype(_BF)
            halo = jnp.concatenate([xnp_ref[...], xnn_ref[...]], axis=0)
            hn_ref[bm:bm + 2 * _HALO, :] = _norm_mod(halo, g, shn, scn).astype(_BF)
            gm = gx[0:bm]
            g_prev = gx[bm + _HALO - 1:bm + _HALO]
            g_next = gx[bm + _HALO:bm + _HALO + 1]
            row = lax.broadcasted_iota(jnp.int32, (bm, 1), 0)
            tok = i * bm + row
            period = jnp.where(i * bm >= n_x_rows, ctx_len, seq)
            up = jnp.where(row == 0, g_prev, pltpu.roll(gm, 1, 0))
            dn = jnp.where(row == bm - 1, g_next, pltpu.roll(gm, bm - 1, 0))
            up = jnp.where((tok & (period - 1)) != 0, up, 0.0)
            dn = jnp.where(((tok + 1) & (period - 1)) != 0, dn, 0.0)
            cw = cw_ref[...]
            gc = cb_ref[...] + up * cw[0:1] + gm * cw[1:2] + dn * cw[2:3]
            o_ref[...] = (jax.nn.gelu(gc) * u).astype(o_ref.dtype)


def _ffn1(X, g, sh, sc, w_in, conv_w, conv_b, layer, geo, nrt, bf):
    t, d = X.shape
    f = conv_w.shape[2]
    bm = geo["bm"]
    mrow = geo["mrow"]
    hb = bm // _HALO
    last = t // _HALO - 1
    nf = f // bf
    nc, rc = _next_rows_chunks(bm, nf)
    nxt = lambda i: jnp.minimum(i + 1, nrt - 1)
    body = functools.partial(_ffn1_body, bm, geo["n_x_rows"], geo["seq"], geo["ctx_len"], nc, rc)
    return pl.pallas_call(
        body,
        grid=(nrt, nf),
        in_specs=[
            pl.BlockSpec((bm, d), lambda i, j: (0, 0)),
            pl.BlockSpec((_HALO, d), lambda i, j: (0, 0)),
            pl.BlockSpec((_HALO, d), lambda i, j: (jnp.minimum(hb, last), 0)),
            pl.BlockSpec((rc, d), lambda i, j: (nxt(i) * nc + jnp.minimum(j, nc - 1), 0)),
            pl.BlockSpec((_HALO, d), lambda i, j: (jnp.maximum(nxt(i) * hb - 1, 0), 0)),
            pl.BlockSpec((_HALO, d), lambda i, j: (jnp.minimum((nxt(i) + 1) * hb, last), 0)),
            pl.BlockSpec((1, d), lambda i, j: (0, 0)),
            pl.BlockSpec((1, 1, d), lambda i, j: (mrow(i), 0, 0)),
            pl.BlockSpec((1, 1, d), lambda i, j: (mrow(i), 0, 0)),
            pl.BlockSpec((1, 1, d), lambda i, j: (mrow(nxt(i)), 0, 0)),
            pl.BlockSpec((1, 1, d), lambda i, j: (mrow(nxt(i)), 0, 0)),
            pl.BlockSpec((None, d, bf), lambda i, j: (layer, 0, j)),
            pl.BlockSpec((None, d, bf), lambda i, j: (layer, 0, j + nf)),
            pl.BlockSpec((None, 3, bf), lambda i, j: (layer, 0, j)),
            pl.BlockSpec((None, 1, bf), lambda i, j: (layer, 0, j)),
        ],
        out_specs=pl.BlockSpec((bm, bf), lambda i, j: (i, j)),
        out_shape=jax.ShapeDtypeStruct((nrt * bm, f), _BF),
        scratch_shapes=[pltpu.VMEM((bm + 2 * _HALO, d), _BF), pltpu.VMEM((bm + 2 * _HALO, d), _BF)],
        compiler_params=_cp("arbitrary", "arbitrary"),
        name="ffn1",
    )(X, X, X, X, X, X, g, sh, sc, sh, sc, w_in, w_in, conv_w, conv_b)


def _with_ones(v):
    return jnp.concatenate([v, jnp.ones(v.shape, v.dtype)], axis=1)


def _lane_tiles(x, n):
    return jnp.concatenate([x] * n, axis=1) if n > 1 else x


def _na_body(seq, sub, kwin, hps, q_ref, k_ref, v_ref, kc_ref, vc_ref, ba_ref, bb_ref, o_ref):
    j = pl.program_id(2)
    for hh in range(hps):
        c = hh * _LANES
        kc = kc_ref[:, c:c + _LANES]
        vcx = _with_ones(vc_ref[:, c:c + _LANES])
        for i, b_ref in enumerate((ba_ref, bb_ref)):
            q0 = (2 * j + i) * sub
            ks = pl.multiple_of(jnp.clip(q0 - (kwin - sub) // 2, 0, seq - kwin), 256)
            q = q_ref[i * sub:(i + 1) * sub, c:c + _LANES]
            k = k_ref[pl.ds(ks, kwin), c:c + _LANES]
            v = v_ref[pl.ds(ks, kwin), c:c + _LANES]
            s = _dot_nt(q, k) + b_ref[0, hh]
            sc = _dot_nt(q, kc)
            m = jnp.maximum(jnp.max(s, axis=-1, keepdims=True), jnp.max(sc, axis=-1, keepdims=True))
            p = jnp.exp2(s - m).astype(_BF)
            pc = jnp.exp2(sc - m).astype(_BF)
            acc = _dot(p, _with_ones(v)) + _dot(pc, vcx)
            o_ref[i * sub:(i + 1) * sub, c:c + _LANES] = (acc[:, :_LANES] / acc[:, _LANES:]).astype(o_ref.dtype)


def _na_bias_table(rel_bias, rows, rq, rk):
    h, nr2, nc2 = rel_bias.shape
    na_rows, na_cols = (nr2 + 1) // 2, (nc2 + 1) // 2
    kr = min(na_rows, rows)
    w = _GRID_W
    nblk = rows // rq
    c = np.arange(w)
    c0 = np.clip(c - na_cols // 2, 0, w - na_cols)
    col_ok = (c[None, :] >= c0[:, None]) & (c[None, :] < c0[:, None] + na_cols)
    rbp = jnp.pad(rel_bias.astype(_F32) * _LOG2E, ((0, 0), (0, 0), (w, w)))
    tcol = jnp.stack([rbp[:, :, na_cols - 1 - qc + w:na_cols - 1 - qc + 2 * w] for qc in range(w)], axis=2)
    tcol = jnp.where(jnp.asarray(col_ok)[None, None], tcol, _NEG)
    masked = jnp.full((h, w, w), _NEG, _F32)
    tabs = []
    for jb in (0, min(1, nblk - 1), nblk - 1):
        kb0 = int(np.clip(jb * rq - (rk - rq) // 2, 0, rows - rk))
        bands = []
        for qi in range(rq):
            r = jb * rq + qi
            r0 = int(np.clip(r - kr // 2, 0, rows - kr))
            blocks = []
            for ki in range(rk):
                krow = kb0 + ki
                blocks.append(tcol[:, krow - r + na_rows - 1] if r0 <= krow < r0 + kr else masked)
            bands.append(jnp.concatenate(blocks, axis=-1))
        tabs.append(jnp.concatenate(bands, axis=1))
    return jnp.stack(tabs, axis=0)


def _na_attention(QKV, bias_tab, geo, n_heads):
    b, s, lc = geo["batch"], geo["seq"], geo["ctx_len"]
    t = QKV.shape[0]
    sub, kwin = bias_tab.shape[2], bias_tab.shape[3]
    bq = 2 * sub
    nj = s // bq
    cblk = geo["n_x_rows"] // lc
    hps = 2
    hw = hps * _LANES
    ng = n_heads // hps
    return pl.pallas_call(
        functools.partial(_na_body, s, sub, kwin, hps),
        grid=(b, ng, nj),
        in_specs=[
            pl.BlockSpec((bq, hw), lambda bi, h, j: (bi * nj + j, h)),
            pl.BlockSpec((s, hw), lambda bi, h, j: (bi, ng + h)),
            pl.BlockSpec((s, hw), lambda bi, h, j: (bi, 2 * ng + h)),
            pl.BlockSpec((lc, hw), lambda bi, h, j: (cblk + bi, ng + h)),
            pl.BlockSpec((lc, hw), lambda bi, h, j: (cblk + bi, 2 * ng + h)),
            pl.BlockSpec((1, hps, sub, kwin), lambda bi, h, j: (jnp.where(j == 0, 0, 1), h, 0, 0)),
            pl.BlockSpec((1, hps, sub, kwin), lambda bi, h, j: (jnp.where(j == nj - 1, 2, 1), h, 0, 0)),
        ],
        out_specs=pl.BlockSpec((bq, hw), lambda bi, h, j: (bi * nj + j, h)),
        out_shape=jax.ShapeDtypeStruct((t, n_heads * _LANES), _BF),
        compiler_params=_cp("arbitrary", "arbitrary", "arbitrary"),
        name="na_attn",
    )(QKV, QKV, QKV, QKV, QKV, bias_tab, bias_tab)


def _ctx_attn_body(group, has_sink, *refs):
    if has_sink:
        q_ref, k_ref, v_ref, sk_ref, _, o_ref = refs
    else:
        q_ref, k_ref, v_ref, _, o_ref = refs
    k = k_ref[...]
    vx = _with_ones(v_ref[...])
    for g in range(group):
        c = g * _LANES
        s = _dot_nt(q_ref[:, c:c + _LANES], k)
        m = jnp.max(s, axis=-1, keepdims=True)
        if has_sink:
            sk = sk_ref[0][:, c:c + 1]
            m = jnp.maximum(m, sk)
        acc = _dot(jnp.exp2(s - m).astype(_BF), vx)
        l = acc[:, _LANES:]
        if has_sink:
            l = l + jnp.exp2(sk - m)
        o_ref[:, c:c + _LANES] = (acc[:, :_LANES] / l).astype(o_ref.dtype)


def _ctx_attention(QKV, O, geo, n_heads, n_kv, sink=None):
    b, lc = geo["batch"], geo["ctx_len"]
    group = n_heads // n_kv
    gw = group * _LANES
    cblk = geo["n_x_rows"] // lc
    in_specs = [
        pl.BlockSpec((lc, gw), lambda bi, n: (cblk + bi, n)),
        pl.BlockSpec((lc, _LANES), lambda bi, n: (cblk + bi, n_heads + n)),
        pl.BlockSpec((lc, _LANES), lambda bi, n: (cblk + bi, n_heads + n_kv + n)),
    ]
    args = [QKV, QKV, QKV]
    if sink is not None:
        in_specs.append(pl.BlockSpec((1, 1, gw), lambda bi, n: (n, 0, 0)))
        args.append(sink)
    in_specs.append(pl.BlockSpec(memory_space=pl.ANY))
    args.append(O)
    return pl.pallas_call(
        functools.partial(_ctx_attn_body, group, sink is not None),
        grid=(b, n_kv),
        in_specs=in_specs,
        out_specs=pl.BlockSpec((lc, gw), lambda bi, n: (cblk + bi, n)),
        out_shape=jax.ShapeDtypeStruct(O.shape, O.dtype),
        input_output_aliases={len(args) - 1: 0},
        compiler_params=_cp("arbitrary", "arbitrary"),
        name="ctx_attn",
    )(*args)


def _swa_body(seq, bq, win, group, q_ref, k_ref, v_ref, kc_ref, vc_ref, sk_ref, o_ref):
    t = pl.program_id(2)
    kw = bq + 2 * win
    ks = jnp.clip(t * bq - win, 0, seq - kw)
    ks = pl.multiple_of(ks, _LANES)
    k = k_ref[pl.ds(ks, kw), :]
    v = v_ref[pl.ds(ks, kw), :]
    kc = kc_ref[...]
    vx = _with_ones(v)
    vcx = _with_ones(vc_ref[...])
    qpos = t * bq + lax.broadcasted_iota(jnp.int32, (bq, 1), 0)
    kpos = ks + lax.broadcasted_iota(jnp.int32, (1, kw), 1)
    band = jnp.abs(kpos - qpos) <= win
    for g in range(group):
        c = g * _LANES
        q = q_ref[:, c:c + _LANES]
        s = jnp.where(band, _dot_nt(q, k), _NEG)
        sc = _dot_nt(q, kc)
        sk = sk_ref[0][:, c:c + 1]
        m = jnp.maximum(jnp.maximum(jnp.max(s, axis=-1, keepdims=True), jnp.max(sc, axis=-1, keepdims=True)), sk)
        p = jnp.exp2(s - m).astype(_BF)
        pc = jnp.exp2(sc - m).astype(_BF)
        acc = _dot(p, vx) + _dot(pc, vcx)
        l = acc[:, _LANES:] + jnp.exp2(sk - m)
        o_ref[:, c:c + _LANES] = (acc[:, :_LANES] / l).astype(o_ref.dtype)


def _swa_attention(QKV, sink, geo, n_heads, n_kv, bq):
    b, s, lc = geo["batch"], geo["seq"], geo["ctx_len"]
    t = QKV.shape[0]
    group = n_heads // n_kv
    gw = group * _LANES
    nq = s // bq
    cblk = geo["n_x_rows"] // lc
    return pl.pallas_call(
        functools.partial(_swa_body, s, bq, _SWA_WINDOW, group),
        grid=(b, n_kv, nq),
        in_specs=[
            pl.BlockSpec((bq, gw), lambda bi, n, j: (bi * nq + j, n)),
            pl.BlockSpec((s, _LANES), lambda bi, n, j: (bi, n_heads + n)),
            pl.BlockSpec((s, _LANES), lambda bi, n, j: (bi, n_heads + n_kv + n)),
            pl.BlockSpec((lc, _LANES), lambda bi, n, j: (cblk + bi, n_heads + n)),
            pl.BlockSpec((lc, _LANES), lambda bi, n, j: (cblk + bi, n_heads + n_kv + n)),
            pl.BlockSpec((1, 1, gw), lambda bi, n, j: (n, 0, 0)),
        ],
        out_specs=pl.BlockSpec((bq, gw), lambda bi, n, j: (bi * nq + j, n)),
        out_shape=jax.ShapeDtypeStruct((t, n_heads * _LANES), _BF),
        compiler_params=_cp("arbitrary", "arbitrary", "arbitrary"),
        name="swa_attn",
    )(QKV, QKV, QKV, QKV, QKV, sink)


def _gqa_body(seq, ck, group, q_ref, k_ref, v_ref, kc_ref, vc_ref, o_ref, m_ref, acc_ref):
    m_ref[...] = jnp.full(m_ref.shape, _NEG, _F32)
    acc_ref[...] = jnp.zeros(acc_ref.shape, _F32)

    def step(k, v):
        width = k.shape[0]
        vx = jnp.concatenate([v, jnp.ones((width, _LANES), v.dtype)], axis=1)
        for g in range(group):
            s = _dot_nt(q_ref[:, g * _LANES:(g + 1) * _LANES], k)
            m_old = m_ref[g]
            m_new = jnp.maximum(m_old, jnp.max(s, axis=-1, keepdims=True))
            alpha = jnp.exp2(m_old - m_new)
            p = jnp.exp2(s - _lane_tiles(m_new, width // _LANES)).astype(_BF)
            acc_ref[g] = _lane_tiles(alpha, 2) * acc_ref[g] + _dot(p, vx)
            m_ref[g] = m_new

    def chunk(c, carry):
        off = pl.multiple_of(c * ck, ck)
        step(k_ref[pl.ds(off, ck), :], v_ref[pl.ds(off, ck), :])
        return carry

    lax.fori_loop(0, seq // ck, chunk, 0, unroll=min(4, seq // ck))
    step(kc_ref[...], vc_ref[...])
    for g in range(group):
        acc = acc_ref[g]
        o_ref[:, g * _LANES:(g + 1) * _LANES] = (acc[:, :_LANES] / acc[:, _LANES:]).astype(o_ref.dtype)


def _gqa_attention(QKV, geo, n_heads, n_kv, bq, ck):
    b, s, lc = geo["batch"], geo["seq"], geo["ctx_len"]
    group = n_heads // n_kv
    gw = group * _LANES
    nq = s // bq
    cblk = geo["n_x_rows"] // lc
    return pl.pallas_call(
        functools.partial(_gqa_body, s, ck, group),
        grid=(b, n_kv, nq),
        in_specs=[
            pl.BlockSpec((bq, gw), lambda bi, n, j: (bi * nq + j, n)),
            pl.BlockSpec((s, _LANES), lambda bi, n, j: (bi, n_heads + n)),
            pl.BlockSpec((s, _LANES), lambda bi, n, j: (bi, n_heads + n_kv + n)),
            pl.BlockSpec((lc, _LANES), lambda bi, n, j: (cblk + bi, n_heads + n)),
            pl.BlockSpec((lc, _LANES), lambda bi, n, j: (cblk + bi, n_heads + n_kv + n)),
        ],
        out_specs=pl.BlockSpec((bq, gw), lambda bi, n, j: (bi * nq + j, n)),
        out_shape=jax.ShapeDtypeStruct((geo["n_x_rows"], n_heads * _LANES), _BF),
        scratch_shapes=[pltpu.VMEM((group, bq, _LANES), _F32), pltpu.VMEM((group, bq, 2 * _LANES), _F32)],
        compiler_params=_cp("arbitrary", "arbitrary", "arbitrary"),
        name="gqa_attn",
    )(QKV, QKV, QKV, QKV, QKV)


def _log_sigmoid(x):
    return jnp.minimum(x, 0.0) - jnp.log1p(jnp.exp(-jnp.abs(x)))


def _mlstm_chunk(d, st, n_heads, L, h, q_ref, k_ref, v_ref, gt_ref, gb_ref, o_ref, c_ref, n_ref, m_ref):
    gates = gt_ref[...] + gb_ref[...]
    lane = lax.broadcasted_iota(jnp.int32, (1, _LANES), 1)
    li_col = jnp.sum(jnp.where(lane == (2 * d) * n_heads + h, gates, 0.0), axis=-1, keepdims=True)
    lf_pre = jnp.sum(jnp.where(lane == (2 * d + 1) * n_heads + h, gates, 0.0), axis=-1, keepdims=True)
    lf_col = _log_sigmoid(lf_pre)
    ti = lax.broadcasted_iota(jnp.int32, (L, L), 0)
    si = lax.broadcasted_iota(jnp.int32, (L, L), 1)
    eye = ti == si
    li_row = jnp.sum(jnp.where(eye, li_col, 0.0), axis=0, keepdims=True)
    lf_row = jnp.sum(jnp.where(eye, lf_col, 0.0), axis=0, keepdims=True)
    allowed = (ti >= si) if d == 0 else (ti <= si)
    allowed_t = (si >= ti) if d == 0 else (si <= ti)
    b_col = jnp.sum(jnp.where(allowed, lf_row, 0.0), axis=-1, keepdims=True)
    b_row = jnp.sum(jnp.where(allowed_t, lf_col, 0.0), axis=0, keepdims=True)
    total = jnp.sum(lf_col, axis=0, keepdims=True)
    m_prev = m_ref[st, 0:1, 0:1]

    q = q_ref[...]
    k = k_ref[...]
    v = v_ref[...]
    dmat = jnp.where(allowed, b_col - b_row + li_row, _NEG)
    g_col = b_col + m_prev
    m_t = jnp.maximum(g_col, jnp.max(dmat, axis=-1, keepdims=True))
    w = jnp.exp(dmat - m_t) * _dot_nt(q, k)
    w_prev = jnp.exp(g_col - m_t)
    cmat = c_ref[st]
    nvec = n_ref[st, 0:1, :]
    num = _dot(w.astype(_BF), v) + w_prev * _dot(q, cmat.astype(_BF))
    qn = jnp.sum(q.astype(_F32) * nvec, axis=-1, keepdims=True)
    den = jnp.sum(w, axis=-1, keepdims=True) + w_prev * qn
    o_ref[...] = (num / jnp.maximum(jnp.abs(den), jnp.exp(-m_t))).astype(o_ref.dtype)

    lw = total - b_col + li_col
    m_new = jnp.maximum(total + m_prev, jnp.max(lw, axis=0, keepdims=True))
    decay = jnp.exp(total + m_prev - m_new)
    kw = k.astype(_F32) * jnp.exp(lw - m_new)
    c_ref[st] = decay * cmat + _dot_tn(kw.astype(_BF), v)
    n_ref[st] = jnp.broadcast_to(decay * nvec + jnp.sum(kw, axis=0, keepdims=True), n_ref.shape[1:])
    m_ref[st] = jnp.broadcast_to(m_new, m_ref.shape[1:])


def _mlstm_body(n_heads, hps, dqk, dv, chunk, qf_ref, kf_ref, vf_ref, gf_ref, qb_ref, kb_ref, vb_ref, gbk_ref,
                gb_ref, of_ref, ob_ref, c_ref, n_ref, m_ref):
    hg = pl.program_id(1)

    @pl.when(pl.program_id(2) == 0)
    def _():
        c_ref[...] = jnp.zeros(c_ref.shape, _F32)
        n_ref[...] = jnp.zeros(n_ref.shape, _F32)
        m_ref[...] = jnp.zeros(m_ref.shape, _F32)

    dirs = ((qf_ref, kf_ref, vf_ref, gf_ref, of_ref), (qb_ref, kb_ref, vb_ref, gbk_ref, ob_ref))
    for hh in range(hps):
        qs, vs = slice(hh * dqk, (hh + 1) * dqk), slice(hh * dv, (hh + 1) * dv)
        for d, (q_ref, k_ref, v_ref, g_ref, o_ref) in enumerate(dirs):
            _mlstm_chunk(d, d * hps + hh, n_heads, chunk, hg * hps + hh, q_ref.at[:, qs], k_ref.at[:, qs],
                         v_ref.at[:, vs], g_ref, gb_ref, o_ref.at[:, vs], c_ref, n_ref, m_ref)


def _mlstm_scan(P, G, gate_b, geo, n_heads, dqk, dv, chunk):
    b, s, lc = geo["batch"], geo["seq"], geo["ctx_len"]
    t = P.shape[0]
    nctx, nx = lc // chunk, s // chunk
    cbase = geo["n_x_rows"] // chunk

    def rblk(d, bi, c):
        cc = c - nctx
        in_ctx = cbase + bi * nctx + (c if d == 0 else nctx - 1 - c)
        in_x = bi * nx + (cc if d == 0 else nx - 1 - cc)
        return jnp.where(c < nctx, in_ctx, in_x)

    hps = 2
    ng = n_heads // hps
    k_off = ng
    v_off = 2 * n_heads * dqk // (hps * dv)
    in_specs, args = [], []
    for d in (0, 1):
        in_specs += [
            pl.BlockSpec((chunk, hps * dqk), lambda bi, h, c, d=d: (rblk(d, bi, c), h)),
            pl.BlockSpec((chunk, hps * dqk), lambda bi, h, c, d=d: (rblk(d, bi, c), k_off + h)),
            pl.BlockSpec((chunk, hps * dv), lambda bi, h, c, d=d: (rblk(d, bi, c), v_off + h)),
            pl.BlockSpec((chunk, _LANES), lambda bi, h, c, d=d: (rblk(d, bi, c), 0)),
        ]
        args += [P, P, P, G]
    in_specs.append(pl.BlockSpec((1, _LANES), lambda bi, h, c: (0, 0)))
    args.append(gate_b)
    out_sds = jax.ShapeDtypeStruct((t, n_heads * dv), _F32)
    ns = 2 * hps
    return pl.pallas_call(
        functools.partial(_mlstm_body, n_heads, hps, dqk, dv, chunk),
        grid=(b, ng, nctx + nx),
        in_specs=in_specs,
        out_specs=[pl.BlockSpec((chunk, hps * dv), lambda bi, h, c, d=d: (rblk(d, bi, c), h)) for d in (0, 1)],
        out_shape=[out_sds, out_sds],
        scratch_shapes=[pltpu.VMEM((ns, dqk, dv), _F32), pltpu.VMEM((ns, 8, dqk), _F32), pltpu.VMEM((ns, 8, _LANES), _F32)],
        compiler_params=_cp("arbitrary", "arbitrary", "arbitrary"),
        name="mlstm_scan",
    )(*args)


def _ml_out_body(n_heads, dv, hf_ref, hb_ref, og_ref, hg_ref, w_ref, x_ref, gt_ref, o_ref):
    hsum = hf_ref[...] + hb_ref[...]
    parts = []
    for hh in range(n_heads):
        a = hsum[:, hh * dv:(hh + 1) * dv]
        hn = a * lax.rsqrt(jnp.mean(a * a, axis=-1, keepdims=True) + _EPS) * hg_ref[:, hh * dv:(hh + 1) * dv]
        og = og_ref[:, hh * dv:(hh + 1) * dv].astype(_F32)
        parts.append((jax.nn.sigmoid(og) * hn).astype(_BF))
    a = jnp.concatenate(parts, axis=-1)
    o_ref[...] = x_ref[...] + gt_ref[0] * _dot(a, w_ref[...])


def _ml_out(Hf, Hb, P, head_g, W, X, gate, geo, n_heads, dv, bm):
    t, d = X.shape
    kd = n_heads * dv
    o_off = (P.shape[1] - kd) // kd
    mrow = functools.partial(geo["mrow_bm"], bm)
    return pl.pallas_call(
        functools.partial(_ml_out_body, n_heads, dv),
        grid=(t // bm,),
        in_specs=[
            pl.BlockSpec((bm, kd), lambda i: (i, 0)),
            pl.BlockSpec((bm, kd), lambda i: (i, 0)),
            pl.BlockSpec((bm, kd), lambda i: (i, o_off)),
            pl.BlockSpec((1, kd), lambda i: (0, 0)),
            pl.BlockSpec((kd, d), lambda i: (0, 0)),
            pl.BlockSpec((bm, d), lambda i: (i, 0)),
            pl.BlockSpec((1, 1, d), lambda i: (mrow(i), 0, 0)),
        ],
        out_specs=pl.BlockSpec((bm, d), lambda i: (i, 0)),
        out_shape=jax.ShapeDtypeStruct((t, d), _F32),
        compiler_params=_cp("arbitrary"),
        name="ml_out",
    )(Hf, Hb, P, head_g, W, X, gate)


def _rope_tables(seq, pad_rows):
    tpos = np.arange(seq)
    row = (tpos // _GRID_W).astype(np.float32)
    col = (tpos % _GRID_W).astype(np.float32)
    nf = _LANES // 4
    inv = (_ROPE_BASE ** (-jnp.arange(nf, dtype=_F32) / nf))
    ar = jnp.asarray(row)[:, None] * inv
    ac = jnp.asarray(col)[:, None] * inv
    cr, sr, cc, sc = jnp.cos(ar), jnp.sin(ar), jnp.cos(ac), jnp.sin(ac)
    cs = jnp.concatenate([cr, cc, cr, cc], axis=1)
    sn = jnp.concatenate([-sr, -sc, sr, sc], axis=1)
    cs = jnp.concatenate([cs, jnp.ones((pad_rows, _LANES), _F32)], axis=0)
    sn = jnp.concatenate([sn, jnp.zeros((pad_rows, _LANES), _F32)], axis=0)
    return cs, sn


def _pair_major(v):
    q = _LANES // 4
    return v.reshape(v.shape[:-1] + (v.shape[-1] // _LANES, 2, 2, q)).swapaxes(-3, -2).reshape(v.shape)


def _rope_operands(w_qkv, q_gain, k_gain, n_qk_heads, cs, sn):
    nqk = n_qk_heads * _LANES
    w = jnp.concatenate([_pair_major(w_qkv[:, :nqk]), w_qkv[:, nqk:]], axis=1).astype(_BF)
    tabs = []
    for gain in (q_gain, k_gain):
        gp = _pair_major(gain.astype(_F32))
        tabs += [cs * gp[None, :], sn * jnp.roll(gp, _LANES // 2)[None, :]]
    return w, tuple(tabs)


def _geometry(batch, seq, ctx_len):
    bm = min(1024, batch * ctx_len)
    assert seq % bm == 0 and (batch * ctx_len) % bm == 0
    assert seq & (seq - 1) == 0 and ctx_len & (ctx_len - 1) == 0
    n_x_rows = batch * seq

    def mrow_bm(bm_, i):
        return jnp.minimum((i * bm_) // seq, batch)

    return dict(batch=batch, seq=seq, ctx_len=ctx_len, bm=bm, n_x_rows=n_x_rows,
                n_x_tiles=n_x_rows // bm, tiles_per_seq=seq // bm,
                mrow=functools.partial(mrow_bm, bm), mrow_bm=mrow_bm)


def kernel(x, c, ctx, c_ctx, ada_w, ada_b, norm1_g, norm2_g, ffn_w_in, ffn_conv_w, ffn_conv_b, ffn_w_out,
           na_w_qkv, na_q_g, na_k_g, na_rel_bias, na_w_o,
           swa_w_qkv, swa_q_g, swa_k_g, swa_sinks, swa_w_o,
           ml_w_in, ml_gate_b, ml_head_g, ml_w_o,
           gqa_w_qkv, gqa_q_g, gqa_k_g, gqa_w_o):
    batch, seq, d = x.shape
    ctx_len = ctx.shape[1]
    depth = ada_w.shape[0]
    n_heads = d // _LANES
    geo = _geometry(batch, seq, ctx_len)
    bm = geo["bm"]
    n_x_rows = geo["n_x_rows"]
    n_x_tiles = geo["n_x_tiles"]
    qscale = (_LANES ** -0.5) * _LOG2E

    X = jnp.concatenate([x.reshape(n_x_rows, d), ctx.reshape(batch * ctx_len, d)], axis=0)
    t_all = X.shape[0]
    nrt_all = t_all // bm

    cond = jnp.concatenate([c, c_ctx[None, :], jnp.zeros((8 - batch - 1, d), _F32)], axis=0)
    mods = _adaln(cond, ada_w, ada_b)

    cs, sn = _rope_tables(seq, bm)
    f = ffn_conv_w.shape[2]
    w_in_b = ffn_w_in.astype(_BF)
    w_out_b = ffn_w_out.astype(_BF)
    conv_b3 = ffn_conv_b.reshape(depth, 1, f)
    bf = 512
    bm_o = min(512, bm)

    for i in range(depth):
        kind, jl = i % 4, i // 4
        need_ctx = i < depth - 1
        mod = [mods[i, :batch + 1, k * d:(k + 1) * d].reshape(batch + 1, 1, d) for k in range(6)]
        sh1, sc1, g1, sh2, sc2, g2 = mod
        n1 = norm1_g[i].reshape(1, d)
        n2 = norm2_g[i].reshape(1, d)
        nrt_o = t_all // bm_o if need_ctx else n_x_rows // bm_o

        if kind == 0:
            w = na_w_qkv[jl].astype(_BF)
            gq = (na_q_g[jl] * qscale).reshape(1, _LANES)
            gk = na_k_g[jl].reshape(1, _LANES)
            bn = 1024
            QKV = _proj(X, n1, sh1, sc1, w, _qkv_layouts(n_heads, n_heads, bn, False), geo, bn, _BF, gains=(gq, gk))
            rows = seq // _GRID_W
            tab = _na_bias_table(na_rel_bias[jl], rows, 4, 12)
            O = _na_attention(QKV, tab, geo, n_heads)
            if need_ctx:
                O = _ctx_attention(QKV, O, geo, n_heads, n_heads)
            X = _oproj(O, na_w_o[jl].astype(_BF), X, g1, geo, nrt_o, bm_o, d)
        elif kind == 1:
            n_kv = (swa_w_qkv.shape[2] // _LANES - n_heads) // 2
            w, tabs = _rope_operands(swa_w_qkv[jl], swa_q_g[jl] * qscale, swa_k_g[jl], n_heads + n_kv, cs, sn)
            bn = _wide_tile(w.shape[1])
            QKV = _proj(X, n1, sh1, sc1, w, _qkv_layouts(n_heads, n_kv, bn, True), geo, bn, _BF, rope=tabs)
            group = n_heads // n_kv
            sink = jnp.repeat(swa_sinks[jl].astype(_F32) * _LOG2E, _LANES).reshape(n_kv, 1, group * _LANES)
            O = _swa_attention(QKV, sink, geo, n_heads, n_kv, 256)
            if need_ctx:
                O = _ctx_attention(QKV, O, geo, n_heads, n_kv, sink)
            X = _oproj(O, swa_w_o[jl].astype(_BF), X, g1, geo, nrt_o, bm_o, d)
        elif kind == 2:
            mh = ml_gate_b.shape[1] // 4
            dv = d // mh
            dqk = dv // 2
            nmain = 2 * mh * dqk + 2 * mh * dv
            w_main = ml_w_in[jl][:, :nmain].astype(_BF)
            w_gate = jnp.pad(ml_w_in[jl][:, nmain:], ((0, 0), (0, _LANES - 4 * mh))).astype(_BF)
            bn = 1024
            segs_by_tile = []
            k0, k1 = mh * dqk, 2 * mh * dqk
            for jt in range(nmain // bn):
                lo, hi = jt * bn, (jt + 1) * bn
                cuts = sorted({lo, hi, min(max(k0, lo), hi), min(max(k1, lo), hi)})
                segs = tuple((a - lo, b_ - lo, "plain", 0, False, dqk ** -0.5 if k0 <= a < k1 else 1.0)
                             for a, b_ in zip(cuts[:-1], cuts[1:]))
                segs_by_tile.append((jt, jt + 1, segs))
            P = _proj(X, n1, sh1, sc1, w_main, tuple(segs_by_tile), geo, bn, _BF)
            G = _proj(X, n1, sh1, sc1, w_gate, ((0, 1, ((0, _LANES, "plain", 0, False, 1.0),)),), geo, _LANES, _F32)
            gb = jnp.pad(ml_gate_b[jl].astype(_F32), (0, _LANES - 4 * mh)).reshape(1, _LANES)
            Hf, Hb = _mlstm_scan(P, G, gb, geo, mh, dqk, dv, 256)
            X = _ml_out(Hf, Hb, P, ml_head_g[jl].reshape(1, mh * dv), ml_w_o[jl].astype(_BF), X, g1, geo, mh, dv, bm_o)
        else:
            n_kv = (gqa_w_qkv.shape[2] // _LANES - n_heads) // 2
            w, tabs = _rope_operands(gqa_w_qkv[jl], gqa_q_g[jl] * qscale, gqa_k_g[jl], n_heads + n_kv, cs, sn)
            bn = _wide_tile(w.shape[1])
            QKV = _proj(X, n1, sh1, sc1, w, _qkv_layouts(n_heads, n_kv, bn, True), geo, bn, _BF, rope=tabs)
            O = _gqa_attention(QKV, geo, n_heads, n_kv, min(1024, seq), 512)
            if need_ctx:
                Oc = jnp.zeros((t_all, d), _BF).at[:n_x_rows].set(O)
                O = _ctx_attention(QKV, Oc, geo, n_heads, n_kv)
            X = _oproj(O, gqa_w_o[jl].astype(_BF), X, g1, geo, nrt_o, bm_o, d)

        nrt_f = nrt_all if need_ctx else n_x_tiles
        A = _ffn1(X, n2, sh2, sc2, w_in_b, ffn_conv_w, conv_b3, i, geo, nrt_f, bf)
        X = _oproj(A, w_out_b, X, g2, geo, A.shape[0] // bm, bm, 512, layer=i)

    return X[:n_x_rows].reshape(batch, seq, d)
```

```python
import functools
import math

import numpy as np
import jax
import jax.numpy as jnp
from jax import lax
from jax.experimental import pallas as pl
from jax.experimental.pallas import tpu as pltpu

_F32 = jnp.float32
_BF = jnp.bfloat16
_EPS = 1e-6
_NEG = -1e30
_LOG2E = 1.4426950408889634
_GRID_W = 64
_SWA_WINDOW = 128
_ROPE_BASE = 10000.0
_LANES = 128
_HALO = 8
_VMEM_LIMIT = 56 << 20


def _cp(*sem):
    return pltpu.CompilerParams(dimension_semantics=sem, vmem_limit_bytes=_VMEM_LIMIT)


def _dot(a, b):
    return jnp.dot(a, b, preferred_element_type=_F32)


def _dot_nt(a, b):
    return lax.dot_general(a, b, (((1,), (1,)), ((), ())), preferred_element_type=_F32)


def _dot_tn(a, b):
    return lax.dot_general(a, b, (((0,), (0,)), ((), ())), preferred_element_type=_F32)


def _norm_mod(x, g, sh, sc):
    ms = jnp.mean(x * x, axis=-1, keepdims=True)
    return (x * lax.rsqrt(ms + _EPS)) * (g * (1.0 + sc)) + sh


def _adaln_body(c_ref, w_ref, b_ref, o_ref):
    c = c_ref[...]
    s = (c * jax.nn.sigmoid(c)).astype(_BF)
    o_ref[0] = _dot(s, w_ref[0].astype(_BF)) + b_ref[0]


def _adaln(cond, ada_w, ada_b):
    depth, d, n = ada_w.shape
    bn = 1024
    return pl.pallas_call(
        _adaln_body,
        grid=(depth, n // bn),
        in_specs=[
            pl.BlockSpec((8, d), lambda l, j: (0, 0)),
            pl.BlockSpec((1, d, bn), lambda l, j: (l, 0, j)),
            pl.BlockSpec((1, 1, bn), lambda l, j: (l, 0, j)),
        ],
        out_specs=pl.BlockSpec((1, 8, bn), lambda l, j: (l, 0, j)),
        out_shape=jax.ShapeDtypeStruct((depth, 8, n), _F32),
        compiler_params=_cp("arbitrary", "arbitrary"),
        name="adaln",
    )(cond, ada_w, ada_b.reshape(depth, 1, n))


def _next_rows_chunks(bm, nj):
    nc = 1
    while nc * 2 <= nj and bm // (nc * 2) >= _LANES:
        nc *= 2
    return nc, bm // nc


def _proj_body(layouts, has_rope, n_gain, n_lat_tiles, *refs):
    x_refs = refs[:1] if n_lat_tiles is None else refs[:2]
    refs = refs[len(x_refs):]
    g_ref, sh_ref, sc_ref, w_ref = refs[:4]
    pos = 4
    tab_refs = ()
    if has_rope:
        tab_refs = refs[pos:pos + 4]
        pos += 4
    gain_refs = refs[pos:pos + n_gain]
    pos += n_gain
    o_ref, h_ref = refs[pos], refs[pos + 1]
    j = pl.program_id(1)

    @pl.when(j == 0)
    def _():
        x = x_refs[0][...]
        if n_lat_tiles is not None:
            x = jnp.where(pl.program_id(0) < n_lat_tiles, x, x_refs[1][...])
        h_ref[...] = _norm_mod(x, g_ref[...], sh_ref[0], sc_ref[0]).astype(_BF)

    for lo, hi, segs in layouts:
        @pl.when((j >= lo) & (j < hi))
        def _(segs=segs):
            acc = _dot(h_ref[...], w_ref[...])
            for c0, c1, kind, gi, rope, mult in segs:
                if kind == "plain":
                    a = acc[:, c0:c1]
                    if mult != 1.0:
                        a = a * mult
                    o_ref[:, c0:c1] = a.astype(o_ref.dtype)
                    continue
                for c in range(c0, c1, _LANES):
                    a = acc[:, c:c + _LANES]
                    inv = lax.rsqrt(jnp.mean(a * a, axis=-1, keepdims=True) + _EPS)
                    if rope:
                        y = a * tab_refs[2 * gi][...] + pltpu.roll(a, _LANES // 2, 1) * tab_refs[2 * gi + 1][...]
                    else:
                        y = a * gain_refs[gi][...]
                    o_ref[:, c:c + _LANES] = (y * inv).astype(o_ref.dtype)


def _proj(X, g, sh, sc, W, layouts, geo, bn, out_dtype, rope=None, gains=()):
    n = W.shape[1]
    bm = geo["bm"]
    nj = n // bn
    mrow = geo["mrow"]
    if isinstance(X, tuple):
        nlt = X[0].shape[0] // bm
        t, d = X[0].shape[0] + X[1].shape[0], X[0].shape[1]
        x_specs = [pl.BlockSpec((bm, d), lambda i, j: (jnp.minimum(i, nlt - 1), 0)),
                   pl.BlockSpec((bm, d), lambda i, j: (jnp.maximum(i - nlt, 0), 0),
                                pipeline_mode=pl.Buffered(1))]
        x_args = list(X)
    else:
        nlt = None
        t, d = X.shape
        x_specs = [pl.BlockSpec((bm, d), lambda i, j: (i, 0))]
        x_args = [X]
    nrt = t // bm
    in_specs = x_specs + [
        pl.BlockSpec((1, d), lambda i, j: (0, 0)),
        pl.BlockSpec((1, 1, d), lambda i, j: (mrow(i), 0, 0)),
        pl.BlockSpec((1, 1, d), lambda i, j: (mrow(i), 0, 0)),
        pl.BlockSpec((d, bn), lambda i, j: (0, j)),
    ]
    args = x_args + [g, sh, sc, W]
    if rope is not None:
        tps, nxt = geo["tiles_per_seq"], geo["n_x_tiles"]
        tab = lambda i, j: (jnp.where(i < nxt, i % tps, tps), 0)
        in_specs += [pl.BlockSpec((bm, _LANES), tab)] * 4
        args += list(rope)
    for gn in gains:
        in_specs.append(pl.BlockSpec((1, _LANES), lambda i, j: (0, 0)))
        args.append(gn)
    return pl.pallas_call(
        functools.partial(_proj_body, layouts, rope is not None, len(gains), nlt),
        grid=(nrt, nj),
        in_specs=in_specs,
        out_specs=pl.BlockSpec((bm, bn), lambda i, j: (i, j)),
        out_shape=jax.ShapeDtypeStruct((t, n), out_dtype),
        scratch_shapes=[pltpu.VMEM((bm, d), _BF)],
        compiler_params=_cp("arbitrary", "arbitrary"),
        name="proj",
    )(*args)


def _wide_tile(n, cap=1280):
    return max(b for b in range(_LANES, cap + 1, _LANES) if n % b == 0)


def _qkv_layouts(n_q, n_kv, bn, rope):
    hd = _LANES
    bounds = [(0, n_q * hd, "q"), (n_q * hd, (n_q + n_kv) * hd, "k"), ((n_q + n_kv) * hd, (n_q + 2 * n_kv) * hd, "v")]
    n = (n_q + 2 * n_kv) * hd
    per_tile = []
    for jt in range(n // bn):
        lo, hi = jt * bn, (jt + 1) * bn
        segs = []
        for b0, b1, nm in bounds:
            s0, s1 = max(lo, b0), min(hi, b1)
            if s0 < s1:
                if nm == "v":
                    segs.append((s0 - lo, s1 - lo, "plain", 0, False, 1.0))
                else:
                    segs.append((s0 - lo, s1 - lo, "head", 0 if nm == "q" else 1, rope, 1.0))
        per_tile.append(tuple(segs))
    layouts = []
    for jt, segs in enumerate(per_tile):
        if layouts and layouts[-1][2] == segs and layouts[-1][1] == jt:
            layouts[-1] = (layouts[-1][0], jt + 1, segs)
        else:
            layouts.append((jt, jt + 1, segs))
    return tuple(layouts)


def _oproj_body(n_lat_tiles, a_ref, w_ref, *refs):
    x_refs, (gt_ref, o_ref) = refs[:-2], refs[-2:]
    x = x_refs[0][...]
    if n_lat_tiles is not None:
        x = jnp.where(pl.program_id(0) < n_lat_tiles, x, x_refs[1][...])
    o_ref[...] = x + gt_ref[0] * _dot(a_ref[...], w_ref[...])


def _oproj(A, W, X, gate, geo, nrt, bm, bn, layer=None):
    k = A.shape[1]
    d = W.shape[-1]
    mrow = functools.partial(geo["mrow_bm"], bm)
    if layer is None:
        w_spec = pl.BlockSpec((k, bn), lambda i, j: (0, j))
    else:
        w_spec = pl.BlockSpec((None, k, bn), lambda i, j: (layer, 0, j))
    if isinstance(X, tuple):
        nlt = X[0].shape[0] // bm
        x_specs = [pl.BlockSpec((bm, bn), lambda i, j: (jnp.minimum(i, nlt - 1), j)),
                   pl.BlockSpec((bm, bn), lambda i, j: (jnp.maximum(i - nlt, 0), j))]
        x_args = list(X)
    else:
        nlt = None
        x_specs = [pl.BlockSpec((bm, bn), lambda i, j: (i, j))]
        x_args = [X]
    return pl.pallas_call(
        functools.partial(_oproj_body, nlt),
        grid=(nrt, d // bn),
        in_specs=[pl.BlockSpec((bm, k), lambda i, j: (i, 0)), w_spec] + x_specs
        + [pl.BlockSpec((1, 1, bn), lambda i, j: (mrow(i), 0, j))],
        out_specs=pl.BlockSpec((bm, bn), lambda i, j: (i, j)),
        out_shape=jax.ShapeDtypeStruct((nrt * bm, d), _F32),
        compiler_params=_cp("arbitrary", "arbitrary"),
        name="oproj",
    )(A, W, *x_args, gate)


def _ffn1_body(bm, n_x_rows, seq, ctx_len, nc, rc, x0_ref, x0p_ref, x0n_ref, xn_ref, xnp_ref, xnn_ref,
               g_ref, sh_ref, sc_ref, shn_ref, scn_ref, wg_ref, wu_ref, cw_ref, cb_ref, o_ref, ha_ref, hb_ref):
    i = pl.program_id(0)
    j = pl.program_id(1)
    h_refs = (ha_ref, hb_ref)

    @pl.when((i == 0) & (j == 0))
    def _():
        g, sh, sc = g_ref[...], sh_ref[0], sc_ref[0]
        ha_ref[0:bm, :] = _norm_mod(x0_ref[...], g, sh, sc).astype(_BF)
        halo = jnp.concatenate([x0p_ref[...], x0n_ref[...]], axis=0)
        ha_ref[bm:bm + 2 * _HALO, :] = _norm_mod(halo, g, sh, sc).astype(_BF)

    for par in (0, 1):
        @pl.when(i % 2 == par)
        def _(par=par):
            h_ref, hn_ref = h_refs[par], h_refs[1 - par]
            gx = _dot(h_ref[...], wg_ref[...])
            u = _dot(h_ref[0:bm, :], wu_ref[...])
            g, shn, scn = g_ref[...], shn_ref[0], scn_ref[0]
            r0 = pl.multiple_of(jnp.minimum(j, nc - 1) * rc, rc)
            hn_ref[pl.ds(r0, rc), :] = _norm_mod(xn_ref[...], g, shn, scn).astype(_BF)
            halo = jnp.concatenate([xnp_ref[...], xnn_ref[...]], axis=0)
            hn_ref[bm:bm + 2 * _HALO, :] = _norm_mod(halo, g, shn, scn).astype(_BF)
            gm = gx[0:bm]
            g_prev = gx[bm + _HALO - 1:bm + _HALO]
            g_next = gx[bm + _HALO:bm + _HALO + 1]
            row = lax.broadcasted_iota(jnp.int32, (bm, 1), 0)
            tok = i * bm + row
            period = jnp.where(i * bm >= n_x_rows, ctx_len, seq)
            up = jnp.where(row == 0, g_prev, pltpu.roll(gm, 1, 0))
            dn = jnp.where(row == bm - 1, g_next, pltpu.roll(gm, bm - 1, 0))
            up = jnp.where((tok & (period - 1)) != 0, up, 0.0)
            dn = jnp.where(((tok + 1) & (period - 1)) != 0, dn, 0.0)
            cw = cw_ref[...]
            gc = cb_ref[...] + up * cw[0:1] + gm * cw[1:2] + dn * cw[2:3]
            o_ref[...] = (jax.nn.gelu(gc) * u).astype(o_ref.dtype)


def _ffn1(X, g, sh, sc, w_in, conv_w, conv_b, layer, geo, nrt, bf):
    t, d = X.shape
    f = conv_w.shape[2]
    bm = geo["bm"]
    mrow = geo["mrow"]
    hb = bm // _HALO
    last = t // _HALO - 1
    nf = f // bf
    nc, rc = _next_rows_chunks(bm, nf)
    nxt = lambda i: jnp.minimum(i + 1, nrt - 1)
    body = functools.partial(_ffn1_body, bm, geo["n_x_rows"], geo["seq"], geo["ctx_len"], nc, rc)
    return pl.pallas_call(
        body,
        grid=(nrt, nf),
        in_specs=[
            pl.BlockSpec((bm, d), lambda i, j: (0, 0)),
            pl.BlockSpec((_HALO, d), lambda i, j: (0, 0)),
            pl.BlockSpec((_HALO, d), lambda i, j: (jnp.minimum(hb, last), 0)),
            pl.BlockSpec((rc, d), lambda i, j: (nxt(i) * nc + jnp.minimum(j, nc - 1), 0)),
            pl.BlockSpec((_HALO, d), lambda i, j: (jnp.maximum(nxt(i) * hb - 1, 0), 0)),
            pl.BlockSpec((_HALO, d), lambda i, j: (jnp.minimum((nxt(i) + 1) * hb, last), 0)),
            pl.BlockSpec((1, d), lambda i, j: (0, 0)),
            pl.BlockSpec((1, 1, d), lambda i, j: (mrow(i), 0, 0)),
            pl.BlockSpec((1, 1, d), lambda i, j: (mrow(i), 0, 0)),
            pl.BlockSpec((1, 1, d), lambda i, j: (mrow(nxt(i)), 0, 0)),
            pl.BlockSpec((1, 1, d), lambda i, j: (mrow(nxt(i)), 0, 0)),
            pl.BlockSpec((None, d, bf), lambda i, j: (layer, 0, j)),
            pl.BlockSpec((None, d, bf), lambda i, j: (layer, 0, j + nf)),
            pl.BlockSpec((None, 3, bf), lambda i, j: (layer, 0, j)),
            pl.BlockSpec((None, 1, bf), lambda i, j: (layer, 0, j)),
        ],
        out_specs=pl.BlockSpec((bm, bf), lambda i, j: (i, j)),
        out_shape=jax.ShapeDtypeStruct((nrt * bm, f), _BF),
        scratch_shapes=[pltpu.VMEM((bm + 2 * _HALO, d), _BF), pltpu.VMEM((bm + 2 * _HALO, d), _BF)],
        compiler_params=_cp("arbitrary", "arbitrary"),
        name="ffn1",
    )(X, X, X, X, X, X, g, sh, sc, sh, sc, w_in, w_in, conv_w, conv_b)


def _with_ones(v):
    return jnp.concatenate([v, jnp.ones(v.shape, v.dtype)], axis=1)


def _lane_tiles(x, n):
    return jnp.concatenate([x] * n, axis=1) if n > 1 else x


def _na_body(seq, sub, kwin, hps, q_ref, k_ref, v_ref, kc_ref, vc_ref, ba_ref, bb_ref, o_ref):
    j = pl.program_id(2)
    for hh in range(hps):
        c = hh * _LANES
        kc = kc_ref[:, c:c + _LANES]
        vcx = _with_ones(vc_ref[:, c:c + _LANES])
        for i, b_ref in enumerate((ba_ref, bb_ref)):
            q0 = (2 * j + i) * sub
            ks = pl.multiple_of(jnp.clip(q0 - (kwin - sub) // 2, 0, seq - kwin), 256)
            q = q_ref[i * sub:(i + 1) * sub, c:c + _LANES]
            k = k_ref[pl.ds(ks, kwin), c:c + _LANES]
            v = v_ref[pl.ds(ks, kwin), c:c + _LANES]
            s = _dot_nt(q, k) + b_ref[0, hh]
            sc = _dot_nt(q, kc)
            m = jnp.maximum(jnp.max(s, axis=-1, keepdims=True), jnp.max(sc, axis=-1, keepdims=True))
            p = jnp.exp2(s - m).astype(_BF)
            pc = jnp.exp2(sc - m).astype(_BF)
            acc = _dot(p, _with_ones(v)) + _dot(pc, vcx)
            o_ref[i * sub:(i + 1) * sub, c:c + _LANES] = (acc[:, :_LANES] / acc[:, _LANES:]).astype(o_ref.dtype)


def _na_bias_table(rel_bias, rows, rq, rk):
    h, nr2, nc2 = rel_bias.shape
    na_rows, na_cols = (nr2 + 1) // 2, (nc2 + 1) // 2
    kr = min(na_rows, rows)
    w = _GRID_W
    nblk = rows // rq
    c = np.arange(w)
    c0 = np.clip(c - na_cols // 2, 0, w - na_cols)
    col_ok = (c[None, :] >= c0[:, None]) & (c[None, :] < c0[:, None] + na_cols)
    rbp = jnp.pad(rel_bias.astype(_F32) * _LOG2E, ((0, 0), (0, 0), (w, w)))
    tcol = jnp.stack([rbp[:, :, na_cols - 1 - qc + w:na_cols - 1 - qc + 2 * w] for qc in range(w)], axis=2)
    tcol = jnp.where(jnp.asarray(col_ok)[None, None], tcol, _NEG)
    tcol = jnp.concatenate([tcol, jnp.full((h, 1, w, w), _NEG, _F32)], axis=1)
    sel = np.zeros((3, rq, rk, nr2 + 1), np.float32)
    for ti, jb in enumerate((0, min(1, nblk - 1), nblk - 1)):
        kb0 = int(np.clip(jb * rq - (rk - rq) // 2, 0, rows - rk))
        for qi in range(rq):
            r = jb * rq + qi
            r0 = int(np.clip(r - kr // 2, 0, rows - kr))
            for ki in range(rk):
                krow = kb0 + ki
                sel[ti, qi, ki, krow - r + na_rows - 1 if r0 <= krow < r0 + kr else nr2] = 1.0
    tab = jnp.einsum("tqkd,hdcx->thqckx", jnp.asarray(sel), tcol, precision=lax.Precision.HIGHEST)
    return tab.reshape(3, h, rq * w, rk * w)


def _na_attention(QKV, bias_tab, geo, n_heads):
    b, s, lc = geo["batch"], geo["seq"], geo["ctx_len"]
    t = QKV.shape[0]
    sub, kwin = bias_tab.shape[2], bias_tab.shape[3]
    bq = 2 * sub
    nj = s // bq
    cblk = geo["n_x_rows"] // lc
    hps = 2
    hw = hps * _LANES
    ng = n_heads // hps
    return pl.pallas_call(
        functools.partial(_na_body, s, sub, kwin, hps),
        grid=(b, ng, nj),
        in_specs=[
            pl.BlockSpec((bq, hw), lambda bi, h, j: (bi * nj + j, h)),
            pl.BlockSpec((s, hw), lambda bi, h, j: (bi, ng + h)),
            pl.BlockSpec((s, hw), lambda bi, h, j: (bi, 2 * ng + h)),
            pl.BlockSpec((lc, hw), lambda bi, h, j: (cblk + bi, ng + h)),
            pl.BlockSpec((lc, hw), lambda bi, h, j: (cblk + bi, 2 * ng + h)),
            pl.BlockSpec((1, hps, sub, kwin), lambda bi, h, j: (jnp.where(j == 0, 0, 1), h, 0, 0)),
            pl.BlockSpec((1, hps, sub, kwin), lambda bi, h, j: (jnp.where(j == nj - 1, 2, 1), h, 0, 0)),
        ],
        out_specs=pl.BlockSpec((bq, hw), lambda bi, h, j: (bi * nj + j, h)),
        out_shape=jax.ShapeDtypeStruct((t, n_heads * _LANES), _BF),
        compiler_params=_cp("arbitrary", "arbitrary", "arbitrary"),
        name="na_attn",
    )(QKV, QKV, QKV, QKV, QKV, bias_tab, bias_tab)


def _ctx_attn_body(group, has_sink, *refs):
    if has_sink:
        q_ref, k_ref, v_ref, sk_ref, _, o_ref = refs
    else:
        q_ref, k_ref, v_ref, _, o_ref = refs
    k = k_ref[...]
    vx = _with_ones(v_ref[...])
    for g in range(group):
        c = g * _LANES
        s = _dot_nt(q_ref[:, c:c + _LANES], k)
        m = jnp.max(s, axis=-1, keepdims=True)
        if has_sink:
            sk = sk_ref[0][:, c:c + 1]
            m = jnp.maximum(m, sk)
        acc = _dot(jnp.exp2(s - m).astype(_BF), vx)
        l = acc[:, _LANES:]
        if has_sink:
            l = l + jnp.exp2(sk - m)
        o_ref[:, c:c + _LANES] = (acc[:, :_LANES] / l).astype(o_ref.dtype)


def _ctx_attention(QKV, O, geo, n_heads, n_kv, sink=None):
    b, lc = geo["batch"], geo["ctx_len"]
    group = n_heads // n_kv
    gw = group * _LANES
    cblk = geo["n_x_rows"] // lc
    in_specs = [
        pl.BlockSpec((lc, gw), lambda bi, n: (cblk + bi, n)),
        pl.BlockSpec((lc, _LANES), lambda bi, n: (cblk + bi, n_heads + n)),
        pl.BlockSpec((lc, _LANES), lambda bi, n: (cblk + bi, n_heads + n_kv + n)),
    ]
    args = [QKV, QKV, QKV]
    if sink is not None:
        in_specs.append(pl.BlockSpec((1, 1, gw), lambda bi, n: (n, 0, 0)))
        args.append(sink)
    in_specs.append(pl.BlockSpec(memory_space=pl.ANY))
    args.append(O)
    return pl.pallas_call(
        functools.partial(_ctx_attn_body, group, sink is not None),
        grid=(b, n_kv),
        in_specs=in_specs,
        out_specs=pl.BlockSpec((lc, gw), lambda bi, n: (cblk + bi, n)),
        out_shape=jax.ShapeDtypeStruct(O.shape, O.dtype),
        input_output_aliases={len(args) - 1: 0},
        compiler_params=_cp("arbitrary", "arbitrary"),
        name="ctx_attn",
    )(*args)


def _swa_body(seq, bq, win, group, q_ref, k_ref, v_ref, kc_ref, vc_ref, sk_ref, o_ref):
    t = pl.program_id(2)
    kw = bq + 2 * win
    ks = jnp.clip(t * bq - win, 0, seq - kw)
    ks = pl.multiple_of(ks, _LANES)
    k = k_ref[pl.ds(ks, kw), :]
    v = v_ref[pl.ds(ks, kw), :]
    kc = kc_ref[...]
    vx = _with_ones(v)
    vcx = _with_ones(vc_ref[...])
    qpos = t * bq + lax.broadcasted_iota(jnp.int32, (bq, 1), 0)
    kpos = ks + lax.broadcasted_iota(jnp.int32, (1, kw), 1)
    band = jnp.abs(kpos - qpos) <= win
    for g in range(group):
        c = g * _LANES
        q = q_ref[:, c:c + _LANES]
        s = jnp.where(band, _dot_nt(q, k), _NEG)
        sc = _dot_nt(q, kc)
        sk = sk_ref[0][:, c:c + 1]
        m = jnp.maximum(jnp.maximum(jnp.max(s, axis=-1, keepdims=True), jnp.max(sc, axis=-1, keepdims=True)), sk)
        p = jnp.exp2(s - m).astype(_BF)
        pc = jnp.exp2(sc - m).astype(_BF)
        acc = _dot(p, vx) + _dot(pc, vcx)
        l = acc[:, _LANES:] + jnp.exp2(sk - m)
        o_ref[:, c:c + _LANES] = (acc[:, :_LANES] / l).astype(o_ref.dtype)


def _swa_attention(QKV, sink, geo, n_heads, n_kv, bq):
    b, s, lc = geo["batch"], geo["seq"], geo["ctx_len"]
    t = QKV.shape[0]
    group = n_heads // n_kv
    gw = group * _LANES
    nq = s // bq
    cblk = geo["n_x_rows"] // lc
    return pl.pallas_call(
        functools.partial(_swa_body, s, bq, _SWA_WINDOW, group),
        grid=(b, n_kv, nq),
        in_specs=[
            pl.BlockSpec((bq, gw), lambda bi, n, j: (bi * nq + j, n)),
            pl.BlockSpec((s, _LANES), lambda bi, n, j: (bi, n_heads + n)),
            pl.BlockSpec((s, _LANES), lambda bi, n, j: (bi, n_heads + n_kv + n)),
            pl.BlockSpec((lc, _LANES), lambda bi, n, j: (cblk + bi, n_heads + n)),
            pl.BlockSpec((lc, _LANES), lambda bi, n, j: (cblk + bi, n_heads + n_kv + n)),
            pl.BlockSpec((1, 1, gw), lambda bi, n, j: (n, 0, 0)),
        ],
        out_specs=pl.BlockSpec((bq, gw), lambda bi, n, j: (bi * nq + j, n)),
        out_shape=jax.ShapeDtypeStruct((t, n_heads * _LANES), _BF),
        compiler_params=_cp("arbitrary", "arbitrary", "arbitrary"),
        name="swa_attn",
    )(QKV, QKV, QKV, QKV, QKV, sink)


def _gqa_body(seq, ck, group, q_ref, k_ref, v_ref, kc_ref, vc_ref, o_ref, m_ref, acc_ref):
    m_ref[...] = jnp.full(m_ref.shape, _NEG, _F32)
    acc_ref[...] = jnp.zeros(acc_ref.shape, _F32)

    def step(k, v):
        width = k.shape[0]
        vx = jnp.concatenate([v, jnp.ones((width, _LANES), v.dtype)], axis=1)
        for g in range(group):
            s = _dot_nt(q_ref[:, g * _LANES:(g + 1) * _LANES], k)
            m_old = m_ref[g]
            m_new = jnp.maximum(m_old, jnp.max(s, axis=-1, keepdims=True))
            alpha = jnp.exp2(m_old - m_new)
            p = jnp.exp2(s - _lane_tiles(m_new, width // _LANES)).astype(_BF)
            acc_ref[g] = _lane_tiles(alpha, 2) * acc_ref[g] + _dot(p, vx)
            m_ref[g] = m_new

    def chunk(c, carry):
        off = pl.multiple_of(c * ck, ck)
        step(k_ref[pl.ds(off, ck), :], v_ref[pl.ds(off, ck), :])
        return carry

    lax.fori_loop(0, seq // ck, chunk, 0, unroll=min(4, seq // ck))
    step(kc_ref[...], vc_ref[...])
    for g in range(group):
        acc = acc_ref[g]
        o_ref[:, g * _LANES:(g + 1) * _LANES] = (acc[:, :_LANES] / acc[:, _LANES:]).astype(o_ref.dtype)


def _gqa_attention(QKV, geo, n_heads, n_kv, bq, ck):
    b, s, lc = geo["batch"], geo["seq"], geo["ctx_len"]
    group = n_heads // n_kv
    gw = group * _LANES
    nq = s // bq
    cblk = geo["n_x_rows"] // lc
    return pl.pallas_call(
        functools.partial(_gqa_body, s, ck, group),
        grid=(b, n_kv, nq),
        in_specs=[
            pl.BlockSpec((bq, gw), lambda bi, n, j: (bi * nq + j, n)),
            pl.BlockSpec((s, _LANES), lambda bi, n, j: (bi, n_heads + n)),
            pl.BlockSpec((s, _LANES), lambda bi, n, j: (bi, n_heads + n_kv + n)),
            pl.BlockSpec((lc, _LANES), lambda bi, n, j: (cblk + bi, n_heads + n)),
            pl.BlockSpec((lc, _LANES), lambda bi, n, j: (cblk + bi, n_heads + n_kv + n)),
        ],
        out_specs=pl.BlockSpec((bq, gw), lambda bi, n, j: (bi * nq + j, n)),
        out_shape=jax.ShapeDtypeStruct((geo["n_x_rows"], n_heads * _LANES), _BF),
        scratch_shapes=[pltpu.VMEM((group, bq, _LANES), _F32), pltpu.VMEM((group, bq, 2 * _LANES), _F32)],
        compiler_params=_cp("arbitrary", "arbitrary", "arbitrary"),
        name="gqa_attn",
    )(QKV, QKV, QKV, QKV, QKV)


def _log_sigmoid(x):
    return jnp.minimum(x, 0.0) - jnp.log1p(jnp.exp(-jnp.abs(x)))


def _mlstm_chunk(d, st, n_heads, L, h, q_ref, k_ref, v_ref, gt_ref, gb_ref, o_ref, c_ref, n_ref, m_ref):
    gates = gt_ref[...] + gb_ref[...]
    lane = lax.broadcasted_iota(jnp.int32, (1, _LANES), 1)
    li_col = jnp.sum(jnp.where(lane == (2 * d) * n_heads + h, gates, 0.0), axis=-1, keepdims=True)
    lf_pre = jnp.sum(jnp.where(lane == (2 * d + 1) * n_heads + h, gates, 0.0), axis=-1, keepdims=True)
    lf_col = _log_sigmoid(lf_pre)
    ti = lax.broadcasted_iota(jnp.int32, (L, L), 0)
    si = lax.broadcasted_iota(jnp.int32, (L, L), 1)
    eye = ti == si
    li_row = jnp.sum(jnp.where(eye, li_col, 0.0), axis=0, keepdims=True)
    lf_row = jnp.sum(jnp.where(eye, lf_col, 0.0), axis=0, keepdims=True)
    allowed = (ti >= si) if d == 0 else (ti <= si)
    allowed_t = (si >= ti) if d == 0 else (si <= ti)
    b_col = jnp.sum(jnp.where(allowed, lf_row, 0.0), axis=-1, keepdims=True)
    b_row = jnp.sum(jnp.where(allowed_t, lf_col, 0.0), axis=0, keepdims=True)
    total = jnp.sum(lf_col, axis=0, keepdims=True)
    m_prev = m_ref[st, 0:1, 0:1]

    q = q_ref[...]
    k = k_ref[...]
    v = v_ref[...]
    dmat = jnp.where(allowed, b_col - b_row + li_row, _NEG)
    g_col = b_col + m_prev
    m_t = jnp.maximum(g_col, jnp.max(dmat, axis=-1, keepdims=True))
    w = jnp.exp(dmat - m_t) * _dot_nt(q, k)
    w_prev = jnp.exp(g_col - m_t)
    cmat = c_ref[st]
    nvec = n_ref[st, 0:1, :]
    num = _dot(w.astype(_BF), v) + w_prev * _dot(q, cmat.astype(_BF))
    qn = jnp.sum(q.astype(_F32) * nvec, axis=-1, keepdims=True)
    den = jnp.sum(w, axis=-1, keepdims=True) + w_prev * qn
    o_ref[...] = (num / jnp.maximum(jnp.abs(den), jnp.exp(-m_t))).astype(o_ref.dtype)

    lw = total - b_col + li_col
    m_new = jnp.maximum(total + m_prev, jnp.max(lw, axis=0, keepdims=True))
    decay = jnp.exp(total + m_prev - m_new)
    kw = k.astype(_F32) * jnp.exp(lw - m_new)
    c_ref[st] = decay * cmat + _dot_tn(kw.astype(_BF), v)
    n_ref[st] = jnp.broadcast_to(decay * nvec + jnp.sum(kw, axis=0, keepdims=True), n_ref.shape[1:])
    m_ref[st] = jnp.broadcast_to(m_new, m_ref.shape[1:])


def _mlstm_body(n_heads, hps, dqk, dv, chunk, qf_ref, kf_ref, vf_ref, gf_ref, qb_ref, kb_ref, vb_ref, gbk_ref,
                gb_ref, of_ref, ob_ref, c_ref, n_ref, m_ref):
    hg = pl.program_id(1)

    @pl.when(pl.program_id(2) == 0)
    def _():
        c_ref[...] = jnp.zeros(c_ref.shape, _F32)
        n_ref[...] = jnp.zeros(n_ref.shape, _F32)
        m_ref[...] = jnp.zeros(m_ref.shape, _F32)

    dirs = ((qf_ref, kf_ref, vf_ref, gf_ref, of_ref), (qb_ref, kb_ref, vb_ref, gbk_ref, ob_ref))
    for hh in range(hps):
        qs, vs = slice(hh * dqk, (hh + 1) * dqk), slice(hh * dv, (hh + 1) * dv)
        for d, (q_ref, k_ref, v_ref, g_ref, o_ref) in enumerate(dirs):
            _mlstm_chunk(d, d * hps + hh, n_heads, chunk, hg * hps + hh, q_ref.at[:, qs], k_ref.at[:, qs],
                         v_ref.at[:, vs], g_ref, gb_ref, o_ref.at[:, vs], c_ref, n_ref, m_ref)


def _mlstm_scan(P, G, gate_b, geo, n_heads, dqk, dv, chunk):
    b, s, lc = geo["batch"], geo["seq"], geo["ctx_len"]
    t = P.shape[0]
    nctx, nx = lc // chunk, s // chunk
    cbase = geo["n_x_rows"] // chunk

    def rblk(d, bi, c):
        cc = c - nctx
        in_ctx = cbase + bi * nctx + (c if d == 0 else nctx - 1 - c)
        in_x = bi * nx + (cc if d == 0 else nx - 1 - cc)
        return jnp.where(c < nctx, in_ctx, in_x)

    hps = 2
    ng = n_heads // hps
    k_off = ng
    v_off = 2 * n_heads * dqk // (hps * dv)
    in_specs, args = [], []
    for d in (0, 1):
        in_specs += [
            pl.BlockSpec((chunk, hps * dqk), lambda bi, h, c, d=d: (rblk(d, bi, c), h)),
            pl.BlockSpec((chunk, hps * dqk), lambda bi, h, c, d=d: (rblk(d, bi, c), k_off + h)),
            pl.BlockSpec((chunk, hps * dv), lambda bi, h, c, d=d: (rblk(d, bi, c), v_off + h)),
            pl.BlockSpec((chunk, _LANES), lambda bi, h, c, d=d: (rblk(d, bi, c), 0)),
        ]
        args += [P, P, P, G]
    in_specs.append(pl.BlockSpec((1, _LANES), lambda bi, h, c: (0, 0)))
    args.append(gate_b)
    out_sds = jax.ShapeDtypeStruct((t, n_heads * dv), _F32)
    ns = 2 * hps
    return pl.pallas_call(
        functools.partial(_mlstm_body, n_heads, hps, dqk, dv, chunk),
        grid=(b, ng, nctx + nx),
        in_specs=in_specs,
        out_specs=[pl.BlockSpec((chunk, hps * dv), lambda bi, h, c, d=d: (rblk(d, bi, c), h)) for d in (0, 1)],
        out_shape=[out_sds, out_sds],
        scratch_shapes=[pltpu.VMEM((ns, dqk, dv), _F32), pltpu.VMEM((ns, 8, dqk), _F32), pltpu.VMEM((ns, 8, _LANES), _F32)],
        compiler_params=_cp("arbitrary", "arbitrary", "arbitrary"),
        name="mlstm_scan",
    )(*args)


def _ml_out_body(n_heads, dv, hf_ref, hb_ref, og_ref, hg_ref, w_ref, x_ref, gt_ref, o_ref):
    hsum = hf_ref[...] + hb_ref[...]
    parts = []
    for hh in range(n_heads):
        a = hsum[:, hh * dv:(hh + 1) * dv]
        hn = a * lax.rsqrt(jnp.mean(a * a, axis=-1, keepdims=True) + _EPS) * hg_ref[:, hh * dv:(hh + 1) * dv]
        og = og_ref[:, hh * dv:(hh + 1) * dv].astype(_F32)
        parts.append((jax.nn.sigmoid(og) * hn).astype(_BF))
    a = jnp.concatenate(parts, axis=-1)
    o_ref[...] = x_ref[...] + gt_ref[0] * _dot(a, w_ref[...])


def _ml_out(Hf, Hb, P, head_g, W, X, gate, geo, n_heads, dv, bm):
    t, d = X.shape
    kd = n_heads * dv
    o_off = (P.shape[1] - kd) // kd
    mrow = functools.partial(geo["mrow_bm"], bm)
    return pl.pallas_call(
        functools.partial(_ml_out_body, n_heads, dv),
        grid=(t // bm,),
        in_specs=[
            pl.BlockSpec((bm, kd), lambda i: (i, 0)),
            pl.BlockSpec((bm, kd), lambda i: (i, 0)),
            pl.BlockSpec((bm, kd), lambda i: (i, o_off)),
            pl.BlockSpec((1, kd), lambda i: (0, 0)),
            pl.BlockSpec((kd, d), lambda i: (0, 0)),
            pl.BlockSpec((bm, d), lambda i: (i, 0)),
            pl.BlockSpec((1, 1, d), lambda i: (mrow(i), 0, 0)),
        ],
        out_specs=pl.BlockSpec((bm, d), lambda i: (i, 0)),
        out_shape=jax.ShapeDtypeStruct((t, d), _F32),
        compiler_params=_cp("arbitrary"),
        name="ml_out",
    )(Hf, Hb, P, head_g, W, X, gate)


def _rope_tables(seq, pad_rows):
    tpos = np.arange(seq)
    row = (tpos // _GRID_W).astype(np.float32)
    col = (tpos % _GRID_W).astype(np.float32)
    nf = _LANES // 4
    inv = (_ROPE_BASE ** (-jnp.arange(nf, dtype=_F32) / nf))
    ar = jnp.asarray(row)[:, None] * inv
    ac = jnp.asarray(col)[:, None] * inv
    cr, sr, cc, sc = jnp.cos(ar), jnp.sin(ar), jnp.cos(ac), jnp.sin(ac)
    cs = jnp.concatenate([cr, cc, cr, cc], axis=1)
    sn = jnp.concatenate([-sr, -sc, sr, sc], axis=1)
    cs = jnp.concatenate([cs, jnp.ones((pad_rows, _LANES), _F32)], axis=0)
    sn = jnp.concatenate([sn, jnp.zeros((pad_rows, _LANES), _F32)], axis=0)
    return cs, sn


def _pair_major(v):
    q = _LANES // 4
    return v.reshape(v.shape[:-1] + (v.shape[-1] // _LANES, 2, 2, q)).swapaxes(-3, -2).reshape(v.shape)


def _rope_operands(w_qkv, q_gain, k_gain, n_qk_heads, cs, sn):
    nqk = n_qk_heads * _LANES
    w = jnp.concatenate([_pair_major(w_qkv[:, :nqk]), w_qkv[:, nqk:]], axis=1).astype(_BF)
    tabs = []
    for gain in (q_gain, k_gain):
        gp = _pair_major(gain.astype(_F32))
        tabs += [cs * gp[None, :], sn * jnp.roll(gp, _LANES // 2)[None, :]]
    return w, tuple(tabs)


def _geometry(batch, seq, ctx_len):
    bm = min(1024, batch * ctx_len)
    assert seq % bm == 0 and (batch * ctx_len) % bm == 0
    assert seq & (seq - 1) == 0 and ctx_len & (ctx_len - 1) == 0
    n_x_rows = batch * seq

    def mrow_bm(bm_, i):
        return jnp.minimum((i * bm_) // seq, batch)

    return dict(batch=batch, seq=seq, ctx_len=ctx_len, bm=bm, n_x_rows=n_x_rows,
                n_x_tiles=n_x_rows // bm, tiles_per_seq=seq // bm,
                mrow=functools.partial(mrow_bm, bm), mrow_bm=mrow_bm)


def kernel(x, c, ctx, c_ctx, ada_w, ada_b, norm1_g, norm2_g, ffn_w_in, ffn_conv_w, ffn_conv_b, ffn_w_out,
           na_w_qkv, na_q_g, na_k_g, na_rel_bias, na_w_o,
           swa_w_qkv, swa_q_g, swa_k_g, swa_sinks, swa_w_o,
           ml_w_in, ml_gate_b, ml_head_g, ml_w_o,
           gqa_w_qkv, gqa_q_g, gqa_k_g, gqa_w_o):
    batch, seq, d = x.shape
    ctx_len = ctx.shape[1]
    depth = ada_w.shape[0]
    n_heads = d // _LANES
    geo = _geometry(batch, seq, ctx_len)
    bm = geo["bm"]
    n_x_rows = geo["n_x_rows"]
    n_x_tiles = geo["n_x_tiles"]
    qscale = (_LANES ** -0.5) * _LOG2E

    X = (x.reshape(n_x_rows, d), ctx.reshape(batch * ctx_len, d))
    t_all = n_x_rows + batch * ctx_len
    nrt_all = t_all // bm

    cond = jnp.concatenate([c, c_ctx[None, :], jnp.zeros((8 - batch - 1, d), _F32)], axis=0)
    mods = _adaln(cond, ada_w, ada_b)

    cs, sn = _rope_tables(seq, bm)
    f = ffn_conv_w.shape[2]
    w_in_b = ffn_w_in.astype(_BF)
    w_out_b = ffn_w_out.astype(_BF)
    conv_b3 = ffn_conv_b.reshape(depth, 1, f)
    bf = 512
    bm_o = min(512, bm)

    for i in range(depth):
        kind, jl = i % 4, i // 4
        need_ctx = i < depth - 1
        mod = [mods[i, :batch + 1, k * d:(k + 1) * d].reshape(batch + 1, 1, d) for k in range(6)]
        sh1, sc1, g1, sh2, sc2, g2 = mod
        n1 = norm1_g[i].reshape(1, d)
        n2 = norm2_g[i].reshape(1, d)
        nrt_o = t_all // bm_o if need_ctx else n_x_rows // bm_o

        if kind == 0:
            w = na_w_qkv[jl].astype(_BF)
            gq = (na_q_g[jl] * qscale).reshape(1, _LANES)
            gk = na_k_g[jl].reshape(1, _LANES)
            bn = 1024
            QKV = _proj(X, n1, sh1, sc1, w, _qkv_layouts(n_heads, n_heads, bn, False), geo, bn, _BF, gains=(gq, gk))
            rows = seq // _GRID_W
            tab = _na_bias_table(na_rel_bias[jl], rows, 4, 12)
            O = _na_attention(QKV, tab, geo, n_heads)
            if need_ctx:
                O = _ctx_attention(QKV, O, geo, n_heads, n_heads)
            X = _oproj(O, na_w_o[jl].astype(_BF), X, g1, geo, nrt_o, bm_o, d)
        elif kind == 1:
            n_kv = (swa_w_qkv.shape[2] // _LANES - n_heads) // 2
            w, tabs = _rope_operands(swa_w_qkv[jl], swa_q_g[jl] * qscale, swa_k_g[jl], n_heads + n_kv, cs, sn)
            bn = _wide_tile(w.shape[1])
            QKV = _proj(X, n1, sh1, sc1, w, _qkv_layouts(n_heads, n_kv, bn, True), geo, bn, _BF, rope=tabs)
            group = n_heads // n_kv
            sink = jnp.repeat(swa_sinks[jl].astype(_F32) * _LOG2E, _LANES).reshape(n_kv, 1, group * _LANES)
            O = _swa_attention(QKV, sink, geo, n_heads, n_kv, 256)
            if need_ctx:
                O = _ctx_attention(QKV, O, geo, n_heads, n_kv, sink)
            X = _oproj(O, swa_w_o[jl].astype(_BF), X, g1, geo, nrt_o, bm_o, d)
        elif kind == 2:
            mh = ml_gate_b.shape[1] // 4
            dv = d // mh
            dqk = dv // 2
            nmain = 2 * mh * dqk + 2 * mh * dv
            w_main = ml_w_in[jl][:, :nmain].astype(_BF)
            w_gate = jnp.pad(ml_w_in[jl][:, nmain:], ((0, 0), (0, _LANES - 4 * mh))).astype(_BF)
            bn = 1024
            segs_by_tile = []
            k0, k1 = mh * dqk, 2 * mh * dqk
            for jt in range(nmain // bn):
                lo, hi = jt * bn, (jt + 1) * bn
                cuts = sorted({lo, hi, min(max(k0, lo), hi), min(max(k1, lo), hi)})
                segs = tuple((a - lo, b_ - lo, "plain", 0, False, dqk ** -0.5 if k0 <= a < k1 else 1.0)
                             for a, b_ in zip(cuts[:-1], cuts[1:]))
                segs_by_tile.append((jt, jt + 1, segs))
            P = _proj(X, n1, sh1, sc1, w_main, tuple(segs_by_tile), geo, bn, _BF)
            G = _proj(X, n1, sh1, sc1, w_gate, ((0, 1, ((0, _LANES, "plain", 0, False, 1.0),)),), geo, _LANES, _F32)
            gb = jnp.pad(ml_gate_b[jl].astype(_F32), (0, _LANES - 4 * mh)).reshape(1, _LANES)
            Hf, Hb = _mlstm_scan(P, G, gb, geo, mh, dqk, dv, 256)
            X = _ml_out(Hf, Hb, P, ml_head_g[jl].reshape(1, mh * dv), ml_w_o[jl].astype(_BF), X, g1, geo, mh, dv, bm_o)
        else:
            n_kv = (gqa_w_qkv.shape[2] // _LANES - n_heads) // 2
            w, tabs = _rope_operands(gqa_w_qkv[jl], gqa_q_g[jl] * qscale, gqa_k_g[jl], n_heads + n_kv, cs, sn)
            bn = _wide_tile(w.shape[1])
            QKV = _proj(X, n1, sh1, sc1, w, _qkv_layouts(n_heads, n_kv, bn, True), geo, bn, _BF, rope=tabs)
            O = _gqa_attention(QKV, geo, n_heads, n_kv, min(1024, seq), 512)
            if need_ctx:
                Oc = jnp.zeros((t_all, d), _BF).at[:n_x_rows].set(O)
                O = _ctx_attention(QKV, Oc, geo, n_heads, n_kv)
            X = _oproj(O, gqa_w_o[jl].astype(_BF), X, g1, geo, nrt_o, bm_o, d)

        nrt_f = nrt_all if need_ctx else n_x_tiles
        A = _ffn1(X, n2, sh2, sc2, w_in_b, ffn_conv_w, conv_b3, i, geo, nrt_f, bf)
        X = _oproj(A, w_out_b, X, g2, geo, A.shape[0] // bm, bm, 512, layer=i)

    return X[:n_x_rows].reshape(batch, seq, d)
```

```python
import functools
import math

import numpy as np
import jax
import jax.numpy as jnp
from jax import lax
from jax.experimental import pallas as pl
from jax.experimental.pallas import tpu as pltpu

_F32 = jnp.float32
_BF = jnp.bfloat16
_EPS = 1e-6
_NEG = -1e30
_LOG2E = 1.4426950408889634
_GRID_W = 64
_SWA_WINDOW = 128
_ROPE_BASE = 10000.0
_LANES = 128
_HALO = 8
_VMEM_LIMIT = 56 << 20


def _cp(*sem):
    return pltpu.CompilerParams(dimension_semantics=sem, vmem_limit_bytes=_VMEM_LIMIT)


def _dot(a, b):
    return jnp.dot(a, b, preferred_element_type=_F32)


def _dot_nt(a, b):
    return lax.dot_general(a, b, (((1,), (1,)), ((), ())), preferred_element_type=_F32)


def _dot_tn(a, b):
    return lax.dot_general(a, b, (((0,), (0,)), ((), ())), preferred_element_type=_F32)


def _norm_mod(x, g, sh, sc):
    ms = jnp.mean(x * x, axis=-1, keepdims=True)
    return (x * lax.rsqrt(ms + _EPS)) * (g * (1.0 + sc)) + sh


def _adaln_body(c_ref, w_ref, b_ref, o_ref):
    c = c_ref[...]
    s = (c * jax.nn.sigmoid(c)).astype(_BF)
    o_ref[0] = _dot(s, w_ref[0].astype(_BF)) + b_ref[0]


def _adaln(cond, ada_w, ada_b):
    depth, d, n = ada_w.shape
    bn = 1024
    return pl.pallas_call(
        _adaln_body,
        grid=(depth, n // bn),
        in_specs=[
            pl.BlockSpec((8, d), lambda l, j: (0, 0)),
            pl.BlockSpec((1, d, bn), lambda l, j: (l, 0, j)),
            pl.BlockSpec((1, 1, bn), lambda l, j: (l, 0, j)),
        ],
        out_specs=pl.BlockSpec((1, 8, bn), lambda l, j: (l, 0, j)),
        out_shape=jax.ShapeDtypeStruct((depth, 8, n), _F32),
        compiler_params=_cp("arbitrary", "arbitrary"),
        name="adaln",
    )(cond, ada_w, ada_b.reshape(depth, 1, n))


def _next_rows_chunks(bm, nj):
    nc = 1
    while nc * 2 <= nj and bm // (nc * 2) >= _LANES:
        nc *= 2
    return nc, bm // nc


def _proj_body(layouts, has_rope, n_gain, n_lat_tiles, *refs):
    x_refs = refs[:1] if n_lat_tiles is None else refs[:2]
    refs = refs[len(x_refs):]
    g_ref, sh_ref, sc_ref, w_ref = refs[:4]
    pos = 4
    tab_refs = ()
    if has_rope:
        tab_refs = refs[pos:pos + 4]
        pos += 4
    gain_refs = refs[pos:pos + n_gain]
    pos += n_gain
    o_ref, h_ref = refs[pos], refs[pos + 1]
    j = pl.program_id(1)

    is_lat = True if n_lat_tiles is None else pl.program_id(0) < n_lat_tiles
    for src_ref, mine in zip(x_refs, (is_lat, jnp.logical_not(is_lat))):
        @pl.when((j == 0) & mine)
        def _(src_ref=src_ref):
            h_ref[...] = _norm_mod(src_ref[...], g_ref[...], sh_ref[0], sc_ref[0]).astype(_BF)

    for lo, hi, segs in layouts:
        @pl.when((j >= lo) & (j < hi))
        def _(segs=segs):
            acc = _dot(h_ref[...], w_ref[...])
            for c0, c1, kind, gi, rope, mult in segs:
                if kind == "plain":
                    a = acc[:, c0:c1]
                    if mult != 1.0:
                        a = a * mult
                    o_ref[:, c0:c1] = a.astype(o_ref.dtype)
                    continue
                for c in range(c0, c1, _LANES):
                    a = acc[:, c:c + _LANES]
                    inv = lax.rsqrt(jnp.mean(a * a, axis=-1, keepdims=True) + _EPS)
                    if rope:
                        y = a * tab_refs[2 * gi][...] + pltpu.roll(a, _LANES // 2, 1) * tab_refs[2 * gi + 1][...]
                    else:
                        y = a * gain_refs[gi][...]
                    o_ref[:, c:c + _LANES] = (y * inv).astype(o_ref.dtype)


def _proj(X, g, sh, sc, W, layouts, geo, bn, out_dtype, rope=None, gains=()):
    n = W.shape[1]
    bm = geo["bm"]
    nj = n // bn
    mrow = geo["mrow"]
    if isinstance(X, tuple):
        nlt = X[0].shape[0] // bm
        t, d = X[0].shape[0] + X[1].shape[0], X[0].shape[1]
        x_specs = [pl.BlockSpec((bm, d), lambda i, j: (jnp.minimum(i, nlt - 1), 0)),
                   pl.BlockSpec((bm, d), lambda i, j: (jnp.maximum(i - nlt, 0), 0),
                                pipeline_mode=pl.Buffered(1))]
        x_args = list(X)
    else:
        nlt = None
        t, d = X.shape
        x_specs = [pl.BlockSpec((bm, d), lambda i, j: (i, 0))]
        x_args = [X]
    nrt = t // bm
    in_specs = x_specs + [
        pl.BlockSpec((1, d), lambda i, j: (0, 0)),
        pl.BlockSpec((1, 1, d), lambda i, j: (mrow(i), 0, 0)),
        pl.BlockSpec((1, 1, d), lambda i, j: (mrow(i), 0, 0)),
        pl.BlockSpec((d, bn), lambda i, j: (0, j)),
    ]
    args = x_args + [g, sh, sc, W]
    if rope is not None:
        tps, nxt = geo["tiles_per_seq"], geo["n_x_tiles"]
        tab = lambda i, j: (jnp.where(i < nxt, i % tps, tps), 0)
        in_specs += [pl.BlockSpec((bm, _LANES), tab)] * 4
        args += list(rope)
    for gn in gains:
        in_specs.append(pl.BlockSpec((1, _LANES), lambda i, j: (0, 0)))
        args.append(gn)
    return pl.pallas_call(
        functools.partial(_proj_body, layouts, rope is not None, len(gains), nlt),
        grid=(nrt, nj),
        in_specs=in_specs,
        out_specs=pl.BlockSpec((bm, bn), lambda i, j: (i, j)),
        out_shape=jax.ShapeDtypeStruct((t, n), out_dtype),
        scratch_shapes=[pltpu.VMEM((bm, d), _BF)],
        compiler_params=_cp("arbitrary", "arbitrary"),
        name="proj",
    )(*args)


def _wide_tile(n, cap=1280):
    return max(b for b in range(_LANES, cap + 1, _LANES) if n % b == 0)


def _qkv_layouts(n_q, n_kv, bn, rope):
    hd = _LANES
    bounds = [(0, n_q * hd, "q"), (n_q * hd, (n_q + n_kv) * hd, "k"), ((n_q + n_kv) * hd, (n_q + 2 * n_kv) * hd, "v")]
    n = (n_q + 2 * n_kv) * hd
    per_tile = []
    for jt in range(n // bn):
        lo, hi = jt * bn, (jt + 1) * bn
        segs = []
        for b0, b1, nm in bounds:
            s0, s1 = max(lo, b0), min(hi, b1)
            if s0 < s1:
                if nm == "v":
                    segs.append((s0 - lo, s1 - lo, "plain", 0, False, 1.0))
                else:
                    segs.append((s0 - lo, s1 - lo, "head", 0 if nm == "q" else 1, rope, 1.0))
        per_tile.append(tuple(segs))
    layouts = []
    for jt, segs in enumerate(per_tile):
        if layouts and layouts[-1][2] == segs and layouts[-1][1] == jt:
            layouts[-1] = (layouts[-1][0], jt + 1, segs)
        else:
            layouts.append((jt, jt + 1, segs))
    return tuple(layouts)


def _oproj_body(n_lat_tiles, a_ref, w_ref, *refs):
    x_refs, (gt_ref, o_ref) = refs[:-2], refs[-2:]
    x = x_refs[0][...]
    if n_lat_tiles is not None:
        x = jnp.where(pl.program_id(0) < n_lat_tiles, x, x_refs[1][...])
    o_ref[...] = x + gt_ref[0] * _dot(a_ref[...], w_ref[...])


def _oproj(A, W, X, gate, geo, nrt, bm, bn, layer=None):
    k = A.shape[1]
    d = W.shape[-1]
    mrow = functools.partial(geo["mrow_bm"], bm)
    if layer is None:
        w_spec = pl.BlockSpec((k, bn), lambda i, j: (0, j))
    else:
        w_spec = pl.BlockSpec((None, k, bn), lambda i, j: (layer, 0, j))
    if isinstance(X, tuple):
        nlt = X[0].shape[0] // bm
        x_specs = [pl.BlockSpec((bm, bn), lambda i, j: (jnp.minimum(i, nlt - 1), j)),
                   pl.BlockSpec((bm, bn), lambda i, j: (jnp.maximum(i - nlt, 0), j))]
        x_args = list(X)
    else:
        nlt = None
        x_specs = [pl.BlockSpec((bm, bn), lambda i, j: (i, j))]
        x_args = [X]
    return pl.pallas_call(
        functools.partial(_oproj_body, nlt),
        grid=(nrt, d // bn),
        in_specs=[pl.BlockSpec((bm, k), lambda i, j: (i, 0)), w_spec] + x_specs
        + [pl.BlockSpec((1, 1, bn), lambda i, j: (mrow(i), 0, j))],
        out_specs=pl.BlockSpec((bm, bn), lambda i, j: (i, j)),
        out_shape=jax.ShapeDtypeStruct((nrt * bm, d), _F32),
        compiler_params=_cp("arbitrary", "arbitrary"),
        name="oproj",
    )(A, W, *x_args, gate)


def _ffn1_body(bm, n_x_rows, seq, ctx_len, nc, rc, x0_ref, x0p_ref, x0n_ref, xn_ref, xnp_ref, xnn_ref,
               g_ref, sh_ref, sc_ref, shn_ref, scn_ref, wg_ref, wu_ref, cw_ref, cb_ref, o_ref, ha_ref, hb_ref):
    i = pl.program_id(0)
    j = pl.program_id(1)
    h_refs = (ha_ref, hb_ref)

    @pl.when((i == 0) & (j == 0))
    def _():
        g, sh, sc = g_ref[...], sh_ref[0], sc_ref[0]
        ha_ref[0:bm, :] = _norm_mod(x0_ref[...], g, sh, sc).astype(_BF)
        halo = jnp.concatenate([x0p_ref[...], x0n_ref[...]], axis=0)
        ha_ref[bm:bm + 2 * _HALO, :] = _norm_mod(halo, g, sh, sc).astype(_BF)

    for par in (0, 1):
        @pl.when(i % 2 == par)
        def _(par=par):
            h_ref, hn_ref = h_refs[par], h_refs[1 - par]
            gx = _dot(h_ref[...], wg_ref[...])
            u = _dot(h_ref[0:bm, :], wu_ref[...])
            g, shn, scn = g_ref[...], shn_ref[0], scn_ref[0]
            r0 = pl.multiple_of(jnp.minimum(j, nc - 1) * rc, rc)
            hn_ref[pl.ds(r0, rc), :] = _norm_mod(xn_ref[...], g, shn, scn).astype(_BF)
            halo = jnp.concatenate([xnp_ref[...], xnn_ref[...]], axis=0)
            hn_ref[bm:bm + 2 * _HALO, :] = _norm_mod(halo, g, shn, scn).astype(_BF)
            gm = gx[0:bm]
            g_prev = gx[bm + _HALO - 1:bm + _HALO]
            g_next = gx[bm + _HALO:bm + _HALO + 1]
            row = lax.broadcasted_iota(jnp.int32, (bm, 1), 0)
            tok = i * bm + row
            period = jnp.where(i * bm >= n_x_rows, ctx_len, seq)
            up = jnp.where(row == 0, g_prev, pltpu.roll(gm, 1, 0))
            dn = jnp.where(row == bm - 1, g_next, pltpu.roll(gm, bm - 1, 0))
            up = jnp.where((tok & (period - 1)) != 0, up, 0.0)
            dn = jnp.where(((tok + 1) & (period - 1)) != 0, dn, 0.0)
            cw = cw_ref[...]
            gc = cb_ref[...] + up * cw[0:1] + gm * cw[1:2] + dn * cw[2:3]
            o_ref[...] = (jax.nn.gelu(gc) * u).astype(o_ref.dtype)


def _ffn1(X, g, sh, sc, w_in, conv_w, conv_b, layer, geo, nrt, bf):
    t, d = X.shape
    f = conv_w.shape[2]
    bm = geo["bm"]
    mrow = geo["mrow"]
    hb = bm // _HALO
    last = t // _HALO - 1
    nf = f // bf
    nc, rc = _next_rows_chunks(bm, nf)
    nxt = lambda i: jnp.minimum(i + 1, nrt - 1)
    body = functools.partial(_ffn1_body, bm, geo["n_x_rows"], geo["seq"], geo["ctx_len"], nc, rc)
    return pl.pallas_call(
        body,
        grid=(nrt, nf),
        in_specs=[
            pl.BlockSpec((bm, d), lambda i, j: (0, 0)),
            pl.BlockSpec((_HALO, d), lambda i, j: (0, 0)),
            pl.BlockSpec((_HALO, d), lambda i, j: (jnp.minimum(hb, last), 0)),
            pl.BlockSpec((rc, d), lambda i, j: (nxt(i) * nc + jnp.minimum(j, nc - 1), 0)),
            pl.BlockSpec((_HALO, d), lambda i, j: (jnp.maximum(nxt(i) * hb - 1, 0), 0)),
            pl.BlockSpec((_HALO, d), lambda i, j: (jnp.minimum((nxt(i) + 1) * hb, last), 0)),
            pl.BlockSpec((1, d), lambda i, j: (0, 0)),
            pl.BlockSpec((1, 1, d), lambda i, j: (mrow(i), 0, 0)),
            pl.BlockSpec((1, 1, d), lambda i, j: (mrow(i), 0, 0)),
            pl.BlockSpec((1, 1, d), lambda i, j: (mrow(nxt(i)), 0, 0)),
            pl.BlockSpec((1, 1, d), lambda i, j: (mrow(nxt(i)), 0, 0)),
            pl.BlockSpec((None, d, bf), lambda i, j: (layer, 0, j)),
            pl.BlockSpec((None, d, bf), lambda i, j: (layer, 0, j + nf)),
            pl.BlockSpec((None, 3, bf), lambda i, j: (layer, 0, j)),
            pl.BlockSpec((None, 1, bf), lambda i, j: (layer, 0, j)),
        ],
        out_specs=pl.BlockSpec((bm, bf), lambda i, j: (i, j)),
        out_shape=jax.ShapeDtypeStruct((nrt * bm, f), _BF),
        scratch_shapes=[pltpu.VMEM((bm + 2 * _HALO, d), _BF), pltpu.VMEM((bm + 2 * _HALO, d), _BF)],
        compiler_params=_cp("arbitrary", "arbitrary"),
        name="ffn1",
    )(X, X, X, X, X, X, g, sh, sc, sh, sc, w_in, w_in, conv_w, conv_b)


def _with_ones(v):
    return jnp.concatenate([v, jnp.ones(v.shape, v.dtype)], axis=1)


def _lane_tiles(x, n):
    return jnp.concatenate([x] * n, axis=1) if n > 1 else x


def _na_body(seq, sub, kwin, hps, nsub, q_ref, k_ref, v_ref, kc_ref, vc_ref, ba_ref, bm_ref, bb_ref, o_ref):
    j = pl.program_id(2)
    b_refs = (ba_ref,) + (bm_ref,) * (nsub - 2) + (bb_ref,)
    for hh in range(hps):
        c = hh * _LANES
        kc = kc_ref[:, c:c + _LANES]
        vcx = _with_ones(vc_ref[:, c:c + _LANES])
        for i, b_ref in enumerate(b_refs):
            q0 = (nsub * j + i) * sub
            ks = pl.multiple_of(jnp.clip(q0 - (kwin - sub) // 2, 0, seq - kwin), 256)
            q = q_ref[i * sub:(i + 1) * sub, c:c + _LANES]
            k = k_ref[pl.ds(ks, kwin), c:c + _LANES]
            v = v_ref[pl.ds(ks, kwin), c:c + _LANES]
            s = _dot_nt(q, k) + b_ref[0, hh]
            sc = _dot_nt(q, kc)
            m = jnp.maximum(jnp.max(s, axis=-1, keepdims=True), jnp.max(sc, axis=-1, keepdims=True))
            p = jnp.exp2(s - m).astype(_BF)
            pc = jnp.exp2(sc - m).astype(_BF)
            acc = _dot(p, _with_ones(v)) + _dot(pc, vcx)
            o_ref[i * sub:(i + 1) * sub, c:c + _LANES] = (acc[:, :_LANES] / acc[:, _LANES:]).astype(o_ref.dtype)


def _na_bias_table(rel_bias, rows, rq, rk):
    h, nr2, nc2 = rel_bias.shape
    na_rows, na_cols = (nr2 + 1) // 2, (nc2 + 1) // 2
    kr = min(na_rows, rows)
    w = _GRID_W
    nblk = rows // rq
    c = np.arange(w)
    c0 = np.clip(c - na_cols // 2, 0, w - na_cols)
    col_ok = (c[None, :] >= c0[:, None]) & (c[None, :] < c0[:, None] + na_cols)
    rbp = jnp.pad(rel_bias.astype(_F32) * _LOG2E, ((0, 0), (0, 0), (w, w)))
    tcol = jnp.stack([rbp[:, :, na_cols - 1 - qc + w:na_cols - 1 - qc + 2 * w] for qc in range(w)], axis=2)
    tcol = jnp.where(jnp.asarray(col_ok)[None, None], tcol, _NEG)
    tcol = jnp.concatenate([tcol, jnp.full((h, 1, w, w), _NEG, _F32)], axis=1)
    sel = np.zeros((3, rq, rk, nr2 + 1), np.float32)
    for ti, jb in enumerate((0, min(1, nblk - 1), nblk - 1)):
        kb0 = int(np.clip(jb * rq - (rk - rq) // 2, 0, rows - rk))
        for qi in range(rq):
            r = jb * rq + qi
            r0 = int(np.clip(r - kr // 2, 0, rows - kr))
            for ki in range(rk):
                krow = kb0 + ki
                sel[ti, qi, ki, krow - r + na_rows - 1 if r0 <= krow < r0 + kr else nr2] = 1.0
    tab = jnp.einsum("tqkd,hdcx->thqckx", jnp.asarray(sel), tcol, precision=lax.Precision.HIGHEST)
    return tab.reshape(3, h, rq * w, rk * w)


def _na_attention(QKV, bias_tab, geo, n_heads):
    b, s, lc = geo["batch"], geo["seq"], geo["ctx_len"]
    t = QKV.shape[0]
    sub, kwin = bias_tab.shape[2], bias_tab.shape[3]
    nsub = 4
    bq = nsub * sub
    nj = s // bq
    cblk = geo["n_x_rows"] // lc
    hps = 2
    hw = hps * _LANES
    ng = n_heads // hps
    return pl.pallas_call(
        functools.partial(_na_body, s, sub, kwin, hps, nsub),
        grid=(b, ng, nj),
        in_specs=[
            pl.BlockSpec((bq, hw), lambda bi, h, j: (bi * nj + j, h)),
            pl.BlockSpec((s, hw), lambda bi, h, j: (bi, ng + h)),
            pl.BlockSpec((s, hw), lambda bi, h, j: (bi, 2 * ng + h)),
            pl.BlockSpec((lc, hw), lambda bi, h, j: (cblk + bi, ng + h)),
            pl.BlockSpec((lc, hw), lambda bi, h, j: (cblk + bi, 2 * ng + h)),
            pl.BlockSpec((1, hps, sub, kwin), lambda bi, h, j: (jnp.where(j == 0, 0, 1), h, 0, 0)),
            pl.BlockSpec((1, hps, sub, kwin), lambda bi, h, j: (1, h, 0, 0)),
            pl.BlockSpec((1, hps, sub, kwin), lambda bi, h, j: (jnp.where(j == nj - 1, 2, 1), h, 0, 0)),
        ],
        out_specs=pl.BlockSpec((bq, hw), lambda bi, h, j: (bi * nj + j, h)),
        out_shape=jax.ShapeDtypeStruct((t, n_heads * _LANES), _BF),
        compiler_params=_cp("arbitrary", "arbitrary", "arbitrary"),
        name="na_attn",
    )(QKV, QKV, QKV, QKV, QKV, bias_tab, bias_tab, bias_tab)


def _ctx_attn_body(group, has_sink, *refs):
    if has_sink:
        q_ref, k_ref, v_ref, sk_ref, _, o_ref = refs
    else:
        q_ref, k_ref, v_ref, _, o_ref = refs
    k = k_ref[...]
    vx = _with_ones(v_ref[...])
    for g in range(group):
        c = g * _LANES
        s = _dot_nt(q_ref[:, c:c + _LANES], k)
        m = jnp.max(s, axis=-1, keepdims=True)
        if has_sink:
            sk = sk_ref[0][:, c:c + 1]
            m = jnp.maximum(m, sk)
        acc = _dot(jnp.exp2(s - m).astype(_BF), vx)
        l = acc[:, _LANES:]
        if has_sink:
            l = l + jnp.exp2(sk - m)
        o_ref[:, c:c + _LANES] = (acc[:, :_LANES] / l).astype(o_ref.dtype)


def _ctx_attention(QKV, O, geo, n_heads, n_kv, sink=None):
    b, lc = geo["batch"], geo["ctx_len"]
    group = n_heads // n_kv
    gw = group * _LANES
    cblk = geo["n_x_rows"] // lc
    in_specs = [
        pl.BlockSpec((lc, gw), lambda bi, n: (cblk + bi, n)),
        pl.BlockSpec((lc, _LANES), lambda bi, n: (cblk + bi, n_heads + n)),
        pl.BlockSpec((lc, _LANES), lambda bi, n: (cblk + bi, n_heads + n_kv + n)),
    ]
    args = [QKV, QKV, QKV]
    if sink is not None:
        in_specs.append(pl.BlockSpec((1, 1, gw), lambda bi, n: (n, 0, 0)))
        args.append(sink)
    in_specs.append(pl.BlockSpec(memory_space=pl.ANY))
    args.append(O)
    return pl.pallas_call(
        functools.partial(_ctx_attn_body, group, sink is not None),
        grid=(b, n_kv),
        in_specs=in_specs,
        out_specs=pl.BlockSpec((lc, gw), lambda bi, n: (cblk + bi, n)),
        out_shape=jax.ShapeDtypeStruct(O.shape, O.dtype),
        input_output_aliases={len(args) - 1: 0},
        compiler_params=_cp("arbitrary", "arbitrary"),
        name="ctx_attn",
    )(*args)


def _swa_body(seq, bq, nblk, win, group, q_ref, k_ref, v_ref, kc_ref, vc_ref, sk_ref, o_ref):
    kw = bq + 2 * win
    kc = kc_ref[...]
    vcx = _with_ones(vc_ref[...])
    for bi in range(nblk):
        t = pl.program_id(2) * nblk + bi
        rows = slice(bi * bq, (bi + 1) * bq)
        ks = pl.multiple_of(jnp.clip(t * bq - win, 0, seq - kw), _LANES)
        k = k_ref[pl.ds(ks, kw), :]
        vx = _with_ones(v_ref[pl.ds(ks, kw), :])
        qpos = t * bq + lax.broadcasted_iota(jnp.int32, (bq, 1), 0)
        kpos = ks + lax.broadcasted_iota(jnp.int32, (1, kw), 1)
        band = jnp.abs(kpos - qpos) <= win
        for g in range(group):
            c = g * _LANES
            q = q_ref[rows, c:c + _LANES]
            s = jnp.where(band, _dot_nt(q, k), _NEG)
            sc = _dot_nt(q, kc)
            sk = sk_ref[0][:, c:c + 1]
            m = jnp.maximum(jnp.maximum(jnp.max(s, axis=-1, keepdims=True), jnp.max(sc, axis=-1, keepdims=True)), sk)
            p = jnp.exp2(s - m).astype(_BF)
            pc = jnp.exp2(sc - m).astype(_BF)
            acc = _dot(p, vx) + _dot(pc, vcx)
            l = acc[:, _LANES:] + jnp.exp2(sk - m)
            o_ref[rows, c:c + _LANES] = (acc[:, :_LANES] / l).astype(o_ref.dtype)


def _swa_attention(QKV, sink, geo, n_heads, n_kv, bq):
    b, s, lc = geo["batch"], geo["seq"], geo["ctx_len"]
    t = QKV.shape[0]
    group = n_heads // n_kv
    gw = group * _LANES
    nblk = 2
    bs = nblk * bq
    nq = s // bs
    cblk = geo["n_x_rows"] // lc
    return pl.pallas_call(
        functools.partial(_swa_body, s, bq, nblk, _SWA_WINDOW, group),
        grid=(b, n_kv, nq),
        in_specs=[
            pl.BlockSpec((bs, gw), lambda bi, n, j: (bi * nq + j, n)),
            pl.BlockSpec((s, _LANES), lambda bi, n, j: (bi, n_heads + n)),
            pl.BlockSpec((s, _LANES), lambda bi, n, j: (bi, n_heads + n_kv + n)),
            pl.BlockSpec((lc, _LANES), lambda bi, n, j: (cblk + bi, n_heads + n)),
            pl.BlockSpec((lc, _LANES), lambda bi, n, j: (cblk + bi, n_heads + n_kv + n)),
            pl.BlockSpec((1, 1, gw), lambda bi, n, j: (n, 0, 0)),
        ],
        out_specs=pl.BlockSpec((bs, gw), lambda bi, n, j: (bi * nq + j, n)),
        out_shape=jax.ShapeDtypeStruct((t, n_heads * _LANES), _BF),
        compiler_params=_cp("arbitrary", "arbitrary", "arbitrary"),
        name="swa_attn",
    )(QKV, QKV, QKV, QKV, QKV, sink)


def _gqa_body(seq, ck, group, q_ref, k_ref, v_ref, kc_ref, vc_ref, o_ref, m_ref, acc_ref):
    m_ref[...] = jnp.full(m_ref.shape, _NEG, _F32)
    acc_ref[...] = jnp.zeros(acc_ref.shape, _F32)

    def step(k, v):
        width = k.shape[0]
        vx = jnp.concatenate([v, jnp.ones((width, _LANES), v.dtype)], axis=1)
        for g in range(group):
            s = _dot_nt(q_ref[:, g * _LANES:(g + 1) * _LANES], k)
            m_old = m_ref[g]
            m_new = jnp.maximum(m_old, jnp.max(s, axis=-1, keepdims=True))
            alpha = jnp.exp2(m_old - m_new)
            p = jnp.exp2(s - _lane_tiles(m_new, width // _LANES)).astype(_BF)
            acc_ref[g] = _lane_tiles(alpha, 2) * acc_ref[g] + _dot(p, vx)
            m_ref[g] = m_new

    def chunk(c, carry):
        off = pl.multiple_of(c * ck, ck)
        step(k_ref[pl.ds(off, ck), :], v_ref[pl.ds(off, ck), :])
        return carry

    lax.fori_loop(0, seq // ck, chunk, 0, unroll=min(4, seq // ck))
    step(kc_ref[...], vc_ref[...])
    for g in range(group):
        acc = acc_ref[g]
        o_ref[:, g * _LANES:(g + 1) * _LANES] = (acc[:, :_LANES] / acc[:, _LANES:]).astype(o_ref.dtype)


def _gqa_attention(QKV, geo, n_heads, n_kv, bq, ck):
    b, s, lc = geo["batch"], geo["seq"], geo["ctx_len"]
    group = n_heads // n_kv
    gw = group * _LANES
    nq = s // bq
    cblk = geo["n_x_rows"] // lc
    return pl.pallas_call(
        functools.partial(_gqa_body, s, ck, group),
        grid=(b, n_kv, nq),
        in_specs=[
            pl.BlockSpec((bq, gw), lambda bi, n, j: (bi * nq + j, n)),
            pl.BlockSpec((s, _LANES), lambda bi, n, j: (bi, n_heads + n)),
            pl.BlockSpec((s, _LANES), lambda bi, n, j: (bi, n_heads + n_kv + n)),
            pl.BlockSpec((lc, _LANES), lambda bi, n, j: (cblk + bi, n_heads + n)),
            pl.BlockSpec((lc, _LANES), lambda bi, n, j: (cblk + bi, n_heads + n_kv + n)),
        ],
        out_specs=pl.BlockSpec((bq, gw), lambda bi, n, j: (bi * nq + j, n)),
        out_shape=jax.ShapeDtypeStruct((geo["n_x_rows"], n_heads * _LANES), _BF),
        scratch_shapes=[pltpu.VMEM((group, bq, _LANES), _F32), pltpu.VMEM((group, bq, 2 * _LANES), _F32)],
        compiler_params=_cp("arbitrary", "arbitrary", "arbitrary"),
        name="gqa_attn",
    )(QKV, QKV, QKV, QKV, QKV)


def _log_sigmoid(x):
    return jnp.minimum(x, 0.0) - jnp.log1p(jnp.exp(-jnp.abs(x)))


def _mlstm_chunk(d, st, n_heads, L, h, q_ref, k_ref, v_ref, gt_ref, gb_ref, o_ref, c_ref, n_ref, m_ref):
    gates = gt_ref[...] + gb_ref[...]
    lane = lax.broadcasted_iota(jnp.int32, (1, _LANES), 1)
    li_col = jnp.sum(jnp.where(lane == (2 * d) * n_heads + h, gates, 0.0), axis=-1, keepdims=True)
    lf_pre = jnp.sum(jnp.where(lane == (2 * d + 1) * n_heads + h, gates, 0.0), axis=-1, keepdims=True)
    lf_col = _log_sigmoid(lf_pre)
    ti = lax.broadcasted_iota(jnp.int32, (L, L), 0)
    si = lax.broadcasted_iota(jnp.int32, (L, L), 1)
    eye = ti == si
    li_row = jnp.sum(jnp.where(eye, li_col, 0.0), axis=0, keepdims=True)
    lf_row = jnp.sum(jnp.where(eye, lf_col, 0.0), axis=0, keepdims=True)
    allowed = (ti >= si) if d == 0 else (ti <= si)
    allowed_t = (si >= ti) if d == 0 else (si <= ti)
    b_col = jnp.sum(jnp.where(allowed, lf_row, 0.0), axis=-1, keepdims=True)
    b_row = jnp.sum(jnp.where(allowed_t, lf_col, 0.0), axis=0, keepdims=True)
    total = jnp.sum(lf_col, axis=0, keepdims=True)
    m_prev = m_ref[st, 0:1, 0:1]

    q = q_ref[...]
    k = k_ref[...]
    v = v_ref[...]
    dmat = jnp.where(allowed, b_col - b_row + li_row, _NEG)
    g_col = b_col + m_prev
    m_t = jnp.maximum(g_col, jnp.max(dmat, axis=-1, keepdims=True))
    w = jnp.exp(dmat - m_t) * _dot_nt(q, k)
    w_prev = jnp.exp(g_col - m_t)
    cmat = c_ref[st]
    nvec = n_ref[st, 0:1, :]
    num = _dot(w.astype(_BF), v) + w_prev * _dot(q, cmat.astype(_BF))
    qn = jnp.sum(q.astype(_F32) * nvec, axis=-1, keepdims=True)
    den = jnp.sum(w, axis=-1, keepdims=True) + w_prev * qn
    o_ref[...] = (num / jnp.maximum(jnp.abs(den), jnp.exp(-m_t))).astype(o_ref.dtype)

    lw = total - b_col + li_col
    m_new = jnp.maximum(total + m_prev, jnp.max(lw, axis=0, keepdims=True))
    decay = jnp.exp(total + m_prev - m_new)
    kw = k.astype(_F32) * jnp.exp(lw - m_new)
    c_ref[st] = decay * cmat + _dot_tn(kw.astype(_BF), v)
    n_ref[st] = jnp.broadcast_to(decay * nvec + jnp.sum(kw, axis=0, keepdims=True), n_ref.shape[1:])
    m_ref[st] = jnp.broadcast_to(m_new, m_ref.shape[1:])


def _mlstm_body(n_heads, hps, dqk, dv, chunk, qf_ref, kf_ref, vf_ref, gf_ref, qb_ref, kb_ref, vb_ref, gbk_ref,
                gb_ref, of_ref, ob_ref, c_ref, n_ref, m_ref):
    hg = pl.program_id(1)

    @pl.when(pl.program_id(2) == 0)
    def _():
        c_ref[...] = jnp.zeros(c_ref.shape, _F32)
        n_ref[...] = jnp.zeros(n_ref.shape, _F32)
        m_ref[...] = jnp.zeros(m_ref.shape, _F32)

    dirs = ((qf_ref, kf_ref, vf_ref, gf_ref, of_ref), (qb_ref, kb_ref, vb_ref, gbk_ref, ob_ref))
    for hh in range(hps):
        qs, vs = slice(hh * dqk, (hh + 1) * dqk), slice(hh * dv, (hh + 1) * dv)
        for d, (q_ref, k_ref, v_ref, g_ref, o_ref) in enumerate(dirs):
            _mlstm_chunk(d, d * hps + hh, n_heads, chunk, hg * hps + hh, q_ref.at[:, qs], k_ref.at[:, qs],
                         v_ref.at[:, vs], g_ref, gb_ref, o_ref.at[:, vs], c_ref, n_ref, m_ref)


def _mlstm_scan(P, G, gate_b, geo, n_heads, dqk, dv, chunk):
    b, s, lc = geo["batch"], geo["seq"], geo["ctx_len"]
    t = P.shape[0]
    nctx, nx = lc // chunk, s // chunk
    cbase = geo["n_x_rows"] // chunk

    def rblk(d, bi, c):
        cc = c - nctx
        in_ctx = cbase + bi * nctx + (c if d == 0 else nctx - 1 - c)
        in_x = bi * nx + (cc if d == 0 else nx - 1 - cc)
        return jnp.where(c < nctx, in_ctx, in_x)

    hps = 2
    ng = n_heads // hps
    k_off = ng
    v_off = 2 * n_heads * dqk // (hps * dv)
    in_specs, args = [], []
    for d in (0, 1):
        in_specs += [
            pl.BlockSpec((chunk, hps * dqk), lambda bi, h, c, d=d: (rblk(d, bi, c), h)),
            pl.BlockSpec((chunk, hps * dqk), lambda bi, h, c, d=d: (rblk(d, bi, c), k_off + h)),
            pl.BlockSpec((chunk, hps * dv), lambda bi, h, c, d=d: (rblk(d, bi, c), v_off + h)),
            pl.BlockSpec((chunk, _LANES), lambda bi, h, c, d=d: (rblk(d, bi, c), 0)),
        ]
        args += [P, P, P, G]
    in_specs.append(pl.BlockSpec((1, _LANES), lambda bi, h, c: (0, 0)))
    args.append(gate_b)
    out_sds = jax.ShapeDtypeStruct((t, n_heads * dv), _F32)
    ns = 2 * hps
    return pl.pallas_call(
        functools.partial(_mlstm_body, n_heads, hps, dqk, dv, chunk),
        grid=(b, ng, nctx + nx),
        in_specs=in_specs,
        out_specs=[pl.BlockSpec((chunk, hps * dv), lambda bi, h, c, d=d: (rblk(d, bi, c), h)) for d in (0, 1)],
        out_shape=[out_sds, out_sds],
        scratch_shapes=[pltpu.VMEM((ns, dqk, dv), _F32), pltpu.VMEM((ns, 8, dqk), _F32), pltpu.VMEM((ns, 8, _LANES), _F32)],
        compiler_params=_cp("arbitrary", "arbitrary", "arbitrary"),
        name="mlstm_scan",
    )(*args)


def _ml_out_body(n_heads, dv, hf_ref, hb_ref, og_ref, hg_ref, w_ref, x_ref, gt_ref, o_ref):
    hsum = hf_ref[...] + hb_ref[...]
    parts = []
    for hh in range(n_heads):
        a = hsum[:, hh * dv:(hh + 1) * dv]
        hn = a * lax.rsqrt(jnp.mean(a * a, axis=-1, keepdims=True) + _EPS) * hg_ref[:, hh * dv:(hh + 1) * dv]
        og = og_ref[:, hh * dv:(hh + 1) * dv].astype(_F32)
        parts.append((jax.nn.sigmoid(og) * hn).astype(_BF))
    a = jnp.concatenate(parts, axis=-1)
    o_ref[...] = x_ref[...] + gt_ref[0] * _dot(a, w_ref[...])


def _ml_out(Hf, Hb, P, head_g, W, X, gate, geo, n_heads, dv, bm):
    t, d = X.shape
    kd = n_heads * dv
    o_off = (P.shape[1] - kd) // kd
    mrow = functools.partial(geo["mrow_bm"], bm)
    return pl.pallas_call(
        functools.partial(_ml_out_body, n_heads, dv),
        grid=(t // bm,),
        in_specs=[
            pl.BlockSpec((bm, kd), lambda i: (i, 0)),
            pl.BlockSpec((bm, kd), lambda i: (i, 0)),
            pl.BlockSpec((bm, kd), lambda i: (i, o_off)),
            pl.BlockSpec((1, kd), lambda i: (0, 0)),
            pl.BlockSpec((kd, d), lambda i: (0, 0)),
            pl.BlockSpec((bm, d), lambda i: (i, 0)),
            pl.BlockSpec((1, 1, d), lambda i: (mrow(i), 0, 0)),
        ],
        out_specs=pl.BlockSpec((bm, d), lambda i: (i, 0)),
        out_shape=jax.ShapeDtypeStruct((t, d), _F32),
        compiler_params=_cp("arbitrary"),
        name="ml_out",
    )(Hf, Hb, P, head_g, W, X, gate)


def _rope_tables(seq, pad_rows):
    tpos = np.arange(seq)
    row = (tpos // _GRID_W).astype(np.float32)
    col = (tpos % _GRID_W).astype(np.float32)
    nf = _LANES // 4
    inv = (_ROPE_BASE ** (-jnp.arange(nf, dtype=_F32) / nf))
    ar = jnp.asarray(row)[:, None] * inv
    ac = jnp.asarray(col)[:, None] * inv
    cr, sr, cc, sc = jnp.cos(ar), jnp.sin(ar), jnp.cos(ac), jnp.sin(ac)
    cs = jnp.concatenate([cr, cc, cr, cc], axis=1)
    sn = jnp.concatenate([-sr, -sc, sr, sc], axis=1)
    cs = jnp.concatenate([cs, jnp.ones((pad_rows, _LANES), _F32)], axis=0)
    sn = jnp.concatenate([sn, jnp.zeros((pad_rows, _LANES), _F32)], axis=0)
    return cs, sn


def _pair_major(v):
    q = _LANES // 4
    return v.reshape(v.shape[:-1] + (v.shape[-1] // _LANES, 2, 2, q)).swapaxes(-3, -2).reshape(v.shape)


def _rope_operands(w_qkv, q_gain, k_gain, n_qk_heads, cs, sn):
    nqk = n_qk_heads * _LANES
    w = jnp.concatenate([_pair_major(w_qkv[:, :nqk]), w_qkv[:, nqk:]], axis=1).astype(_BF)
    tabs = []
    for gain in (q_gain, k_gain):
        gp = _pair_major(gain.astype(_F32))
        tabs += [cs * gp[None, :], sn * jnp.roll(gp, _LANES // 2)[None, :]]
    return w, tuple(tabs)


def _geometry(batch, seq, ctx_len):
    bm = min(1024, batch * ctx_len)
    assert seq % bm == 0 and (batch * ctx_len) % bm == 0
    assert seq & (seq - 1) == 0 and ctx_len & (ctx_len - 1) == 0
    n_x_rows = batch * seq

    def mrow_bm(bm_, i):
        return jnp.minimum((i * bm_) // seq, batch)

    return dict(batch=batch, seq=seq, ctx_len=ctx_len, bm=bm, n_x_rows=n_x_rows,
                n_x_tiles=n_x_rows // bm, tiles_per_seq=seq // bm,
                mrow=functools.partial(mrow_bm, bm), mrow_bm=mrow_bm)


def kernel(x, c, ctx, c_ctx, ada_w, ada_b, norm1_g, norm2_g, ffn_w_in, ffn_conv_w, ffn_conv_b, ffn_w_out,
           na_w_qkv, na_q_g, na_k_g, na_rel_bias, na_w_o,
           swa_w_qkv, swa_q_g, swa_k_g, swa_sinks, swa_w_o,
           ml_w_in, ml_gate_b, ml_head_g, ml_w_o,
           gqa_w_qkv, gqa_q_g, gqa_k_g, gqa_w_o):
    batch, seq, d = x.shape
    ctx_len = ctx.shape[1]
    depth = ada_w.shape[0]
    n_heads = d // _LANES
    geo = _geometry(batch, seq, ctx_len)
    bm = geo["bm"]
    n_x_rows = geo["n_x_rows"]
    n_x_tiles = geo["n_x_tiles"]
    qscale = (_LANES ** -0.5) * _LOG2E

    X = (x.reshape(n_x_rows, d), ctx.reshape(batch * ctx_len, d))
    t_all = n_x_rows + batch * ctx_len
    nrt_all = t_all // bm

    cond = jnp.concatenate([c, c_ctx[None, :], jnp.zeros((8 - batch - 1, d), _F32)], axis=0)
    mods = _adaln(cond, ada_w, ada_b)

    cs, sn = _rope_tables(seq, bm)
    f = ffn_conv_w.shape[2]
    w_in_b = ffn_w_in.astype(_BF)
    w_out_b = ffn_w_out.astype(_BF)
    conv_b3 = ffn_conv_b.reshape(depth, 1, f)
    bf = 512
    bm_o = min(512, bm)

    for i in range(depth):
        kind, jl = i % 4, i // 4
        need_ctx = i < depth - 1
        mod = [mods[i, :batch + 1, k * d:(k + 1) * d].reshape(batch + 1, 1, d) for k in range(6)]
        sh1, sc1, g1, sh2, sc2, g2 = mod
        n1 = norm1_g[i].reshape(1, d)
        n2 = norm2_g[i].reshape(1, d)
        nrt_o = t_all // bm_o if need_ctx else n_x_rows // bm_o

        if kind == 0:
            w = na_w_qkv[jl].astype(_BF)
            gq = (na_q_g[jl] * qscale).reshape(1, _LANES)
            gk = na_k_g[jl].reshape(1, _LANES)
            bn = 1024
            QKV = _proj(X, n1, sh1, sc1, w, _qkv_layouts(n_heads, n_heads, bn, False), geo, bn, _BF, gains=(gq, gk))
            rows = seq // _GRID_W
            tab = _na_bias_table(na_rel_bias[jl], rows, 4, 12)
            O = _na_attention(QKV, tab, geo, n_heads)
            if need_ctx:
                O = _ctx_attention(QKV, O, geo, n_heads, n_heads)
            X = _oproj(O, na_w_o[jl].astype(_BF), X, g1, geo, nrt_o, bm_o, d)
        elif kind == 1:
            n_kv = (swa_w_qkv.shape[2] // _LANES - n_heads) // 2
            w, tabs = _rope_operands(swa_w_qkv[jl], swa_q_g[jl] * qscale, swa_k_g[jl], n_heads + n_kv, cs, sn)
            bn = _wide_tile(w.shape[1])
            QKV = _proj(X, n1, sh1, sc1, w, _qkv_layouts(n_heads, n_kv, bn, True), geo, bn, _BF, rope=tabs)
            group = n_heads // n_kv
            sink = jnp.repeat(swa_sinks[jl].astype(_F32) * _LOG2E, _LANES).reshape(n_kv, 1, group * _LANES)
            O = _swa_attention(QKV, sink, geo, n_heads, n_kv, 256)
            if need_ctx:
                O = _ctx_attention(QKV, O, geo, n_heads, n_kv, sink)
            X = _oproj(O, swa_w_o[jl].astype(_BF), X, g1, geo, nrt_o, bm_o, d)
        elif kind == 2:
            mh = ml_gate_b.shape[1] // 4
            dv = d // mh
            dqk = dv // 2
            nmain = 2 * mh * dqk + 2 * mh * dv
            w_main = ml_w_in[jl][:, :nmain].astype(_BF)
            w_gate = jnp.pad(ml_w_in[jl][:, nmain:], ((0, 0), (0, _LANES - 4 * mh))).astype(_BF)
            bn = 1024
            segs_by_tile = []
            k0, k1 = mh * dqk, 2 * mh * dqk
            for jt in range(nmain // bn):
                lo, hi = jt * bn, (jt + 1) * bn
                cuts = sorted({lo, hi, min(max(k0, lo), hi), min(max(k1, lo), hi)})
                segs = tuple((a - lo, b_ - lo, "plain", 0, False, dqk ** -0.5 if k0 <= a < k1 else 1.0)
                             for a, b_ in zip(cuts[:-1], cuts[1:]))
                segs_by_tile.append((jt, jt + 1, segs))
            P = _proj(X, n1, sh1, sc1, w_main, tuple(segs_by_tile), geo, bn, _BF)
            G = _proj(X, n1, sh1, sc1, w_gate, ((0, 1, ((0, _LANES, "plain", 0, False, 1.0),)),), geo, _LANES, _F32)
            gb = jnp.pad(ml_gate_b[jl].astype(_F32), (0, _LANES - 4 * mh)).reshape(1, _LANES)
            Hf, Hb = _mlstm_scan(P, G, gb, geo, mh, dqk, dv, 256)
            X = _ml_out(Hf, Hb, P, ml_head_g[jl].reshape(1, mh * dv), ml_w_o[jl].astype(_BF), X, g1, geo, mh, dv, bm_o)
        else:
            n_kv = (gqa_w_qkv.shape[2] // _LANES - n_heads) // 2
            w, tabs = _rope_operands(gqa_w_qkv[jl], gqa_q_g[jl] * qscale, gqa_k_g[jl], n_heads + n_kv, cs, sn)
            bn = _wide_tile(w.shape[1])
            QKV = _proj(X, n1, sh1, sc1, w, _qkv_layouts(n_heads, n_kv, bn, True), geo, bn, _BF, rope=tabs)
            O = _gqa_attention(QKV, geo, n_heads, n_kv, min(1024, seq), 512)
            if need_ctx:
                Oc = jnp.zeros((t_all, d), _BF).at[:n_x_rows].set(O)
                O = _ctx_attention(QKV, Oc, geo, n_heads, n_kv)
            X = _oproj(O, gqa_w_o[jl].astype(_BF), X, g1, geo, nrt_o, bm_o, d)

        nrt_f = nrt_all if need_ctx else n_x_tiles
        A = _ffn1(X, n2, sh2, sc2, w_in_b, ffn_conv_w, conv_b3, i, geo, nrt_f, bf)
        X = _oproj(A, w_out_b, X, g2, geo, A.shape[0] // bm, bm, 512, layer=i)

    return X[:n_x_rows].reshape(batch, seq, d)
```

```python
import functools
import math

import numpy as np
import jax
import jax.numpy as jnp
from jax import lax
from jax.experimental import pallas as pl
from jax.experimental.pallas import tpu as pltpu

_F32 = jnp.float32
_BF = jnp.bfloat16
_EPS = 1e-6
_NEG = -1e30
_LOG2E = 1.4426950408889634
_GRID_W = 64
_SWA_WINDOW = 128
_ROPE_BASE = 10000.0
_LANES = 128
_HALO = 8
_VMEM_LIMIT = 56 << 20


def _cp(*sem):
    return pltpu.CompilerParams(dimension_semantics=sem, vmem_limit_bytes=_VMEM_LIMIT)


def _dot(a, b):
    return jnp.dot(a, b, preferred_element_type=_F32)


def _dot_nt(a, b):
    return lax.dot_general(a, b, (((1,), (1,)), ((), ())), preferred_element_type=_F32)


def _dot_tn(a, b):
    return lax.dot_general(a, b, (((0,), (0,)), ((), ())), preferred_element_type=_F32)


def _norm_mod(x, g, sh, sc):
    ms = jnp.mean(x * x, axis=-1, keepdims=True)
    return (x * lax.rsqrt(ms + _EPS)) * (g * (1.0 + sc)) + sh


def _adaln_body(c_ref, w_ref, b_ref, o_ref):
    c = c_ref[...]
    s = (c * jax.nn.sigmoid(c)).astype(_BF)
    o_ref[0] = _dot(s, w_ref[0].astype(_BF)) + b_ref[0]


def _adaln(cond, ada_w, ada_b):
    depth, d, n = ada_w.shape
    bn = 1024
    return pl.pallas_call(
        _adaln_body,
        grid=(depth, n // bn),
        in_specs=[
            pl.BlockSpec((8, d), lambda l, j: (0, 0)),
            pl.BlockSpec((1, d, bn), lambda l, j: (l, 0, j)),
            pl.BlockSpec((1, 1, bn), lambda l, j: (l, 0, j)),
        ],
        out_specs=pl.BlockSpec((1, 8, bn), lambda l, j: (l, 0, j)),
        out_shape=jax.ShapeDtypeStruct((depth, 8, n), _F32),
        compiler_params=_cp("arbitrary", "arbitrary"),
        name="adaln",
    )(cond, ada_w, ada_b.reshape(depth, 1, n))


def _next_rows_chunks(bm, nj):
    nc = 1
    while nc * 2 <= nj and bm // (nc * 2) >= _LANES:
        nc *= 2
    return nc, bm // nc


def _proj_body(layouts, has_rope, n_gain, n_lat_tiles, *refs):
    x_refs = refs[:1] if n_lat_tiles is None else refs[:2]
    refs = refs[len(x_refs):]
    g_ref, sh_ref, sc_ref, w_ref = refs[:4]
    pos = 4
    tab_refs = ()
    if has_rope:
        tab_refs = refs[pos:pos + 4]
        pos += 4
    gain_refs = refs[pos:pos + n_gain]
    pos += n_gain
    o_ref, h_ref = refs[pos], refs[pos + 1]
    j = pl.program_id(1)

    is_lat = True if n_lat_tiles is None else pl.program_id(0) < n_lat_tiles
    for src_ref, mine in zip(x_refs, (is_lat, jnp.logical_not(is_lat))):
        @pl.when((j == 0) & mine)
        def _(src_ref=src_ref):
            h_ref[...] = _norm_mod(src_ref[...], g_ref[...], sh_ref[0], sc_ref[0]).astype(_BF)

    for lo, hi, segs in layouts:
        @pl.when((j >= lo) & (j < hi))
        def _(segs=segs):
            acc = _dot(h_ref[...], w_ref[...])
            for c0, c1, kind, gi, rope, mult in segs:
                if kind == "plain":
                    a = acc[:, c0:c1]
                    if mult != 1.0:
                        a = a * mult
                    o_ref[:, c0:c1] = a.astype(o_ref.dtype)
                    continue
                for c in range(c0, c1, _LANES):
                    a = acc[:, c:c + _LANES]
                    inv = lax.rsqrt(jnp.mean(a * a, axis=-1, keepdims=True) + _EPS)
                    if rope:
                        y = a * tab_refs[2 * gi][...] + pltpu.roll(a, _LANES // 2, 1) * tab_refs[2 * gi + 1][...]
                    else:
                        y = a * gain_refs[gi][...]
                    o_ref[:, c:c + _LANES] = (y * inv).astype(o_ref.dtype)


def _proj(X, g, sh, sc, W, layouts, geo, bn, out_dtype, rope=None, gains=()):
    n = W.shape[1]
    bm = geo["bm"]
    nj = n // bn
    mrow = geo["mrow"]
    if isinstance(X, tuple):
        nlt = X[0].shape[0] // bm
        t, d = X[0].shape[0] + X[1].shape[0], X[0].shape[1]
        x_specs = [pl.BlockSpec((bm, d), lambda i, j: (jnp.minimum(i, nlt - 1), 0)),
                   pl.BlockSpec((bm, d), lambda i, j: (jnp.maximum(i - nlt, 0), 0),
                                pipeline_mode=pl.Buffered(1))]
        x_args = list(X)
    else:
        nlt = None
        t, d = X.shape
        x_specs = [pl.BlockSpec((bm, d), lambda i, j: (i, 0))]
        x_args = [X]
    nrt = t // bm
    in_specs = x_specs + [
        pl.BlockSpec((1, d), lambda i, j: (0, 0)),
        pl.BlockSpec((1, 1, d), lambda i, j: (mrow(i), 0, 0)),
        pl.BlockSpec((1, 1, d), lambda i, j: (mrow(i), 0, 0)),
        pl.BlockSpec((d, bn), lambda i, j: (0, j)),
    ]
    args = x_args + [g, sh, sc, W]
    if rope is not None:
        tps, nxt = geo["tiles_per_seq"], geo["n_x_tiles"]
        tab = lambda i, j: (jnp.where(i < nxt, i % tps, tps), 0)
        in_specs += [pl.BlockSpec((bm, _LANES), tab)] * 4
        args += list(rope)
    for gn in gains:
        in_specs.append(pl.BlockSpec((1, _LANES), lambda i, j: (0, 0)))
        args.append(gn)
    return pl.pallas_call(
        functools.partial(_proj_body, layouts, rope is not None, len(gains), nlt),
        grid=(nrt, nj),
        in_specs=in_specs,
        out_specs=pl.BlockSpec((bm, bn), lambda i, j: (i, j)),
        out_shape=jax.ShapeDtypeStruct((t, n), out_dtype),
        scratch_shapes=[pltpu.VMEM((bm, d), _BF)],
        compiler_params=_cp("arbitrary", "arbitrary"),
        name="proj",
    )(*args)


def _wide_tile(n, cap=1280):
    return max(b for b in range(_LANES, cap + 1, _LANES) if n % b == 0)


def _qkv_layouts(n_q, n_kv, bn, rope):
    hd = _LANES
    bounds = [(0, n_q * hd, "q"), (n_q * hd, (n_q + n_kv) * hd, "k"), ((n_q + n_kv) * hd, (n_q + 2 * n_kv) * hd, "v")]
    n = (n_q + 2 * n_kv) * hd
    per_tile = []
    for jt in range(n // bn):
        lo, hi = jt * bn, (jt + 1) * bn
        segs = []
        for b0, b1, nm in bounds:
            s0, s1 = max(lo, b0), min(hi, b1)
            if s0 < s1:
                if nm == "v":
                    segs.append((s0 - lo, s1 - lo, "plain", 0, False, 1.0))
                else:
                    segs.append((s0 - lo, s1 - lo, "head", 0 if nm == "q" else 1, rope, 1.0))
        per_tile.append(tuple(segs))
    layouts = []
    for jt, segs in enumerate(per_tile):
        if layouts and layouts[-1][2] == segs and layouts[-1][1] == jt:
            layouts[-1] = (layouts[-1][0], jt + 1, segs)
        else:
            layouts.append((jt, jt + 1, segs))
    return tuple(layouts)


def _oproj_body(n_lat_tiles, a_ref, w_ref, *refs):
    x_refs, (gt_ref, o_ref) = refs[:-2], refs[-2:]
    x = x_refs[0][...]
    if n_lat_tiles is not None:
        x = jnp.where(pl.program_id(0) < n_lat_tiles, x, x_refs[1][...])
    o_ref[...] = x + gt_ref[0] * _dot(a_ref[...], w_ref[...])


def _oproj(A, W, X, gate, geo, nrt, bm, bn, layer=None):
    k = A.shape[1]
    d = W.shape[-1]
    mrow = functools.partial(geo["mrow_bm"], bm)
    if layer is None:
        w_spec = pl.BlockSpec((k, bn), lambda i, j: (0, j))
    else:
        w_spec = pl.BlockSpec((None, k, bn), lambda i, j: (layer, 0, j))
    if isinstance(X, tuple):
        nlt = X[0].shape[0] // bm
        x_specs = [pl.BlockSpec((bm, bn), lambda i, j: (jnp.minimum(i, nlt - 1), j)),
                   pl.BlockSpec((bm, bn), lambda i, j: (jnp.maximum(i - nlt, 0), j))]
        x_args = list(X)
    else:
        nlt = None
        x_specs = [pl.BlockSpec((bm, bn), lambda i, j: (i, j))]
        x_args = [X]
    return pl.pallas_call(
        functools.partial(_oproj_body, nlt),
        grid=(nrt, d // bn),
        in_specs=[pl.BlockSpec((bm, k), lambda i, j: (i, 0)), w_spec] + x_specs
        + [pl.BlockSpec((1, 1, bn), lambda i, j: (mrow(i), 0, j))],
        out_specs=pl.BlockSpec((bm, bn), lambda i, j: (i, j)),
        out_shape=jax.ShapeDtypeStruct((nrt * bm, d), _F32),
        compiler_params=_cp("arbitrary", "arbitrary"),
        name="oproj",
    )(A, W, *x_args, gate)


def _ffn1_body(bm, n_x_rows, seq, ctx_len, nc, rc, x0_ref, x0p_ref, x0n_ref, xn_ref, xnp_ref, xnn_ref,
               g_ref, sh_ref, sc_ref, shn_ref, scn_ref, wg_ref, wu_ref, cw_ref, cb_ref, o_ref, ha_ref, hb_ref):
    i = pl.program_id(0)
    j = pl.program_id(1)
    h_refs = (ha_ref, hb_ref)

    @pl.when((i == 0) & (j == 0))
    def _():
        g, sh, sc = g_ref[...], sh_ref[0], sc_ref[0]
        ha_ref[0:bm, :] = _norm_mod(x0_ref[...], g, sh, sc).astype(_BF)
        halo = jnp.concatenate([x0p_ref[...], x0n_ref[...]], axis=0)
        ha_ref[bm:bm + 2 * _HALO, :] = _norm_mod(halo, g, sh, sc).astype(_BF)

    for par in (0, 1):
        @pl.when(i % 2 == par)
        def _(par=par):
            h_ref, hn_ref = h_refs[par], h_refs[1 - par]
            gx = _dot(h_ref[...], wg_ref[...])
            u = _dot(h_ref[0:bm, :], wu_ref[...])
            g, shn, scn = g_ref[...], shn_ref[0], scn_ref[0]
            r0 = pl.multiple_of(jnp.minimum(j, nc - 1) * rc, rc)
            hn_ref[pl.ds(r0, rc), :] = _norm_mod(xn_ref[...], g, shn, scn).astype(_BF)
            halo = jnp.concatenate([xnp_ref[...], xnn_ref[...]], axis=0)
            hn_ref[bm:bm + 2 * _HALO, :] = _norm_mod(halo, g, shn, scn).astype(_BF)
            gm = gx[0:bm]
            g_prev = gx[bm + _HALO - 1:bm + _HALO]
            g_next = gx[bm + _HALO:bm + _HALO + 1]
            row = lax.broadcasted_iota(jnp.int32, (bm, 1), 0)
            tok = i * bm + row
            period = jnp.where(i * bm >= n_x_rows, ctx_len, seq)
            up = jnp.where(row == 0, g_prev, pltpu.roll(gm, 1, 0))
            dn = jnp.where(row == bm - 1, g_next, pltpu.roll(gm, bm - 1, 0))
            up = jnp.where((tok & (period - 1)) != 0, up, 0.0)
            dn = jnp.where(((tok + 1) & (period - 1)) != 0, dn, 0.0)
            cw = cw_ref[...]
            gc = cb_ref[...] + up * cw[0:1] + gm * cw[1:2] + dn * cw[2:3]
            o_ref[...] = (jax.nn.gelu(gc) * u).astype(o_ref.dtype)


def _ffn1(X, g, sh, sc, w_in, conv_w, conv_b, layer, geo, nrt, bf):
    t, d = X.shape
    f = conv_w.shape[2]
    bm = geo["bm"]
    mrow = geo["mrow"]
    hb = bm // _HALO
    last = t // _HALO - 1
    nf = f // bf
    nc, rc = _next_rows_chunks(bm, nf)
    nxt = lambda i: jnp.minimum(i + 1, nrt - 1)
    body = functools.partial(_ffn1_body, bm, geo["n_x_rows"], geo["seq"], geo["ctx_len"], nc, rc)
    return pl.pallas_call(
        body,
        grid=(nrt, nf),
        in_specs=[
            pl.BlockSpec((bm, d), lambda i, j: (0, 0)),
            pl.BlockSpec((_HALO, d), lambda i, j: (0, 0)),
            pl.BlockSpec((_HALO, d), lambda i, j: (jnp.minimum(hb, last), 0)),
            pl.BlockSpec((rc, d), lambda i, j: (nxt(i) * nc + jnp.minimum(j, nc - 1), 0)),
            pl.BlockSpec((_HALO, d), lambda i, j: (jnp.maximum(nxt(i) * hb - 1, 0), 0)),
            pl.BlockSpec((_HALO, d), lambda i, j: (jnp.minimum((nxt(i) + 1) * hb, last), 0)),
            pl.BlockSpec((1, d), lambda i, j: (0, 0)),
            pl.BlockSpec((1, 1, d), lambda i, j: (mrow(i), 0, 0)),
            pl.BlockSpec((1, 1, d), lambda i, j: (mrow(i), 0, 0)),
            pl.BlockSpec((1, 1, d), lambda i, j: (mrow(nxt(i)), 0, 0)),
            pl.BlockSpec((1, 1, d), lambda i, j: (mrow(nxt(i)), 0, 0)),
            pl.BlockSpec((None, d, bf), lambda i, j: (layer, 0, j)),
            pl.BlockSpec((None, d, bf), lambda i, j: (layer, 0, j + nf)),
            pl.BlockSpec((None, 3, bf), lambda i, j: (layer, 0, j)),
            pl.BlockSpec((None, 1, bf), lambda i, j: (layer, 0, j)),
        ],
        out_specs=pl.BlockSpec((bm, bf), lambda i, j: (i, j)),
        out_shape=jax.ShapeDtypeStruct((nrt * bm, f), _BF),
        scratch_shapes=[pltpu.VMEM((bm + 2 * _HALO, d), _BF), pltpu.VMEM((bm + 2 * _HALO, d), _BF)],
        compiler_params=_cp("arbitrary", "arbitrary"),
        name="ffn1",
    )(X, X, X, X, X, X, g, sh, sc, sh, sc, w_in, w_in, conv_w, conv_b)


def _with_ones(v):
    return jnp.concatenate([v, jnp.ones(v.shape, v.dtype)], axis=1)


def _lane_tiles(x, n):
    return jnp.concatenate([x] * n, axis=1) if n > 1 else x


def _na_body(seq, sub, kwin, hps, nsub, q_ref, k_ref, v_ref, kc_ref, vc_ref, ba_ref, bm_ref, bb_ref, o_ref):
    j = pl.program_id(2)
    b_refs = (ba_ref,) + (bm_ref,) * (nsub - 2) + (bb_ref,)
    for hh in range(hps):
        c = hh * _LANES
        kc = kc_ref[:, c:c + _LANES]
        vcx = _with_ones(vc_ref[:, c:c + _LANES])
        for i, b_ref in enumerate(b_refs):
            q0 = (nsub * j + i) * sub
            ks = pl.multiple_of(jnp.clip(q0 - (kwin - sub) // 2, 0, seq - kwin), 256)
            q = q_ref[i * sub:(i + 1) * sub, c:c + _LANES]
            k = k_ref[pl.ds(ks, kwin), c:c + _LANES]
            v = v_ref[pl.ds(ks, kwin), c:c + _LANES]
            s = _dot_nt(q, k) + b_ref[0, hh]
            sc = _dot_nt(q, kc)
            m = jnp.maximum(jnp.max(s, axis=-1, keepdims=True), jnp.max(sc, axis=-1, keepdims=True))
            p = jnp.exp2(s - m).astype(_BF)
            pc = jnp.exp2(sc - m).astype(_BF)
            acc = _dot(p, _with_ones(v)) + _dot(pc, vcx)
            o_ref[i * sub:(i + 1) * sub, c:c + _LANES] = (acc[:, :_LANES] / acc[:, _LANES:]).astype(o_ref.dtype)


def _na_bias_table(rel_bias, rows, rq, rk):
    h, nr2, nc2 = rel_bias.shape
    na_rows, na_cols = (nr2 + 1) // 2, (nc2 + 1) // 2
    kr = min(na_rows, rows)
    w = _GRID_W
    nblk = rows // rq
    c = np.arange(w)
    c0 = np.clip(c - na_cols // 2, 0, w - na_cols)
    col_ok = (c[None, :] >= c0[:, None]) & (c[None, :] < c0[:, None] + na_cols)
    rbp = jnp.pad(rel_bias.astype(_F32) * _LOG2E, ((0, 0), (0, 0), (w, w)))
    tcol = jnp.stack([rbp[:, :, na_cols - 1 - qc + w:na_cols - 1 - qc + 2 * w] for qc in range(w)], axis=2)
    tcol = jnp.where(jnp.asarray(col_ok)[None, None], tcol, _NEG)
    tcol = jnp.concatenate([tcol, jnp.full((h, 1, w, w), _NEG, _F32)], axis=1)
    sel = np.zeros((3, rq, rk, nr2 + 1), np.float32)
    for ti, jb in enumerate((0, min(1, nblk - 1), nblk - 1)):
        kb0 = int(np.clip(jb * rq - (rk - rq) // 2, 0, rows - rk))
        for qi in range(rq):
            r = jb * rq + qi
            r0 = int(np.clip(r - kr // 2, 0, rows - kr))
            for ki in range(rk):
                krow = kb0 + ki
                sel[ti, qi, ki, krow - r + na_rows - 1 if r0 <= krow < r0 + kr else nr2] = 1.0
    tab = jnp.einsum("tqkd,hdcx->thqckx", jnp.asarray(sel), tcol, precision=lax.Precision.HIGHEST)
    return tab.reshape(3, h, rq * w, rk * w)


def _na_attention(QKV, bias_tab, geo, n_heads):
    b, s, lc = geo["batch"], geo["seq"], geo["ctx_len"]
    t = QKV.shape[0]
    sub, kwin = bias_tab.shape[2], bias_tab.shape[3]
    nsub = min(8, s // sub)
    bq = nsub * sub
    nj = s // bq
    cblk = geo["n_x_rows"] // lc
    hps = 2
    hw = hps * _LANES
    ng = n_heads // hps
    return pl.pallas_call(
        functools.partial(_na_body, s, sub, kwin, hps, nsub),
        grid=(b, ng, nj),
        in_specs=[
            pl.BlockSpec((bq, hw), lambda bi, h, j: (bi * nj + j, h)),
            pl.BlockSpec((s, hw), lambda bi, h, j: (bi, ng + h)),
            pl.BlockSpec((s, hw), lambda bi, h, j: (bi, 2 * ng + h)),
            pl.BlockSpec((lc, hw), lambda bi, h, j: (cblk + bi, ng + h)),
            pl.BlockSpec((lc, hw), lambda bi, h, j: (cblk + bi, 2 * ng + h)),
            pl.BlockSpec((1, hps, sub, kwin), lambda bi, h, j: (jnp.where(j == 0, 0, 1), h, 0, 0)),
            pl.BlockSpec((1, hps, sub, kwin), lambda bi, h, j: (1, h, 0, 0)),
            pl.BlockSpec((1, hps, sub, kwin), lambda bi, h, j: (jnp.where(j == nj - 1, 2, 1), h, 0, 0)),
        ],
        out_specs=pl.BlockSpec((bq, hw), lambda bi, h, j: (bi * nj + j, h)),
        out_shape=jax.ShapeDtypeStruct((t, n_heads * _LANES), _BF),
        compiler_params=_cp("arbitrary", "arbitrary", "arbitrary"),
        name="na_attn",
    )(QKV, QKV, QKV, QKV, QKV, bias_tab, bias_tab, bias_tab)


def _ctx_attn_body(group, has_sink, *refs):
    if has_sink:
        q_ref, k_ref, v_ref, sk_ref, _, o_ref = refs
    else:
        q_ref, k_ref, v_ref, _, o_ref = refs
    k = k_ref[...]
    vx = _with_ones(v_ref[...])
    for g in range(group):
        c = g * _LANES
        s = _dot_nt(q_ref[:, c:c + _LANES], k)
        m = jnp.max(s, axis=-1, keepdims=True)
        if has_sink:
            sk = sk_ref[0][:, c:c + 1]
            m = jnp.maximum(m, sk)
        acc = _dot(jnp.exp2(s - m).astype(_BF), vx)
        l = acc[:, _LANES:]
        if has_sink:
            l = l + jnp.exp2(sk - m)
        o_ref[:, c:c + _LANES] = (acc[:, :_LANES] / l).astype(o_ref.dtype)


def _ctx_attention(QKV, O, geo, n_heads, n_kv, sink=None):
    b, lc = geo["batch"], geo["ctx_len"]
    group = n_heads // n_kv
    gw = group * _LANES
    cblk = geo["n_x_rows"] // lc
    in_specs = [
        pl.BlockSpec((lc, gw), lambda bi, n: (cblk + bi, n)),
        pl.BlockSpec((lc, _LANES), lambda bi, n: (cblk + bi, n_heads + n)),
        pl.BlockSpec((lc, _LANES), lambda bi, n: (cblk + bi, n_heads + n_kv + n)),
    ]
    args = [QKV, QKV, QKV]
    if sink is not None:
        in_specs.append(pl.BlockSpec((1, 1, gw), lambda bi, n: (n, 0, 0)))
        args.append(sink)
    in_specs.append(pl.BlockSpec(memory_space=pl.ANY))
    args.append(O)
    return pl.pallas_call(
        functools.partial(_ctx_attn_body, group, sink is not None),
        grid=(b, n_kv),
        in_specs=in_specs,
        out_specs=pl.BlockSpec((lc, gw), lambda bi, n: (cblk + bi, n)),
        out_shape=jax.ShapeDtypeStruct(O.shape, O.dtype),
        input_output_aliases={len(args) - 1: 0},
        compiler_params=_cp("arbitrary", "arbitrary"),
        name="ctx_attn",
    )(*args)


def _swa_body(seq, bq, nblk, win, group, q_ref, k_ref, v_ref, kc_ref, vc_ref, sk_ref, o_ref):
    kw = bq + 2 * win
    kc = kc_ref[...]
    vcx = _with_ones(vc_ref[...])
    for bi in range(nblk):
        t = pl.program_id(2) * nblk + bi
        rows = slice(bi * bq, (bi + 1) * bq)
        ks = pl.multiple_of(jnp.clip(t * bq - win, 0, seq - kw), _LANES)
        k = k_ref[pl.ds(ks, kw), :]
        vx = _with_ones(v_ref[pl.ds(ks, kw), :])
        qpos = t * bq + lax.broadcasted_iota(jnp.int32, (bq, 1), 0)
        kpos = ks + lax.broadcasted_iota(jnp.int32, (1, kw), 1)
        band = jnp.abs(kpos - qpos) <= win
        for g in range(group):
            c = g * _LANES
            q = q_ref[rows, c:c + _LANES]
            s = jnp.where(band, _dot_nt(q, k), _NEG)
            sc = _dot_nt(q, kc)
            sk = sk_ref[0][:, c:c + 1]
            m = jnp.maximum(jnp.maximum(jnp.max(s, axis=-1, keepdims=True), jnp.max(sc, axis=-1, keepdims=True)), sk)
            p = jnp.exp2(s - m).astype(_BF)
            pc = jnp.exp2(sc - m).astype(_BF)
            acc = _dot(p, vx) + _dot(pc, vcx)
            l = acc[:, _LANES:] + jnp.exp2(sk - m)
            o_ref[rows, c:c + _LANES] = (acc[:, :_LANES] / l).astype(o_ref.dtype)


def _swa_attention(QKV, sink, geo, n_heads, n_kv, bq):
    b, s, lc = geo["batch"], geo["seq"], geo["ctx_len"]
    t = QKV.shape[0]
    group = n_heads // n_kv
    gw = group * _LANES
    nblk = 2
    bs = nblk * bq
    nq = s // bs
    cblk = geo["n_x_rows"] // lc
    return pl.pallas_call(
        functools.partial(_swa_body, s, bq, nblk, _SWA_WINDOW, group),
        grid=(b, n_kv, nq),
        in_specs=[
            pl.BlockSpec((bs, gw), lambda bi, n, j: (bi * nq + j, n)),
            pl.BlockSpec((s, _LANES), lambda bi, n, j: (bi, n_heads + n)),
            pl.BlockSpec((s, _LANES), lambda bi, n, j: (bi, n_heads + n_kv + n)),
            pl.BlockSpec((lc, _LANES), lambda bi, n, j: (cblk + bi, n_heads + n)),
            pl.BlockSpec((lc, _LANES), lambda bi, n, j: (cblk + bi, n_heads + n_kv + n)),
            pl.BlockSpec((1, 1, gw), lambda bi, n, j: (n, 0, 0)),
        ],
        out_specs=pl.BlockSpec((bs, gw), lambda bi, n, j: (bi * nq + j, n)),
        out_shape=jax.ShapeDtypeStruct((t, n_heads * _LANES), _BF),
        compiler_params=_cp("arbitrary", "arbitrary", "arbitrary"),
        name="swa_attn",
    )(QKV, QKV, QKV, QKV, QKV, sink)


def _gqa_body(seq, ck, group, q_ref, k_ref, v_ref, kc_ref, vc_ref, o_ref, m_ref, acc_ref):
    m_ref[...] = jnp.full(m_ref.shape, _NEG, _F32)
    acc_ref[...] = jnp.zeros(acc_ref.shape, _F32)

    def step(k, v):
        width = k.shape[0]
        vx = jnp.concatenate([v, jnp.ones((width, _LANES), v.dtype)], axis=1)
        for g in range(group):
            s = _dot_nt(q_ref[:, g * _LANES:(g + 1) * _LANES], k)
            m_old = m_ref[g]
            m_new = jnp.maximum(m_old, jnp.max(s, axis=-1, keepdims=True))
            alpha = jnp.exp2(m_old - m_new)
            p = jnp.exp2(s - _lane_tiles(m_new, width // _LANES)).astype(_BF)
            acc_ref[g] = _lane_tiles(alpha, 2) * acc_ref[g] + _dot(p, vx)
            m_ref[g] = m_new

    def chunk(c, carry):
        off = pl.multiple_of(c * ck, ck)
        step(k_ref[pl.ds(off, ck), :], v_ref[pl.ds(off, ck), :])
        return carry

    lax.fori_loop(0, seq // ck, chunk, 0, unroll=min(8, seq // ck))
    step(kc_ref[...], vc_ref[...])
    for g in range(group):
        acc = acc_ref[g]
        o_ref[:, g * _LANES:(g + 1) * _LANES] = (acc[:, :_LANES] / acc[:, _LANES:]).astype(o_ref.dtype)


def _gqa_attention(QKV, geo, n_heads, n_kv, bq, ck):
    b, s, lc = geo["batch"], geo["seq"], geo["ctx_len"]
    group = n_heads // n_kv
    gw = group * _LANES
    nq = s // bq
    cblk = geo["n_x_rows"] // lc
    return pl.pallas_call(
        functools.partial(_gqa_body, s, ck, group),
        grid=(b, n_kv, nq),
        in_specs=[
            pl.BlockSpec((bq, gw), lambda bi, n, j: (bi * nq + j, n)),
            pl.BlockSpec((s, _LANES), lambda bi, n, j: (bi, n_heads + n)),
            pl.BlockSpec((s, _LANES), lambda bi, n, j: (bi, n_heads + n_kv + n)),
            pl.BlockSpec((lc, _LANES), lambda bi, n, j: (cblk + bi, n_heads + n)),
            pl.BlockSpec((lc, _LANES), lambda bi, n, j: (cblk + bi, n_heads + n_kv + n)),
        ],
        out_specs=pl.BlockSpec((bq, gw), lambda bi, n, j: (bi * nq + j, n)),
        out_shape=jax.ShapeDtypeStruct((geo["n_x_rows"], n_heads * _LANES), _BF),
        scratch_shapes=[pltpu.VMEM((group, bq, _LANES), _F32), pltpu.VMEM((group, bq, 2 * _LANES), _F32)],
        compiler_params=_cp("arbitrary", "arbitrary", "arbitrary"),
        name="gqa_attn",
    )(QKV, QKV, QKV, QKV, QKV)


def _log_sigmoid(x):
    return jnp.minimum(x, 0.0) - jnp.log1p(jnp.exp(-jnp.abs(x)))


def _mlstm_chunk(d, st, n_heads, L, h, q_ref, k_ref, v_ref, gt_ref, gb_ref, o_ref, c_ref, n_ref, m_ref):
    gates = gt_ref[...] + gb_ref[...]
    lane = lax.broadcasted_iota(jnp.int32, (1, _LANES), 1)
    li_col = jnp.sum(jnp.where(lane == (2 * d) * n_heads + h, gates, 0.0), axis=-1, keepdims=True)
    lf_pre = jnp.sum(jnp.where(lane == (2 * d + 1) * n_heads + h, gates, 0.0), axis=-1, keepdims=True)
    lf_col = _log_sigmoid(lf_pre)
    ti = lax.broadcasted_iota(jnp.int32, (L, L), 0)
    si = lax.broadcasted_iota(jnp.int32, (L, L), 1)
    eye = ti == si
    li_row = jnp.sum(jnp.where(eye, li_col, 0.0), axis=0, keepdims=True)
    lf_row = jnp.sum(jnp.where(eye, lf_col, 0.0), axis=0, keepdims=True)
    allowed = (ti >= si) if d == 0 else (ti <= si)
    allowed_t = (si >= ti) if d == 0 else (si <= ti)
    b_col = jnp.sum(jnp.where(allowed, lf_row, 0.0), axis=-1, keepdims=True)
    b_row = jnp.sum(jnp.where(allowed_t, lf_col, 0.0), axis=0, keepdims=True)
    total = jnp.sum(lf_col, axis=0, keepdims=True)
    m_prev = m_ref[st, 0:1, 0:1]

    q = q_ref[...]
    k = k_ref[...]
    v = v_ref[...]
    dmat = jnp.where(allowed, b_col - b_row + li_row, _NEG)
    g_col = b_col + m_prev
    m_t = jnp.maximum(g_col, jnp.max(dmat, axis=-1, keepdims=True))
    w = jnp.exp(dmat - m_t) * _dot_nt(q, k)
    w_prev = jnp.exp(g_col - m_t)
    cmat = c_ref[st]
    nvec = n_ref[st, 0:1, :]
    num = _dot(w.astype(_BF), v) + w_prev * _dot(q, cmat.astype(_BF))
    qn = jnp.sum(q.astype(_F32) * nvec, axis=-1, keepdims=True)
    den = jnp.sum(w, axis=-1, keepdims=True) + w_prev * qn
    o_ref[...] = (num / jnp.maximum(jnp.abs(den), jnp.exp(-m_t))).astype(o_ref.dtype)

    lw = total - b_col + li_col
    m_new = jnp.maximum(total + m_prev, jnp.max(lw, axis=0, keepdims=True))
    decay = jnp.exp(total + m_prev - m_new)
    kw = k.astype(_F32) * jnp.exp(lw - m_new)
    c_ref[st] = decay * cmat + _dot_tn(kw.astype(_BF), v)
    n_ref[st] = jnp.broadcast_to(decay * nvec + jnp.sum(kw, axis=0, keepdims=True), n_ref.shape[1:])
    m_ref[st] = jnp.broadcast_to(m_new, m_ref.shape[1:])


def _mlstm_body(n_heads, hps, dqk, dv, chunk, qf_ref, kf_ref, vf_ref, gf_ref, qb_ref, kb_ref, vb_ref, gbk_ref,
                gb_ref, of_ref, ob_ref, c_ref, n_ref, m_ref):
    hg = pl.program_id(1)

    @pl.when(pl.program_id(2) == 0)
    def _():
        c_ref[...] = jnp.zeros(c_ref.shape, _F32)
        n_ref[...] = jnp.zeros(n_ref.shape, _F32)
        m_ref[...] = jnp.zeros(m_ref.shape, _F32)

    dirs = ((qf_ref, kf_ref, vf_ref, gf_ref, of_ref), (qb_ref, kb_ref, vb_ref, gbk_ref, ob_ref))
    for hh in range(hps):
        qs, vs = slice(hh * dqk, (hh + 1) * dqk), slice(hh * dv, (hh + 1) * dv)
        for d, (q_ref, k_ref, v_ref, g_ref, o_ref) in enumerate(dirs):
            _mlstm_chunk(d, d * hps + hh, n_heads, chunk, hg * hps + hh, q_ref.at[:, qs], k_ref.at[:, qs],
                         v_ref.at[:, vs], g_ref, gb_ref, o_ref.at[:, vs], c_ref, n_ref, m_ref)


def _mlstm_scan(P, G, gate_b, geo, n_heads, dqk, dv, chunk):
    b, s, lc = geo["batch"], geo["seq"], geo["ctx_len"]
    t = P.shape[0]
    nctx, nx = lc // chunk, s // chunk
    cbase = geo["n_x_rows"] // chunk

    def rblk(d, bi, c):
        cc = c - nctx
        in_ctx = cbase + bi * nctx + (c if d == 0 else nctx - 1 - c)
        in_x = bi * nx + (cc if d == 0 else nx - 1 - cc)
        return jnp.where(c < nctx, in_ctx, in_x)

    hps = 4
    ng = n_heads // hps
    k_off = ng
    v_off = 2 * n_heads * dqk // (hps * dv)
    in_specs, args = [], []
    for d in (0, 1):
        in_specs += [
            pl.BlockSpec((chunk, hps * dqk), lambda bi, h, c, d=d: (rblk(d, bi, c), h)),
            pl.BlockSpec((chunk, hps * dqk), lambda bi, h, c, d=d: (rblk(d, bi, c), k_off + h)),
            pl.BlockSpec((chunk, hps * dv), lambda bi, h, c, d=d: (rblk(d, bi, c), v_off + h)),
            pl.BlockSpec((chunk, _LANES), lambda bi, h, c, d=d: (rblk(d, bi, c), 0)),
        ]
        args += [P, P, P, G]
    in_specs.append(pl.BlockSpec((1, _LANES), lambda bi, h, c: (0, 0)))
    args.append(gate_b)
    out_sds = jax.ShapeDtypeStruct((t, n_heads * dv), _F32)
    ns = 2 * hps
    return pl.pallas_call(
        functools.partial(_mlstm_body, n_heads, hps, dqk, dv, chunk),
        grid=(b, ng, nctx + nx),
        in_specs=in_specs,
        out_specs=[pl.BlockSpec((chunk, hps * dv), lambda bi, h, c, d=d: (rblk(d, bi, c), h)) for d in (0, 1)],
        out_shape=[out_sds, out_sds],
        scratch_shapes=[pltpu.VMEM((ns, dqk, dv), _F32), pltpu.VMEM((ns, 8, dqk), _F32), pltpu.VMEM((ns, 8, _LANES), _F32)],
        compiler_params=_cp("arbitrary", "arbitrary", "arbitrary"),
        name="mlstm_scan",
    )(*args)


def _ml_out_body(n_heads, dv, hf_ref, hb_ref, og_ref, hg_ref, w_ref, x_ref, gt_ref, o_ref):
    hsum = hf_ref[...] + hb_ref[...]
    parts = []
    for hh in range(n_heads):
        a = hsum[:, hh * dv:(hh + 1) * dv]
        hn = a * lax.rsqrt(jnp.mean(a * a, axis=-1, keepdims=True) + _EPS) * hg_ref[:, hh * dv:(hh + 1) * dv]
        og = og_ref[:, hh * dv:(hh + 1) * dv].astype(_F32)
        parts.append((jax.nn.sigmoid(og) * hn).astype(_BF))
    a = jnp.concatenate(parts, axis=-1)
    o_ref[...] = x_ref[...] + gt_ref[0] * _dot(a, w_ref[...])


def _ml_out(Hf, Hb, P, head_g, W, X, gate, geo, n_heads, dv, bm):
    t, d = X.shape
    kd = n_heads * dv
    o_off = (P.shape[1] - kd) // kd
    mrow = functools.partial(geo["mrow_bm"], bm)
    return pl.pallas_call(
        functools.partial(_ml_out_body, n_heads, dv),
        grid=(t // bm,),
        in_specs=[
            pl.BlockSpec((bm, kd), lambda i: (i, 0)),
            pl.BlockSpec((bm, kd), lambda i: (i, 0)),
            pl.BlockSpec((bm, kd), lambda i: (i, o_off)),
            pl.BlockSpec((1, kd), lambda i: (0, 0)),
            pl.BlockSpec((kd, d), lambda i: (0, 0)),
            pl.BlockSpec((bm, d), lambda i: (i, 0)),
            pl.BlockSpec((1, 1, d), lambda i: (mrow(i), 0, 0)),
        ],
        out_specs=pl.BlockSpec((bm, d), lambda i: (i, 0)),
        out_shape=jax.ShapeDtypeStruct((t, d), _F32),
        compiler_params=_cp("arbitrary"),
        name="ml_out",
    )(Hf, Hb, P, head_g, W, X, gate)


def _rope_tables(seq, pad_rows):
    tpos = np.arange(seq)
    row = (tpos // _GRID_W).astype(np.float32)
    col = (tpos % _GRID_W).astype(np.float32)
    nf = _LANES // 4
    inv = (_ROPE_BASE ** (-jnp.arange(nf, dtype=_F32) / nf))
    ar = jnp.asarray(row)[:, None] * inv
    ac = jnp.asarray(col)[:, None] * inv
    cr, sr, cc, sc = jnp.cos(ar), jnp.sin(ar), jnp.cos(ac), jnp.sin(ac)
    cs = jnp.concatenate([cr, cc, cr, cc], axis=1)
    sn = jnp.concatenate([-sr, -sc, sr, sc], axis=1)
    cs = jnp.concatenate([cs, jnp.ones((pad_rows, _LANES), _F32)], axis=0)
    sn = jnp.concatenate([sn, jnp.zeros((pad_rows, _LANES), _F32)], axis=0)
    return cs, sn


def _pair_major(v):
    q = _LANES // 4
    return v.reshape(v.shape[:-1] + (v.shape[-1] // _LANES, 2, 2, q)).swapaxes(-3, -2).reshape(v.shape)


def _rope_operands(w_qkv, q_gain, k_gain, n_qk_heads, cs, sn):
    nqk = n_qk_heads * _LANES
    w = jnp.concatenate([_pair_major(w_qkv[:, :nqk]), w_qkv[:, nqk:]], axis=1).astype(_BF)
    tabs = []
    for gain in (q_gain, k_gain):
        gp = _pair_major(gain.astype(_F32))
        tabs += [cs * gp[None, :], sn * jnp.roll(gp, _LANES // 2)[None, :]]
    return w, tuple(tabs)


def _geometry(batch, seq, ctx_len):
    bm = min(1024, batch * ctx_len)
    assert seq % bm == 0 and (batch * ctx_len) % bm == 0
    assert seq & (seq - 1) == 0 and ctx_len & (ctx_len - 1) == 0
    n_x_rows = batch * seq

    def mrow_bm(bm_, i):
        return jnp.minimum((i * bm_) // seq, batch)

    return dict(batch=batch, seq=seq, ctx_len=ctx_len, bm=bm, n_x_rows=n_x_rows,
                n_x_tiles=n_x_rows // bm, tiles_per_seq=seq // bm,
                mrow=functools.partial(mrow_bm, bm), mrow_bm=mrow_bm)


def kernel(x, c, ctx, c_ctx, ada_w, ada_b, norm1_g, norm2_g, ffn_w_in, ffn_conv_w, ffn_conv_b, ffn_w_out,
           na_w_qkv, na_q_g, na_k_g, na_rel_bias, na_w_o,
           swa_w_qkv, swa_q_g, swa_k_g, swa_sinks, swa_w_o,
           ml_w_in, ml_gate_b, ml_head_g, ml_w_o,
           gqa_w_qkv, gqa_q_g, gqa_k_g, gqa_w_o):
    batch, seq, d = x.shape
    ctx_len = ctx.shape[1]
    depth = ada_w.shape[0]
    n_heads = d // _LANES
    geo = _geometry(batch, seq, ctx_len)
    bm = geo["bm"]
    n_x_rows = geo["n_x_rows"]
    n_x_tiles = geo["n_x_tiles"]
    qscale = (_LANES ** -0.5) * _LOG2E

    X = (x.reshape(n_x_rows, d), ctx.reshape(batch * ctx_len, d))
    t_all = n_x_rows + batch * ctx_len
    nrt_all = t_all // bm

    cond = jnp.concatenate([c, c_ctx[None, :], jnp.zeros((8 - batch - 1, d), _F32)], axis=0)
    mods = _adaln(cond, ada_w, ada_b)

    cs, sn = _rope_tables(seq, bm)
    f = ffn_conv_w.shape[2]
    w_in_b = ffn_w_in.astype(_BF)
    w_out_b = ffn_w_out.astype(_BF)
    conv_b3 = ffn_conv_b.reshape(depth, 1, f)
    bf = 512
    bm_o = min(512, bm)

    for i in range(depth):
        kind, jl = i % 4, i // 4
        need_ctx = i < depth - 1
        mod = [mods[i, :batch + 1, k * d:(k + 1) * d].reshape(batch + 1, 1, d) for k in range(6)]
        sh1, sc1, g1, sh2, sc2, g2 = mod
        n1 = norm1_g[i].reshape(1, d)
        n2 = norm2_g[i].reshape(1, d)
        nrt_o = t_all // bm_o if need_ctx else n_x_rows // bm_o

        if kind == 0:
            w = na_w_qkv[jl].astype(_BF)
            gq = (na_q_g[jl] * qscale).reshape(1, _LANES)
            gk = na_k_g[jl].reshape(1, _LANES)
            bn = 1024
            QKV = _proj(X, n1, sh1, sc1, w, _qkv_layouts(n_heads, n_heads, bn, False), geo, bn, _BF, gains=(gq, gk))
            rows = seq // _GRID_W
            tab = _na_bias_table(na_rel_bias[jl], rows, 4, 12)
            O = _na_attention(QKV, tab, geo, n_heads)
            if need_ctx:
                O = _ctx_attention(QKV, O, geo, n_heads, n_heads)
            X = _oproj(O, na_w_o[jl].astype(_BF), X, g1, geo, nrt_o, bm_o, d)
        elif kind == 1:
            n_kv = (swa_w_qkv.shape[2] // _LANES - n_heads) // 2
            w, tabs = _rope_operands(swa_w_qkv[jl], swa_q_g[jl] * qscale, swa_k_g[jl], n_heads + n_kv, cs, sn)
            bn = _wide_tile(w.shape[1])
            QKV = _proj(X, n1, sh1, sc1, w, _qkv_layouts(n_heads, n_kv, bn, True), geo, bn, _BF, rope=tabs)
            group = n_heads // n_kv
            sink = jnp.repeat(swa_sinks[jl].astype(_F32) * _LOG2E, _LANES).reshape(n_kv, 1, group * _LANES)
            O = _swa_attention(QKV, sink, geo, n_heads, n_kv, 256)
            if need_ctx:
                O = _ctx_attention(QKV, O, geo, n_heads, n_kv, sink)
            X = _oproj(O, swa_w_o[jl].astype(_BF), X, g1, geo, nrt_o, bm_o, d)
        elif kind == 2:
            mh = ml_gate_b.shape[1] // 4
            dv = d // mh
            dqk = dv // 2
            nmain = 2 * mh * dqk + 2 * mh * dv
            w_main = ml_w_in[jl][:, :nmain].astype(_BF)
            w_gate = jnp.pad(ml_w_in[jl][:, nmain:], ((0, 0), (0, _LANES - 4 * mh))).astype(_BF)
            bn = 1024
            segs_by_tile = []
            k0, k1 = mh * dqk, 2 * mh * dqk
            for jt in range(nmain // bn):
                lo, hi = jt * bn, (jt + 1) * bn
                cuts = sorted({lo, hi, min(max(k0, lo), hi), min(max(k1, lo), hi)})
                segs = tuple((a - lo, b_ - lo, "plain", 0, False, dqk ** -0.5 if k0 <= a < k1 else 1.0)
                             for a, b_ in zip(cuts[:-1], cuts[1:]))
                segs_by_tile.append((jt, jt + 1, segs))
            P = _proj(X, n1, sh1, sc1, w_main, tuple(segs_by_tile), geo, bn, _BF)
            G = _proj(X, n1, sh1, sc1, w_gate, ((0, 1, ((0, _LANES, "plain", 0, False, 1.0),)),), geo, _LANES, _F32)
            gb = jnp.pad(ml_gate_b[jl].astype(_F32), (0, _LANES - 4 * mh)).reshape(1, _LANES)
            Hf, Hb = _mlstm_scan(P, G, gb, geo, mh, dqk, dv, 256)
            X = _ml_out(Hf, Hb, P, ml_head_g[jl].reshape(1, mh * dv), ml_w_o[jl].astype(_BF), X, g1, geo, mh, dv, bm_o)
        else:
            n_kv = (gqa_w_qkv.shape[2] // _LANES - n_heads) // 2
            w, tabs = _rope_operands(gqa_w_qkv[jl], gqa_q_g[jl] * qscale, gqa_k_g[jl], n_heads + n_kv, cs, sn)
            bn = _wide_tile(w.shape[1])
            QKV = _proj(X, n1, sh1, sc1, w, _qkv_layouts(n_heads, n_kv, bn, True), geo, bn, _BF, rope=tabs)
            O = _gqa_attention(QKV, geo, n_heads, n_kv, min(1024, seq), 512)
            if need_ctx:
                Oc = jnp.zeros((t_all, d), _BF).at[:n_x_rows].set(O)
                O = _ctx_attention(QKV, Oc, geo, n_heads, n_kv)
            X = _oproj(O, gqa_w_o[jl].astype(_BF), X, g1, geo, nrt_o, bm_o, d)

        nrt_f = nrt_all if need_ctx else n_x_tiles
        A = _ffn1(X, n2, sh2, sc2, w_in_b, ffn_conv_w, conv_b3, i, geo, nrt_f, bf)
        X = _oproj(A, w_out_b, X, g2, geo, A.shape[0] // bm, bm, 512, layer=i)

    return X[:n_x_rows].reshape(batch, seq, d)
```

```python
import functools
import math

import numpy as np
import jax
import jax.numpy as jnp
from jax import lax
from jax.experimental import pallas as pl
from jax.experimental.pallas import tpu as pltpu

_F32 = jnp.float32
_BF = jnp.bfloat16
_EPS = 1e-6
_NEG = -1e30
_LOG2E = 1.4426950408889634
_GRID_W = 64
_SWA_WINDOW = 128
_ROPE_BASE = 10000.0
_LANES = 128
_HALO = 8
_VMEM_LIMIT = 56 << 20


def _cp(*sem):
    return pltpu.CompilerParams(dimension_semantics=sem, vmem_limit_bytes=_VMEM_LIMIT)


def _dot(a, b):
    return jnp.dot(a, b, preferred_element_type=_F32)


def _dot_nt(a, b):
    return lax.dot_general(a, b, (((1,), (1,)), ((), ())), preferred_element_type=_F32)


def _dot_tn(a, b):
    return lax.dot_general(a, b, (((0,), (0,)), ((), ())), preferred_element_type=_F32)


def _norm_mod(x, g, sh, sc):
    ms = jnp.mean(x * x, axis=-1, keepdims=True)
    return (x * lax.rsqrt(ms + _EPS)) * (g * (1.0 + sc)) + sh


def _adaln_body(c_ref, w_ref, b_ref, o_ref):
    c = c_ref[...]
    s = (c * jax.nn.sigmoid(c)).astype(_BF)
    o_ref[0] = _dot(s, w_ref[0].astype(_BF)) + b_ref[0]


def _adaln(cond, ada_w, ada_b):
    depth, d, n = ada_w.shape
    bn = 1024
    return pl.pallas_call(
        _adaln_body,
        grid=(depth, n // bn),
        in_specs=[
            pl.BlockSpec((8, d), lambda l, j: (0, 0)),
            pl.BlockSpec((1, d, bn), lambda l, j: (l, 0, j)),
            pl.BlockSpec((1, 1, bn), lambda l, j: (l, 0, j)),
        ],
        out_specs=pl.BlockSpec((1, 8, bn), lambda l, j: (l, 0, j)),
        out_shape=jax.ShapeDtypeStruct((depth, 8, n), _F32),
        compiler_params=_cp("arbitrary", "arbitrary"),
        name="adaln",
    )(cond, ada_w, ada_b.reshape(depth, 1, n))


def _next_rows_chunks(bm, nj):
    nc = 1
    while nc * 2 <= nj and bm // (nc * 2) >= _LANES:
        nc *= 2
    return nc, bm // nc


def _proj_body(layouts, has_rope, n_gain, n_lat_tiles, *refs):
    x_refs = refs[:1] if n_lat_tiles is None else refs[:2]
    refs = refs[len(x_refs):]
    g_ref, sh_ref, sc_ref, w_ref = refs[:4]
    pos = 4
    tab_refs = ()
    if has_rope:
        tab_refs = refs[pos:pos + 4]
        pos += 4
    gain_refs = refs[pos:pos + n_gain]
    pos += n_gain
    o_ref, h_ref = refs[pos], refs[pos + 1]
    j = pl.program_id(1)

    is_lat = True if n_lat_tiles is None else pl.program_id(0) < n_lat_tiles
    for src_ref, mine in zip(x_refs, (is_lat, jnp.logical_not(is_lat))):
        @pl.when((j == 0) & mine)
        def _(src_ref=src_ref):
            h_ref[...] = _norm_mod(src_ref[...], g_ref[...], sh_ref[0], sc_ref[0]).astype(_BF)

    for lo, hi, segs in layouts:
        @pl.when((j >= lo) & (j < hi))
        def _(segs=segs):
            acc = _dot(h_ref[...], w_ref[...])
            for c0, c1, kind, gi, rope, mult in segs:
                if kind == "plain":
                    a = acc[:, c0:c1]
                    if mult != 1.0:
                        a = a * mult
                    o_ref[:, c0:c1] = a.astype(o_ref.dtype)
                    continue
                for c in range(c0, c1, _LANES):
                    a = acc[:, c:c + _LANES]
                    inv = lax.rsqrt(jnp.mean(a * a, axis=-1, keepdims=True) + _EPS)
                    if rope:
                        y = a * tab_refs[2 * gi][...] + pltpu.roll(a, _LANES // 2, 1) * tab_refs[2 * gi + 1][...]
                    else:
                        y = a * gain_refs[gi][...]
                    o_ref[:, c:c + _LANES] = (y * inv).astype(o_ref.dtype)


def _proj(X, g, sh, sc, W, layouts, geo, bn, out_dtype, rope=None, gains=()):
    n = W.shape[1]
    bm = geo["bm"]
    nj = n // bn
    mrow = geo["mrow"]
    if isinstance(X, tuple):
        nlt = X[0].shape[0] // bm
        t, d = X[0].shape[0] + X[1].shape[0], X[0].shape[1]
        x_specs = [pl.BlockSpec((bm, d), lambda i, j: (jnp.minimum(i, nlt - 1), 0)),
                   pl.BlockSpec((bm, d), lambda i, j: (jnp.maximum(i - nlt, 0), 0),
                                pipeline_mode=pl.Buffered(1))]
        x_args = list(X)
    else:
        nlt = None
        t, d = X.shape
        x_specs = [pl.BlockSpec((bm, d), lambda i, j: (i, 0))]
        x_args = [X]
    nrt = t // bm
    in_specs = x_specs + [
        pl.BlockSpec((1, d), lambda i, j: (0, 0)),
        pl.BlockSpec((1, 1, d), lambda i, j: (mrow(i), 0, 0)),
        pl.BlockSpec((1, 1, d), lambda i, j: (mrow(i), 0, 0)),
        pl.BlockSpec((d, bn), lambda i, j: (0, j)),
    ]
    args = x_args + [g, sh, sc, W]
    if rope is not None:
        tps, nxt = geo["tiles_per_seq"], geo["n_x_tiles"]
        tab = lambda i, j: (jnp.where(i < nxt, i % tps, tps), 0)
        in_specs += [pl.BlockSpec((bm, _LANES), tab)] * 4
        args += list(rope)
    for gn in gains:
        in_specs.append(pl.BlockSpec((1, _LANES), lambda i, j: (0, 0)))
        args.append(gn)
    return pl.pallas_call(
        functools.partial(_proj_body, layouts, rope is not None, len(gains), nlt),
        grid=(nrt, nj),
        in_specs=in_specs,
        out_specs=pl.BlockSpec((bm, bn), lambda i, j: (i, j)),
        out_shape=jax.ShapeDtypeStruct((t, n), out_dtype),
        scratch_shapes=[pltpu.VMEM((bm, d), _BF)],
        compiler_params=_cp("arbitrary", "arbitrary"),
        name="proj",
    )(*args)


def _wide_tile(n, cap=1280):
    return max(b for b in range(_LANES, cap + 1, _LANES) if n % b == 0)


def _qkv_layouts(n_q, n_kv, bn, rope):
    hd = _LANES
    bounds = [(0, n_q * hd, "q"), (n_q * hd, (n_q + n_kv) * hd, "k"), ((n_q + n_kv) * hd, (n_q + 2 * n_kv) * hd, "v")]
    n = (n_q + 2 * n_kv) * hd
    per_tile = []
    for jt in range(n // bn):
        lo, hi = jt * bn, (jt + 1) * bn
        segs = []
        for b0, b1, nm in bounds:
            s0, s1 = max(lo, b0), min(hi, b1)
            if s0 < s1:
                if nm == "v":
                    segs.append((s0 - lo, s1 - lo, "plain", 0, False, 1.0))
                else:
                    segs.append((s0 - lo, s1 - lo, "head", 0 if nm == "q" else 1, rope, 1.0))
        per_tile.append(tuple(segs))
    layouts = []
    for jt, segs in enumerate(per_tile):
        if layouts and layouts[-1][2] == segs and layouts[-1][1] == jt:
            layouts[-1] = (layouts[-1][0], jt + 1, segs)
        else:
            layouts.append((jt, jt + 1, segs))
    return tuple(layouts)


def _oproj_body(n_lat_tiles, a_ref, w_ref, *refs):
    x_refs, (gt_ref, o_ref) = refs[:-2], refs[-2:]
    x = x_refs[0][...]
    if n_lat_tiles is not None:
        x = jnp.where(pl.program_id(0) < n_lat_tiles, x, x_refs[1][...])
    o_ref[...] = x + gt_ref[0] * _dot(a_ref[...], w_ref[...])


def _oproj(A, W, X, gate, geo, nrt, bm, bn, layer=None):
    k = A.shape[1]
    d = W.shape[-1]
    mrow = functools.partial(geo["mrow_bm"], bm)
    if layer is None:
        w_spec = pl.BlockSpec((k, bn), lambda i, j: (0, j))
    else:
        w_spec = pl.BlockSpec((None, k, bn), lambda i, j: (layer, 0, j))
    if isinstance(X, tuple):
        nlt = X[0].shape[0] // bm
        x_specs = [pl.BlockSpec((bm, bn), lambda i, j: (jnp.minimum(i, nlt - 1), j)),
                   pl.BlockSpec((bm, bn), lambda i, j: (jnp.maximum(i - nlt, 0), j))]
        x_args = list(X)
    else:
        nlt = None
        x_specs = [pl.BlockSpec((bm, bn), lambda i, j: (i, j))]
        x_args = [X]
    return pl.pallas_call(
        functools.partial(_oproj_body, nlt),
        grid=(nrt, d // bn),
        in_specs=[pl.BlockSpec((bm, k), lambda i, j: (i, 0)), w_spec] + x_specs
        + [pl.BlockSpec((1, 1, bn), lambda i, j: (mrow(i), 0, j))],
        out_specs=pl.BlockSpec((bm, bn), lambda i, j: (i, j)),
        out_shape=jax.ShapeDtypeStruct((nrt * bm, d), _F32),
        compiler_params=_cp("arbitrary", "arbitrary"),
        name="oproj",
    )(A, W, *x_args, gate)


def _ffn1_body(bm, n_x_rows, seq, ctx_len, nc, rc, x0_ref, x0p_ref, x0n_ref, xn_ref, xnp_ref, xnn_ref,
               g_ref, sh_ref, sc_ref, shn_ref, scn_ref, wg_ref, wu_ref, cw_ref, cb_ref, o_ref, ha_ref, hb_ref):
    i = pl.program_id(0)
    j = pl.program_id(1)
    h_refs = (ha_ref, hb_ref)

    @pl.when((i == 0) & (j == 0))
    def _():
        g, sh, sc = g_ref[...], sh_ref[0], sc_ref[0]
        ha_ref[0:bm, :] = _norm_mod(x0_ref[...], g, sh, sc).astype(_BF)
        halo = jnp.concatenate([x0p_ref[...], x0n_ref[...]], axis=0)
        ha_ref[bm:bm + 2 * _HALO, :] = _norm_mod(halo, g, sh, sc).astype(_BF)

    for par in (0, 1):
        @pl.when(i % 2 == par)
        def _(par=par):
            h_ref, hn_ref = h_refs[par], h_refs[1 - par]
            gx = _dot(h_ref[...], wg_ref[...])
            u = _dot(h_ref[0:bm, :], wu_ref[...])
            g, shn, scn = g_ref[...], shn_ref[0], scn_ref[0]
            r0 = pl.multiple_of(jnp.minimum(j, nc - 1) * rc, rc)
            hn_ref[pl.ds(r0, rc), :] = _norm_mod(xn_ref[...], g, shn, scn).astype(_BF)
            halo = jnp.concatenate([xnp_ref[...], xnn_ref[...]], axis=0)
            hn_ref[bm:bm + 2 * _HALO, :] = _norm_mod(halo, g, shn, scn).astype(_BF)
            gm = gx[0:bm]
            g_prev = gx[bm + _HALO - 1:bm + _HALO]
            g_next = gx[bm + _HALO:bm + _HALO + 1]
            row = lax.broadcasted_iota(jnp.int32, (bm, 1), 0)
            tok = i * bm + row
            period = jnp.where(i * bm >= n_x_rows, ctx_len, seq)
            up = jnp.where(row == 0, g_prev, pltpu.roll(gm, 1, 0))
            dn = jnp.where(row == bm - 1, g_next, pltpu.roll(gm, bm - 1, 0))
            up = jnp.where((tok & (period - 1)) != 0, up, 0.0)
            dn = jnp.where(((tok + 1) & (period - 1)) != 0, dn, 0.0)
            cw = cw_ref[...]
            gc = cb_ref[...] + up * cw[0:1] + gm * cw[1:2] + dn * cw[2:3]
            o_ref[...] = (jax.nn.gelu(gc) * u).astype(o_ref.dtype)


def _ffn1(X, g, sh, sc, w_in, conv_w, conv_b, layer, geo, nrt, bf):
    t, d = X.shape
    f = conv_w.shape[2]
    bm = geo["bm"]
    mrow = geo["mrow"]
    hb = bm // _HALO
    last = t // _HALO - 1
    nf = f // bf
    nc, rc = _next_rows_chunks(bm, nf)
    nxt = lambda i: jnp.minimum(i + 1, nrt - 1)
    body = functools.partial(_ffn1_body, bm, geo["n_x_rows"], geo["seq"], geo["ctx_len"], nc, rc)
    return pl.pallas_call(
        body,
        grid=(nrt, nf),
        in_specs=[
            pl.BlockSpec((bm, d), lambda i, j: (0, 0)),
            pl.BlockSpec((_HALO, d), lambda i, j: (0, 0)),
            pl.BlockSpec((_HALO, d), lambda i, j: (jnp.minimum(hb, last), 0)),
            pl.BlockSpec((rc, d), lambda i, j: (nxt(i) * nc + jnp.minimum(j, nc - 1), 0)),
            pl.BlockSpec((_HALO, d), lambda i, j: (jnp.maximum(nxt(i) * hb - 1, 0), 0)),
            pl.BlockSpec((_HALO, d), lambda i, j: (jnp.minimum((nxt(i) + 1) * hb, last), 0)),
            pl.BlockSpec((1, d), lambda i, j: (0, 0)),
            pl.BlockSpec((1, 1, d), lambda i, j: (mrow(i), 0, 0)),
            pl.BlockSpec((1, 1, d), lambda i, j: (mrow(i), 0, 0)),
            pl.BlockSpec((1, 1, d), lambda i, j: (mrow(nxt(i)), 0, 0)),
            pl.BlockSpec((1, 1, d), lambda i, j: (mrow(nxt(i)), 0, 0)),
            pl.BlockSpec((None, d, bf), lambda i, j: (layer, 0, j)),
            pl.BlockSpec((None, d, bf), lambda i, j: (layer, 0, j + nf)),
            pl.BlockSpec((None, 3, bf), lambda i, j: (layer, 0, j)),
            pl.BlockSpec((None, 1, bf), lambda i, j: (layer, 0, j)),
        ],
        out_specs=pl.BlockSpec((bm, bf), lambda i, j: (i, j)),
        out_shape=jax.ShapeDtypeStruct((nrt * bm, f), _BF),
        scratch_shapes=[pltpu.VMEM((bm + 2 * _HALO, d), _BF), pltpu.VMEM((bm + 2 * _HALO, d), _BF)],
        compiler_params=_cp("arbitrary", "arbitrary"),
        name="ffn1",
    )(X, X, X, X, X, X, g, sh, sc, sh, sc, w_in, w_in, conv_w, conv_b)


def _with_ones(v):
    return jnp.concatenate([v, jnp.ones(v.shape, v.dtype)], axis=1)


def _lane_tiles(x, n):
    return jnp.concatenate([x] * n, axis=1) if n > 1 else x


def _na_body(seq, sub, kwin, hps, nsub, q_ref, k_ref, v_ref, kc_ref, vc_ref, ba_ref, bm_ref, bb_ref, o_ref):
    j = pl.program_id(2)
    b_refs = (ba_ref,) + (bm_ref,) * (nsub - 2) + (bb_ref,)
    for hh in range(hps):
        c = hh * _LANES
        kc = kc_ref[:, c:c + _LANES]
        vcx = _with_ones(vc_ref[:, c:c + _LANES])
        for i, b_ref in enumerate(b_refs):
            q0 = (nsub * j + i) * sub
            ks = pl.multiple_of(jnp.clip(q0 - (kwin - sub) // 2, 0, seq - kwin), 256)
            q = q_ref[i * sub:(i + 1) * sub, c:c + _LANES]
            k = k_ref[pl.ds(ks, kwin), c:c + _LANES]
            v = v_ref[pl.ds(ks, kwin), c:c + _LANES]
            s = _dot_nt(q, k) + b_ref[0, hh]
            sc = _dot_nt(q, kc)
            m = jnp.maximum(jnp.max(s, axis=-1, keepdims=True), jnp.max(sc, axis=-1, keepdims=True))
            p = jnp.exp2(s - m).astype(_BF)
            pc = jnp.exp2(sc - m).astype(_BF)
            acc = _dot(p, _with_ones(v)) + _dot(pc, vcx)
            o_ref[i * sub:(i + 1) * sub, c:c + _LANES] = (acc[:, :_LANES] / acc[:, _LANES:]).astype(o_ref.dtype)


def _na_bias_table(rel_bias, rows, rq, rk):
    h, nr2, nc2 = rel_bias.shape
    na_rows, na_cols = (nr2 + 1) // 2, (nc2 + 1) // 2
    kr = min(na_rows, rows)
    w = _GRID_W
    nblk = rows // rq
    c = np.arange(w)
    c0 = np.clip(c - na_cols // 2, 0, w - na_cols)
    col_ok = (c[None, :] >= c0[:, None]) & (c[None, :] < c0[:, None] + na_cols)
    rbp = jnp.pad(rel_bias.astype(_F32) * _LOG2E, ((0, 0), (0, 0), (w, w)))
    tcol = jnp.stack([rbp[:, :, na_cols - 1 - qc + w:na_cols - 1 - qc + 2 * w] for qc in range(w)], axis=2)
    tcol = jnp.where(jnp.asarray(col_ok)[None, None], tcol, _NEG)
    tcol = jnp.concatenate([tcol, jnp.full((h, 1, w, w), _NEG, _F32)], axis=1)
    sel = np.zeros((3, rq, rk, nr2 + 1), np.float32)
    for ti, jb in enumerate((0, min(1, nblk - 1), nblk - 1)):
        kb0 = int(np.clip(jb * rq - (rk - rq) // 2, 0, rows - rk))
        for qi in range(rq):
            r = jb * rq + qi
            r0 = int(np.clip(r - kr // 2, 0, rows - kr))
            for ki in range(rk):
                krow = kb0 + ki
                sel[ti, qi, ki, krow - r + na_rows - 1 if r0 <= krow < r0 + kr else nr2] = 1.0
    tab = jnp.einsum("tqkd,hdcx->thqckx", jnp.asarray(sel), tcol, precision=lax.Precision.HIGHEST)
    return tab.reshape(3, h, rq * w, rk * w)


def _na_attention(QKV, bias_tab, geo, n_heads):
    b, s, lc = geo["batch"], geo["seq"], geo["ctx_len"]
    t = QKV.shape[0]
    sub, kwin = bias_tab.shape[2], bias_tab.shape[3]
    nsub = min(8, s // sub)
    bq = nsub * sub
    nj = s // bq
    cblk = geo["n_x_rows"] // lc
    hps = 2
    hw = hps * _LANES
    ng = n_heads // hps
    return pl.pallas_call(
        functools.partial(_na_body, s, sub, kwin, hps, nsub),
        grid=(b, ng, nj),
        in_specs=[
            pl.BlockSpec((bq, hw), lambda bi, h, j: (bi * nj + j, h)),
            pl.BlockSpec((s, hw), lambda bi, h, j: (bi, ng + h)),
            pl.BlockSpec((s, hw), lambda bi, h, j: (bi, 2 * ng + h)),
            pl.BlockSpec((lc, hw), lambda bi, h, j: (cblk + bi, ng + h)),
            pl.BlockSpec((lc, hw), lambda bi, h, j: (cblk + bi, 2 * ng + h)),
            pl.BlockSpec((1, hps, sub, kwin), lambda bi, h, j: (jnp.where(j == 0, 0, 1), h, 0, 0)),
            pl.BlockSpec((1, hps, sub, kwin), lambda bi, h, j: (1, h, 0, 0)),
            pl.BlockSpec((1, hps, sub, kwin), lambda bi, h, j: (jnp.where(j == nj - 1, 2, 1), h, 0, 0)),
        ],
        out_specs=pl.BlockSpec((bq, hw), lambda bi, h, j: (bi * nj + j, h)),
        out_shape=jax.ShapeDtypeStruct((t, n_heads * _LANES), _BF),
        compiler_params=_cp("arbitrary", "arbitrary", "arbitrary"),
        name="na_attn",
    )(QKV, QKV, QKV, QKV, QKV, bias_tab, bias_tab, bias_tab)


def _ctx_attn_body(group, has_sink, *refs):
    if has_sink:
        q_ref, k_ref, v_ref, sk_ref, _, o_ref = refs
    else:
        q_ref, k_ref, v_ref, _, o_ref = refs
    k = k_ref[...]
    vx = _with_ones(v_ref[...])
    for g in range(group):
        c = g * _LANES
        s = _dot_nt(q_ref[:, c:c + _LANES], k)
        m = jnp.max(s, axis=-1, keepdims=True)
        if has_sink:
            sk = sk_ref[0][:, c:c + 1]
            m = jnp.maximum(m, sk)
        acc = _dot(jnp.exp2(s - m).astype(_BF), vx)
        l = acc[:, _LANES:]
        if has_sink:
            l = l + jnp.exp2(sk - m)
        o_ref[:, c:c + _LANES] = (acc[:, :_LANES] / l).astype(o_ref.dtype)


def _ctx_attention(QKV, O, geo, n_heads, n_kv, sink=None):
    b, lc = geo["batch"], geo["ctx_len"]
    group = n_heads // n_kv
    gw = group * _LANES
    cblk = geo["n_x_rows"] // lc
    in_specs = [
        pl.BlockSpec((lc, gw), lambda bi, n: (cblk + bi, n)),
        pl.BlockSpec((lc, _LANES), lambda bi, n: (cblk + bi, n_heads + n)),
        pl.BlockSpec((lc, _LANES), lambda bi, n: (cblk + bi, n_heads + n_kv + n)),
    ]
    args = [QKV, QKV, QKV]
    if sink is not None:
        in_specs.append(pl.BlockSpec((1, 1, gw), lambda bi, n: (n, 0, 0)))
        args.append(sink)
    in_specs.append(pl.BlockSpec(memory_space=pl.ANY))
    args.append(O)
    return pl.pallas_call(
        functools.partial(_ctx_attn_body, group, sink is not None),
        grid=(b, n_kv),
        in_specs=in_specs,
        out_specs=pl.BlockSpec((lc, gw), lambda bi, n: (cblk + bi, n)),
        out_shape=jax.ShapeDtypeStruct(O.shape, O.dtype),
        input_output_aliases={len(args) - 1: 0},
        compiler_params=_cp("arbitrary", "arbitrary"),
        name="ctx_attn",
    )(*args)


def _swa_body(seq, bq, nblk, win, group, q_ref, k_ref, v_ref, kc_ref, vc_ref, sk_ref, o_ref):
    kw = bq + 2 * win
    kc = kc_ref[...]
    vcx = _with_ones(vc_ref[...])
    for bi in range(nblk):
        t = pl.program_id(2) * nblk + bi
        rows = slice(bi * bq, (bi + 1) * bq)
        ks = pl.multiple_of(jnp.clip(t * bq - win, 0, seq - kw), _LANES)
        k = k_ref[pl.ds(ks, kw), :]
        vx = _with_ones(v_ref[pl.ds(ks, kw), :])
        qpos = t * bq + lax.broadcasted_iota(jnp.int32, (bq, 1), 0)
        kpos = ks + lax.broadcasted_iota(jnp.int32, (1, kw), 1)
        band = jnp.abs(kpos - qpos) <= win
        for g in range(group):
            c = g * _LANES
            q = q_ref[rows, c:c + _LANES]
            s = jnp.where(band, _dot_nt(q, k), _NEG)
            sc = _dot_nt(q, kc)
            sk = sk_ref[0][:, c:c + 1]
            m = jnp.maximum(jnp.maximum(jnp.max(s, axis=-1, keepdims=True), jnp.max(sc, axis=-1, keepdims=True)), sk)
            p = jnp.exp2(s - m).astype(_BF)
            pc = jnp.exp2(sc - m).astype(_BF)
            acc = _dot(p, vx) + _dot(pc, vcx)
            l = acc[:, _LANES:] + jnp.exp2(sk - m)
            o_ref[rows, c:c + _LANES] = (acc[:, :_LANES] / l).astype(o_ref.dtype)


def _swa_attention(QKV, sink, geo, n_heads, n_kv, bq):
    b, s, lc = geo["batch"], geo["seq"], geo["ctx_len"]
    t = QKV.shape[0]
    group = n_heads // n_kv
    gw = group * _LANES
    nblk = 2
    bs = nblk * bq
    nq = s // bs
    cblk = geo["n_x_rows"] // lc
    return pl.pallas_call(
        functools.partial(_swa_body, s, bq, nblk, _SWA_WINDOW, group),
        grid=(b, n_kv, nq),
        in_specs=[
            pl.BlockSpec((bs, gw), lambda bi, n, j: (bi * nq + j, n)),
            pl.BlockSpec((s, _LANES), lambda bi, n, j: (bi, n_heads + n)),
            pl.BlockSpec((s, _LANES), lambda bi, n, j: (bi, n_heads + n_kv + n)),
            pl.BlockSpec((lc, _LANES), lambda bi, n, j: (cblk + bi, n_heads + n)),
            pl.BlockSpec((lc, _LANES), lambda bi, n, j: (cblk + bi, n_heads + n_kv + n)),
            pl.BlockSpec((1, 1, gw), lambda bi, n, j: (n, 0, 0)),
        ],
        out_specs=pl.BlockSpec((bs, gw), lambda bi, n, j: (bi * nq + j, n)),
        out_shape=jax.ShapeDtypeStruct((t, n_heads * _LANES), _BF),
        compiler_params=_cp("arbitrary", "arbitrary", "arbitrary"),
        name="swa_attn",
    )(QKV, QKV, QKV, QKV, QKV, sink)


def _gqa_body(seq, ck, group, q_ref, k_ref, v_ref, kc_ref, vc_ref, o_ref, m_ref, acc_ref):
    m_ref[...] = jnp.full(m_ref.shape, _NEG, _F32)
    acc_ref[...] = jnp.zeros(acc_ref.shape, _F32)

    def step(k, v):
        width = k.shape[0]
        vx = jnp.concatenate([v, jnp.ones((width, _LANES), v.dtype)], axis=1)
        for g in range(group):
            s = _dot_nt(q_ref[:, g * _LANES:(g + 1) * _LANES], k)
            m_old = m_ref[g]
            m_new = jnp.maximum(m_old, jnp.max(s, axis=-1, keepdims=True))
            alpha = jnp.exp2(m_old - m_new)
            p = jnp.exp2(s - _lane_tiles(m_new, width // _LANES)).astype(_BF)
            acc_ref[g] = _lane_tiles(alpha, 2) * acc_ref[g] + _dot(p, vx)
            m_ref[g] = m_new

    def chunk(c, carry):
        off = pl.multiple_of(c * ck, ck)
        step(k_ref[pl.ds(off, ck), :], v_ref[pl.ds(off, ck), :])
        return carry

    lax.fori_loop(0, seq // ck, chunk, 0, unroll=min(8, seq // ck))
    step(kc_ref[...], vc_ref[...])
    for g in range(group):
        acc = acc_ref[g]
        o_ref[:, g * _LANES:(g + 1) * _LANES] = (acc[:, :_LANES] / acc[:, _LANES:]).astype(o_ref.dtype)


def _gqa_attention(QKV, geo, n_heads, n_kv, bq, ck):
    b, s, lc = geo["batch"], geo["seq"], geo["ctx_len"]
    group = n_heads // n_kv
    gw = group * _LANES
    nq = s // bq
    cblk = geo["n_x_rows"] // lc
    return pl.pallas_call(
        functools.partial(_gqa_body, s, ck, group),
        grid=(b, n_kv, nq),
        in_specs=[
            pl.BlockSpec((bq, gw), lambda bi, n, j: (bi * nq + j, n)),
            pl.BlockSpec((s, _LANES), lambda bi, n, j: (bi, n_heads + n)),
            pl.BlockSpec((s, _LANES), lambda bi, n, j: (bi, n_heads + n_kv + n)),
            pl.BlockSpec((lc, _LANES), lambda bi, n, j: (cblk + bi, n_heads + n)),
            pl.BlockSpec((lc, _LANES), lambda bi, n, j: (cblk + bi, n_heads + n_kv + n)),
        ],
        out_specs=pl.BlockSpec((bq, gw), lambda bi, n, j: (bi * nq + j, n)),
        out_shape=jax.ShapeDtypeStruct((geo["n_x_rows"], n_heads * _LANES), _BF),
        scratch_shapes=[pltpu.VMEM((group, bq, _LANES), _F32), pltpu.VMEM((group, bq, 2 * _LANES), _F32)],
        compiler_params=_cp("arbitrary", "arbitrary", "arbitrary"),
        name="gqa_attn",
    )(QKV, QKV, QKV, QKV, QKV)


def _log_sigmoid(x):
    return jnp.minimum(x, 0.0) - jnp.log1p(jnp.exp(-jnp.abs(x)))


def _mlstm_chunk(d, st, n_heads, L, h, q_ref, k_ref, v_ref, gt_ref, gb_ref, o_ref, c_ref, n_ref, m_ref):
    gates = gt_ref[...] + gb_ref[...]
    lane = lax.broadcasted_iota(jnp.int32, (1, _LANES), 1)
    li_col = jnp.sum(jnp.where(lane == (2 * d) * n_heads + h, gates, 0.0), axis=-1, keepdims=True)
    lf_pre = jnp.sum(jnp.where(lane == (2 * d + 1) * n_heads + h, gates, 0.0), axis=-1, keepdims=True)
    lf_col = _log_sigmoid(lf_pre)
    ti = lax.broadcasted_iota(jnp.int32, (L, L), 0)
    si = lax.broadcasted_iota(jnp.int32, (L, L), 1)
    eye = ti == si
    li_row = jnp.sum(jnp.where(eye, li_col, 0.0), axis=0, keepdims=True)
    lf_row = jnp.sum(jnp.where(eye, lf_col, 0.0), axis=0, keepdims=True)
    allowed = (ti >= si) if d == 0 else (ti <= si)
    allowed_t = (si >= ti) if d == 0 else (si <= ti)
    b_col = jnp.sum(jnp.where(allowed, lf_row, 0.0), axis=-1, keepdims=True)
    b_row = jnp.sum(jnp.where(allowed_t, lf_col, 0.0), axis=0, keepdims=True)
    total = jnp.sum(lf_col, axis=0, keepdims=True)
    m_prev = m_ref[st, 0:1, 0:1]

    q = q_ref[...]
    k = k_ref[...]
    v = v_ref[...]
    dmat = jnp.where(allowed, b_col - b_row + li_row, _NEG)
    g_col = b_col + m_prev
    m_t = jnp.maximum(g_col, jnp.max(dmat, axis=-1, keepdims=True))
    w = jnp.exp(dmat - m_t) * _dot_nt(q, k)
    w_prev = jnp.exp(g_col - m_t)
    cmat = c_ref[st]
    nvec = n_ref[st, 0:1, :]
    num = _dot(w.astype(_BF), v) + w_prev * _dot(q, cmat.astype(_BF))
    qn = jnp.sum(q.astype(_F32) * nvec, axis=-1, keepdims=True)
    den = jnp.sum(w, axis=-1, keepdims=True) + w_prev * qn
    o_ref[...] = (num / jnp.maximum(jnp.abs(den), jnp.exp(-m_t))).astype(o_ref.dtype)

    lw = total - b_col + li_col
    m_new = jnp.maximum(total + m_prev, jnp.max(lw, axis=0, keepdims=True))
    decay = jnp.exp(total + m_prev - m_new)
    kw = k.astype(_F32) * jnp.exp(lw - m_new)
    c_ref[st] = decay * cmat + _dot_tn(kw.astype(_BF), v)
    n_ref[st] = jnp.broadcast_to(decay * nvec + jnp.sum(kw, axis=0, keepdims=True), n_ref.shape[1:])
    m_ref[st] = jnp.broadcast_to(m_new, m_ref.shape[1:])


def _mlstm_body(n_heads, hps, dqk, dv, chunk, qf_ref, kf_ref, vf_ref, gf_ref, qb_ref, kb_ref, vb_ref, gbk_ref,
                gb_ref, of_ref, ob_ref, c_ref, n_ref, m_ref):
    hg = pl.program_id(1)

    @pl.when(pl.program_id(2) == 0)
    def _():
        c_ref[...] = jnp.zeros(c_ref.shape, _F32)
        n_ref[...] = jnp.zeros(n_ref.shape, _F32)
        m_ref[...] = jnp.zeros(m_ref.shape, _F32)

    dirs = ((qf_ref, kf_ref, vf_ref, gf_ref, of_ref), (qb_ref, kb_ref, vb_ref, gbk_ref, ob_ref))
    for hh in range(hps):
        qs, vs = slice(hh * dqk, (hh + 1) * dqk), slice(hh * dv, (hh + 1) * dv)
        for d, (q_ref, k_ref, v_ref, g_ref, o_ref) in enumerate(dirs):
            _mlstm_chunk(d, d * hps + hh, n_heads, chunk, hg * hps + hh, q_ref.at[:, qs], k_ref.at[:, qs],
                         v_ref.at[:, vs], g_ref, gb_ref, o_ref.at[:, vs], c_ref, n_ref, m_ref)


def _mlstm_scan(P, G, gate_b, geo, n_heads, dqk, dv, chunk):
    b, s, lc = geo["batch"], geo["seq"], geo["ctx_len"]
    t = P.shape[0]
    nctx, nx = lc // chunk, s // chunk
    cbase = geo["n_x_rows"] // chunk

    def rblk(d, bi, c):
        cc = c - nctx
        in_ctx = cbase + bi * nctx + (c if d == 0 else nctx - 1 - c)
        in_x = bi * nx + (cc if d == 0 else nx - 1 - cc)
        return jnp.where(c < nctx, in_ctx, in_x)

    hps = 4
    ng = n_heads // hps
    k_off = ng
    v_off = 2 * n_heads * dqk // (hps * dv)
    in_specs, args = [], []
    for d in (0, 1):
        in_specs += [
            pl.BlockSpec((chunk, hps * dqk), lambda bi, h, c, d=d: (rblk(d, bi, c), h)),
            pl.BlockSpec((chunk, hps * dqk), lambda bi, h, c, d=d: (rblk(d, bi, c), k_off + h)),
            pl.BlockSpec((chunk, hps * dv), lambda bi, h, c, d=d: (rblk(d, bi, c), v_off + h)),
            pl.BlockSpec((chunk, _LANES), lambda bi, h, c, d=d: (rblk(d, bi, c), 0)),
        ]
        args += [P, P, P, G]
    in_specs.append(pl.BlockSpec((1, _LANES), lambda bi, h, c: (0, 0)))
    args.append(gate_b)
    out_sds = jax.ShapeDtypeStruct((t, n_heads * dv), _BF)
    ns = 2 * hps
    return pl.pallas_call(
        functools.partial(_mlstm_body, n_heads, hps, dqk, dv, chunk),
        grid=(b, ng, nctx + nx),
        in_specs=in_specs,
        out_specs=[pl.BlockSpec((chunk, hps * dv), lambda bi, h, c, d=d: (rblk(d, bi, c), h)) for d in (0, 1)],
        out_shape=[out_sds, out_sds],
        scratch_shapes=[pltpu.VMEM((ns, dqk, dv), _F32), pltpu.VMEM((ns, 8, dqk), _F32), pltpu.VMEM((ns, 8, _LANES), _F32)],
        compiler_params=_cp("arbitrary", "arbitrary", "arbitrary"),
        name="mlstm_scan",
    )(*args)


def _ml_out_body(n_heads, dv, hf_ref, hb_ref, og_ref, hg_ref, w_ref, x_ref, gt_ref, o_ref):
    hsum = hf_ref[...].astype(_F32) + hb_ref[...].astype(_F32)
    parts = []
    for hh in range(n_heads):
        a = hsum[:, hh * dv:(hh + 1) * dv]
        hn = a * lax.rsqrt(jnp.mean(a * a, axis=-1, keepdims=True) + _EPS) * hg_ref[:, hh * dv:(hh + 1) * dv]
        og = og_ref[:, hh * dv:(hh + 1) * dv].astype(_F32)
        parts.append((jax.nn.sigmoid(og) * hn).astype(_BF))
    a = jnp.concatenate(parts, axis=-1)
    o_ref[...] = x_ref[...] + gt_ref[0] * _dot(a, w_ref[...])


def _ml_out(Hf, Hb, P, head_g, W, X, gate, geo, n_heads, dv, bm):
    t, d = X.shape
    kd = n_heads * dv
    o_off = (P.shape[1] - kd) // kd
    mrow = functools.partial(geo["mrow_bm"], bm)
    return pl.pallas_call(
        functools.partial(_ml_out_body, n_heads, dv),
        grid=(t // bm,),
        in_specs=[
            pl.BlockSpec((bm, kd), lambda i: (i, 0)),
            pl.BlockSpec((bm, kd), lambda i: (i, 0)),
            pl.BlockSpec((bm, kd), lambda i: (i, o_off)),
            pl.BlockSpec((1, kd), lambda i: (0, 0)),
            pl.BlockSpec((kd, d), lambda i: (0, 0)),
            pl.BlockSpec((bm, d), lambda i: (i, 0)),
            pl.BlockSpec((1, 1, d), lambda i: (mrow(i), 0, 0)),
        ],
        out_specs=pl.BlockSpec((bm, d), lambda i: (i, 0)),
        out_shape=jax.ShapeDtypeStruct((t, d), _F32),
        compiler_params=_cp("arbitrary"),
        name="ml_out",
    )(Hf, Hb, P, head_g, W, X, gate)


def _rope_tables(seq, pad_rows):
    tpos = np.arange(seq)
    row = (tpos // _GRID_W).astype(np.float32)
    col = (tpos % _GRID_W).astype(np.float32)
    nf = _LANES // 4
    inv = (_ROPE_BASE ** (-jnp.arange(nf, dtype=_F32) / nf))
    ar = jnp.asarray(row)[:, None] * inv
    ac = jnp.asarray(col)[:, None] * inv
    cr, sr, cc, sc = jnp.cos(ar), jnp.sin(ar), jnp.cos(ac), jnp.sin(ac)
    cs = jnp.concatenate([cr, cc, cr, cc], axis=1)
    sn = jnp.concatenate([-sr, -sc, sr, sc], axis=1)
    cs = jnp.concatenate([cs, jnp.ones((pad_rows, _LANES), _F32)], axis=0)
    sn = jnp.concatenate([sn, jnp.zeros((pad_rows, _LANES), _F32)], axis=0)
    return cs, sn


def _pair_major(v):
    q = _LANES // 4
    return v.reshape(v.shape[:-1] + (v.shape[-1] // _LANES, 2, 2, q)).swapaxes(-3, -2).reshape(v.shape)


def _rope_operands(w_qkv, q_gain, k_gain, n_qk_heads, cs, sn):
    nqk = n_qk_heads * _LANES
    w = jnp.concatenate([_pair_major(w_qkv[:, :nqk]), w_qkv[:, nqk:]], axis=1).astype(_BF)
    tabs = []
    for gain in (q_gain, k_gain):
        gp = _pair_major(gain.astype(_F32))
        tabs += [cs * gp[None, :], sn * jnp.roll(gp, _LANES // 2)[None, :]]
    return w, tuple(tabs)


def _geometry(batch, seq, ctx_len):
    bm = min(1024, batch * ctx_len)
    assert seq % bm == 0 and (batch * ctx_len) % bm == 0
    assert seq & (seq - 1) == 0 and ctx_len & (ctx_len - 1) == 0
    n_x_rows = batch * seq

    def mrow_bm(bm_, i):
        return jnp.minimum((i * bm_) // seq, batch)

    return dict(batch=batch, seq=seq, ctx_len=ctx_len, bm=bm, n_x_rows=n_x_rows,
                n_x_tiles=n_x_rows // bm, tiles_per_seq=seq // bm,
                mrow=functools.partial(mrow_bm, bm), mrow_bm=mrow_bm)


def kernel(x, c, ctx, c_ctx, ada_w, ada_b, norm1_g, norm2_g, ffn_w_in, ffn_conv_w, ffn_conv_b, ffn_w_out,
           na_w_qkv, na_q_g, na_k_g, na_rel_bias, na_w_o,
           swa_w_qkv, swa_q_g, swa_k_g, swa_sinks, swa_w_o,
           ml_w_in, ml_gate_b, ml_head_g, ml_w_o,
           gqa_w_qkv, gqa_q_g, gqa_k_g, gqa_w_o):
    batch, seq, d = x.shape
    ctx_len = ctx.shape[1]
    depth = ada_w.shape[0]
    n_heads = d // _LANES
    geo = _geometry(batch, seq, ctx_len)
    bm = geo["bm"]
    n_x_rows = geo["n_x_rows"]
    n_x_tiles = geo["n_x_tiles"]
    qscale = (_LANES ** -0.5) * _LOG2E

    X = (x.reshape(n_x_rows, d), ctx.reshape(batch * ctx_len, d))
    t_all = n_x_rows + batch * ctx_len
    nrt_all = t_all // bm

    cond = jnp.concatenate([c, c_ctx[None, :], jnp.zeros((8 - batch - 1, d), _F32)], axis=0)
    mods = _adaln(cond, ada_w, ada_b)

    cs, sn = _rope_tables(seq, bm)
    f = ffn_conv_w.shape[2]
    w_in_b = ffn_w_in.astype(_BF)
    w_out_b = ffn_w_out.astype(_BF)
    conv_b3 = ffn_conv_b.reshape(depth, 1, f)
    bf = 512
    bm_o = min(512, bm)

    for i in range(depth):
        kind, jl = i % 4, i // 4
        need_ctx = i < depth - 1
        mod = [mods[i, :batch + 1, k * d:(k + 1) * d].reshape(batch + 1, 1, d) for k in range(6)]
        sh1, sc1, g1, sh2, sc2, g2 = mod
        n1 = norm1_g[i].reshape(1, d)
        n2 = norm2_g[i].reshape(1, d)
        nrt_o = t_all // bm_o if need_ctx else n_x_rows // bm_o

        if kind == 0:
            w = na_w_qkv[jl].astype(_BF)
            gq = (na_q_g[jl] * qscale).reshape(1, _LANES)
            gk = na_k_g[jl].reshape(1, _LANES)
            bn = _wide_tile(w.shape[1], 1536)
            QKV = _proj(X, n1, sh1, sc1, w, _qkv_layouts(n_heads, n_heads, bn, False), geo, bn, _BF, gains=(gq, gk))
            rows = seq // _GRID_W
            tab = _na_bias_table(na_rel_bias[jl], rows, 4, 12)
            O = _na_attention(QKV, tab, geo, n_heads)
            if need_ctx:
                O = _ctx_attention(QKV, O, geo, n_heads, n_heads)
            X = _oproj(O, na_w_o[jl].astype(_BF), X, g1, geo, nrt_o, bm_o, d)
        elif kind == 1:
            n_kv = (swa_w_qkv.shape[2] // _LANES - n_heads) // 2
            w, tabs = _rope_operands(swa_w_qkv[jl], swa_q_g[jl] * qscale, swa_k_g[jl], n_heads + n_kv, cs, sn)
            bn = _wide_tile(w.shape[1])
            QKV = _proj(X, n1, sh1, sc1, w, _qkv_layouts(n_heads, n_kv, bn, True), geo, bn, _BF, rope=tabs)
            group = n_heads // n_kv
            sink = jnp.repeat(swa_sinks[jl].astype(_F32) * _LOG2E, _LANES).reshape(n_kv, 1, group * _LANES)
            O = _swa_attention(QKV, sink, geo, n_heads, n_kv, 256)
            if need_ctx:
                O = _ctx_attention(QKV, O, geo, n_heads, n_kv, sink)
            X = _oproj(O, swa_w_o[jl].astype(_BF), X, g1, geo, nrt_o, bm_o, d)
        elif kind == 2:
            mh = ml_gate_b.shape[1] // 4
            dv = d // mh
            dqk = dv // 2
            nmain = 2 * mh * dqk + 2 * mh * dv
            w_main = ml_w_in[jl][:, :nmain].astype(_BF)
            w_gate = jnp.pad(ml_w_in[jl][:, nmain:], ((0, 0), (0, _LANES - 4 * mh))).astype(_BF)
            bn = _wide_tile(nmain, 1536)
            segs_by_tile = []
            k0, k1 = mh * dqk, 2 * mh * dqk
            for jt in range(nmain // bn):
                lo, hi = jt * bn, (jt + 1) * bn
                cuts = sorted({lo, hi, min(max(k0, lo), hi), min(max(k1, lo), hi)})
                segs = tuple((a - lo, b_ - lo, "plain", 0, False, dqk ** -0.5 if k0 <= a < k1 else 1.0)
                             for a, b_ in zip(cuts[:-1], cuts[1:]))
                segs_by_tile.append((jt, jt + 1, segs))
            P = _proj(X, n1, sh1, sc1, w_main, tuple(segs_by_tile), geo, bn, _BF)
            G = _proj(X, n1, sh1, sc1, w_gate, ((0, 1, ((0, _LANES, "plain", 0, False, 1.0),)),), geo, _LANES, _F32)
            gb = jnp.pad(ml_gate_b[jl].astype(_F32), (0, _LANES - 4 * mh)).reshape(1, _LANES)
            Hf, Hb = _mlstm_scan(P, G, gb, geo, mh, dqk, dv, 256)
            X = _ml_out(Hf, Hb, P, ml_head_g[jl].reshape(1, mh * dv), ml_w_o[jl].astype(_BF), X, g1, geo, mh, dv, bm_o)
        else:
            n_kv = (gqa_w_qkv.shape[2] // _LANES - n_heads) // 2
            w, tabs = _rope_operands(gqa_w_qkv[jl], gqa_q_g[jl] * qscale, gqa_k_g[jl], n_heads + n_kv, cs, sn)
            bn = _wide_tile(w.shape[1])
            QKV = _proj(X, n1, sh1, sc1, w, _qkv_layouts(n_heads, n_kv, bn, True), geo, bn, _BF, rope=tabs)
            O = _gqa_attention(QKV, geo, n_heads, n_kv, min(1024, seq), 512)
            if need_ctx:
                Oc = jnp.zeros((t_all, d), _BF).at[:n_x_rows].set(O)
                O = _ctx_attention(QKV, Oc, geo, n_heads, n_kv)
            X = _oproj(O, gqa_w_o[jl].astype(_BF), X, g1, geo, nrt_o, bm_o, d)

        nrt_f = nrt_all if need_ctx else n_x_tiles
        A = _ffn1(X, n2, sh2, sc2, w_in_b, ffn_conv_w, conv_b3, i, geo, nrt_f, bf)
        X = _oproj(A, w_out_b, X, g2, geo, A.shape[0] // bm, bm, 512, layer=i)

    return X[:n_x_rows].reshape(batch, seq, d)
```

```python
import functools

import numpy as np
import jax
import jax.numpy as jnp
from jax import lax
from jax.experimental import pallas as pl
from jax.experimental.pallas import tpu as pltpu

_F32 = jnp.float32
_BF = jnp.bfloat16
_EPS = 1e-6
_NEG = -1e30
_LOG2E = 1.4426950408889634
_GRID_W = 64
_SWA_WINDOW = 128
_ROPE_BASE = 10000.0
_LANES = 128
_HALO = 8
_VMEM_LIMIT = 56 << 20


def _cp(*sem):
    return pltpu.CompilerParams(dimension_semantics=sem, vmem_limit_bytes=_VMEM_LIMIT)


def _dot(a, b):
    return jnp.dot(a, b, preferred_element_type=_F32)


def _dot_nt(a, b):
    return lax.dot_general(a, b, (((1,), (1,)), ((), ())), preferred_element_type=_F32)


def _dot_tn(a, b):
    return lax.dot_general(a, b, (((0,), (0,)), ((), ())), preferred_element_type=_F32)


def _norm_mod(x, g, sh, sc):
    ms = jnp.mean(x * x, axis=-1, keepdims=True)
    return (x * lax.rsqrt(ms + _EPS)) * (g * (1.0 + sc)) + sh


def _adaln_body(c_ref, w_ref, b_ref, o_ref):
    c = c_ref[...]
    s = (c * jax.nn.sigmoid(c)).astype(_BF)
    o_ref[0] = _dot(s, w_ref[0].astype(_BF)) + b_ref[0]


def _adaln(cond, ada_w, ada_b):
    depth, d, n = ada_w.shape
    bn = 1024
    return pl.pallas_call(
        _adaln_body,
        grid=(depth, n // bn),
        in_specs=[
            pl.BlockSpec((8, d), lambda l, j: (0, 0)),
            pl.BlockSpec((1, d, bn), lambda l, j: (l, 0, j)),
            pl.BlockSpec((1, 1, bn), lambda l, j: (l, 0, j)),
        ],
        out_specs=pl.BlockSpec((1, 8, bn), lambda l, j: (l, 0, j)),
        out_shape=jax.ShapeDtypeStruct((depth, 8, n), _F32),
        compiler_params=_cp("arbitrary", "arbitrary"),
        name="adaln",
    )(cond, ada_w, ada_b.reshape(depth, 1, n))


def _next_rows_chunks(bm, nj):
    nc = 1
    while nc * 2 <= nj and bm // (nc * 2) >= _LANES:
        nc *= 2
    return nc, bm // nc


def _proj_body(layouts, has_rope, n_gain, n_lat_tiles, *refs):
    x_refs = refs[:1] if n_lat_tiles is None else refs[:2]
    refs = refs[len(x_refs):]
    g_ref, sh_ref, sc_ref, w_ref = refs[:4]
    pos = 4
    tab_refs = ()
    if has_rope:
        tab_refs = refs[pos:pos + 4]
        pos += 4
    gain_refs = refs[pos:pos + n_gain]
    pos += n_gain
    o_ref, h_ref = refs[pos], refs[pos + 1]
    j = pl.program_id(1)

    is_lat = True if n_lat_tiles is None else pl.program_id(0) < n_lat_tiles
    for src_ref, mine in zip(x_refs, (is_lat, jnp.logical_not(is_lat))):
        @pl.when((j == 0) & mine)
        def _(src_ref=src_ref):
            h_ref[...] = _norm_mod(src_ref[...], g_ref[...], sh_ref[0], sc_ref[0]).astype(_BF)

    for lo, hi, segs in layouts:
        @pl.when((j >= lo) & (j < hi))
        def _(segs=segs):
            acc = _dot(h_ref[...], w_ref[...])
            for c0, c1, kind, gi, rope, mult in segs:
                if kind == "plain":
                    a = acc[:, c0:c1]
                    if mult != 1.0:
                        a = a * mult
                    o_ref[:, c0:c1] = a.astype(o_ref.dtype)
                    continue
                for c in range(c0, c1, _LANES):
                    a = acc[:, c:c + _LANES]
                    inv = lax.rsqrt(jnp.mean(a * a, axis=-1, keepdims=True) + _EPS)
                    if rope:
                        y = a * tab_refs[2 * gi][...] + pltpu.roll(a, _LANES // 2, 1) * tab_refs[2 * gi + 1][...]
                    else:
                        y = a * gain_refs[gi][...]
                    o_ref[:, c:c + _LANES] = (y * inv).astype(o_ref.dtype)


def _proj(X, g, sh, sc, W, layouts, geo, bn, out_dtype, rope=None, gains=()):
    n = W.shape[1]
    bm = geo["bm"]
    nj = n // bn
    mrow = geo["mrow"]
    if isinstance(X, tuple):
        nlt = X[0].shape[0] // bm
        t, d = X[0].shape[0] + X[1].shape[0], X[0].shape[1]
        x_specs = [pl.BlockSpec((bm, d), lambda i, j: (jnp.minimum(i, nlt - 1), 0)),
                   pl.BlockSpec((bm, d), lambda i, j: (jnp.maximum(i - nlt, 0), 0),
                                pipeline_mode=pl.Buffered(1))]
        x_args = list(X)
    else:
        nlt = None
        t, d = X.shape
        x_specs = [pl.BlockSpec((bm, d), lambda i, j: (i, 0))]
        x_args = [X]
    nrt = t // bm
    in_specs = x_specs + [
        pl.BlockSpec((1, d), lambda i, j: (0, 0)),
        pl.BlockSpec((1, 1, d), lambda i, j: (mrow(i), 0, 0)),
        pl.BlockSpec((1, 1, d), lambda i, j: (mrow(i), 0, 0)),
        pl.BlockSpec((d, bn), lambda i, j: (0, j)),
    ]
    args = x_args + [g, sh, sc, W]
    if rope is not None:
        tps, nxt = geo["tiles_per_seq"], geo["n_x_tiles"]
        tab = lambda i, j: (jnp.where(i < nxt, i % tps, tps), 0)
        in_specs += [pl.BlockSpec((bm, _LANES), tab)] * 4
        args += list(rope)
    for gn in gains:
        in_specs.append(pl.BlockSpec((1, _LANES), lambda i, j: (0, 0)))
        args.append(gn)
    return pl.pallas_call(
        functools.partial(_proj_body, layouts, rope is not None, len(gains), nlt),
        grid=(nrt, nj),
        in_specs=in_specs,
        out_specs=pl.BlockSpec((bm, bn), lambda i, j: (i, j)),
        out_shape=jax.ShapeDtypeStruct((t, n), out_dtype),
        scratch_shapes=[pltpu.VMEM((bm, d), _BF)],
        compiler_params=_cp("arbitrary", "arbitrary"),
        name="proj",
    )(*args)


def _wide_tile(n, cap=1536):
    return max(b for b in range(_LANES, cap + 1, _LANES) if n % b == 0)


def _qkv_layouts(n_q, n_kv, bn, rope):
    hd = _LANES
    bounds = [(0, n_q * hd, "q"), (n_q * hd, (n_q + n_kv) * hd, "k"), ((n_q + n_kv) * hd, (n_q + 2 * n_kv) * hd, "v")]
    n = (n_q + 2 * n_kv) * hd
    per_tile = []
    for jt in range(n // bn):
        lo, hi = jt * bn, (jt + 1) * bn
        segs = []
        for b0, b1, nm in bounds:
            s0, s1 = max(lo, b0), min(hi, b1)
            if s0 < s1:
                if nm == "v":
                    segs.append((s0 - lo, s1 - lo, "plain", 0, False, 1.0))
                else:
                    segs.append((s0 - lo, s1 - lo, "head", 0 if nm == "q" else 1, rope, 1.0))
        per_tile.append(tuple(segs))
    layouts = []
    for jt, segs in enumerate(per_tile):
        if layouts and layouts[-1][2] == segs and layouts[-1][1] == jt:
            layouts[-1] = (layouts[-1][0], jt + 1, segs)
        else:
            layouts.append((jt, jt + 1, segs))
    return tuple(layouts)


def _oproj_body(n_lat_tiles, a_ref, w_ref, *refs):
    x_refs, (gt_ref, o_ref) = refs[:-2], refs[-2:]
    x = x_refs[0][...]
    if n_lat_tiles is not None:
        x = jnp.where(pl.program_id(0) < n_lat_tiles, x, x_refs[1][...])
    o_ref[...] = x + gt_ref[0] * _dot(a_ref[...], w_ref[...])


def _oproj(A, W, X, gate, geo, nrt, bm, bn, layer=None):
    k = A.shape[1]
    d = W.shape[-1]
    mrow = functools.partial(geo["mrow_bm"], bm)
    if layer is None:
        w_spec = pl.BlockSpec((k, bn), lambda i, j: (0, j))
    else:
        w_spec = pl.BlockSpec((None, k, bn), lambda i, j: (layer, 0, j))
    if isinstance(X, tuple):
        nlt = X[0].shape[0] // bm
        x_specs = [pl.BlockSpec((bm, bn), lambda i, j: (jnp.minimum(i, nlt - 1), j)),
                   pl.BlockSpec((bm, bn), lambda i, j: (jnp.maximum(i - nlt, 0), j))]
        x_args = list(X)
    else:
        nlt = None
        x_specs = [pl.BlockSpec((bm, bn), lambda i, j: (i, j))]
        x_args = [X]
    return pl.pallas_call(
        functools.partial(_oproj_body, nlt),
        grid=(nrt, d // bn),
        in_specs=[pl.BlockSpec((bm, k), lambda i, j: (i, 0)), w_spec] + x_specs
        + [pl.BlockSpec((1, 1, bn), lambda i, j: (mrow(i), 0, j))],
        out_specs=pl.BlockSpec((bm, bn), lambda i, j: (i, j)),
        out_shape=jax.ShapeDtypeStruct((nrt * bm, d), _F32),
        compiler_params=_cp("arbitrary", "arbitrary"),
        name="oproj",
    )(A, W, *x_args, gate)


def _ffn1_body(bm, n_x_rows, seq, ctx_len, nc, rc, x0_ref, x0p_ref, x0n_ref, xn_ref, xnp_ref, xnn_ref,
               g_ref, sh_ref, sc_ref, shn_ref, scn_ref, wg_ref, wu_ref, cw_ref, cb_ref, o_ref, ha_ref, hb_ref):
    i = pl.program_id(0)
    j = pl.program_id(1)
    h_refs = (ha_ref, hb_ref)

    @pl.when((i == 0) & (j == 0))
    def _():
        g, sh, sc = g_ref[...], sh_ref[0], sc_ref[0]
        ha_ref[0:bm, :] = _norm_mod(x0_ref[...], g, sh, sc).astype(_BF)
        halo = jnp.concatenate([x0p_ref[...], x0n_ref[...]], axis=0)
        ha_ref[bm:bm + 2 * _HALO, :] = _norm_mod(halo, g, sh, sc).astype(_BF)

    for par in (0, 1):
        @pl.when(i % 2 == par)
        def _(par=par):
            h_ref, hn_ref = h_refs[par], h_refs[1 - par]
            gx = _dot(h_ref[...], wg_ref[...])
            u = _dot(h_ref[0:bm, :], wu_ref[...])
            g, shn, scn = g_ref[...], shn_ref[0], scn_ref[0]
            r0 = pl.multiple_of(jnp.minimum(j, nc - 1) * rc, rc)
            hn_ref[pl.ds(r0, rc), :] = _norm_mod(xn_ref[...], g, shn, scn).astype(_BF)
            halo = jnp.concatenate([xnp_ref[...], xnn_ref[...]], axis=0)
            hn_ref[bm:bm + 2 * _HALO, :] = _norm_mod(halo, g, shn, scn).astype(_BF)
            gm = gx[0:bm]
            g_prev = gx[bm + _HALO - 1:bm + _HALO]
            g_next = gx[bm + _HALO:bm + _HALO + 1]
            row = lax.broadcasted_iota(jnp.int32, (bm, 1), 0)
            tok = i * bm + row
            period = jnp.where(i * bm >= n_x_rows, ctx_len, seq)
            up = jnp.where(row == 0, g_prev, pltpu.roll(gm, 1, 0))
            dn = jnp.where(row == bm - 1, g_next, pltpu.roll(gm, bm - 1, 0))
            up = jnp.where((tok & (period - 1)) != 0, up, 0.0)
            dn = jnp.where(((tok + 1) & (period - 1)) != 0, dn, 0.0)
            cw = cw_ref[...]
            gc = cb_ref[...] + up * cw[0:1] + gm * cw[1:2] + dn * cw[2:3]
            o_ref[...] = (jax.nn.gelu(gc) * u).astype(o_ref.dtype)


def _ffn1(X, g, sh, sc, w_in, conv_w, conv_b, layer, geo, nrt, bf):
    t, d = X.shape
    f = conv_w.shape[2]
    bm = geo["bm"]
    mrow = geo["mrow"]
    hb = bm // _HALO
    last = t // _HALO - 1
    nf = f // bf
    nc, rc = _next_rows_chunks(bm, nf)
    nxt = lambda i: jnp.minimum(i + 1, nrt - 1)
    body = functools.partial(_ffn1_body, bm, geo["n_x_rows"], geo["seq"], geo["ctx_len"], nc, rc)
    return pl.pallas_call(
        body,
        grid=(nrt, nf),
        in_specs=[
            pl.BlockSpec((bm, d), lambda i, j: (0, 0)),
            pl.BlockSpec((_HALO, d), lambda i, j: (0, 0)),
            pl.BlockSpec((_HALO, d), lambda i, j: (jnp.minimum(hb, last), 0)),
            pl.BlockSpec((rc, d), lambda i, j: (nxt(i) * nc + jnp.minimum(j, nc - 1), 0)),
            pl.BlockSpec((_HALO, d), lambda i, j: (jnp.maximum(nxt(i) * hb - 1, 0), 0)),
            pl.BlockSpec((_HALO, d), lambda i, j: (jnp.minimum((nxt(i) + 1) * hb, last), 0)),
            pl.BlockSpec((1, d), lambda i, j: (0, 0)),
            pl.BlockSpec((1, 1, d), lambda i, j: (mrow(i), 0, 0)),
            pl.BlockSpec((1, 1, d), lambda i, j: (mrow(i), 0, 0)),
            pl.BlockSpec((1, 1, d), lambda i, j: (mrow(nxt(i)), 0, 0)),
            pl.BlockSpec((1, 1, d), lambda i, j: (mrow(nxt(i)), 0, 0)),
            pl.BlockSpec((None, d, bf), lambda i, j: (layer, 0, j)),
            pl.BlockSpec((None, d, bf), lambda i, j: (layer, 0, j + nf)),
            pl.BlockSpec((None, 3, bf), lambda i, j: (layer, 0, j)),
            pl.BlockSpec((None, 1, bf), lambda i, j: (layer, 0, j)),
        ],
        out_specs=pl.BlockSpec((bm, bf), lambda i, j: (i, j)),
        out_shape=jax.ShapeDtypeStruct((nrt * bm, f), _BF),
        scratch_shapes=[pltpu.VMEM((bm + 2 * _HALO, d), _BF), pltpu.VMEM((bm + 2 * _HALO, d), _BF)],
        compiler_params=_cp("arbitrary", "arbitrary"),
        name="ffn1",
    )(X, X, X, X, X, X, g, sh, sc, sh, sc, w_in, w_in, conv_w, conv_b)


def _with_ones(v):
    return jnp.concatenate([v, jnp.ones(v.shape, v.dtype)], axis=1)


def _lane_tiles(x, n):
    return jnp.concatenate([x] * n, axis=1) if n > 1 else x


def _na_body(seq, sub, kwin, hps, nsub, q_ref, k_ref, v_ref, kc_ref, vc_ref, ba_ref, bm_ref, bb_ref, o_ref):
    j = pl.program_id(2)
    b_refs = (ba_ref,) + (bm_ref,) * (nsub - 2) + (bb_ref,)
    for hh in range(hps):
        c = hh * _LANES
        kc = kc_ref[:, c:c + _LANES]
        vcx = _with_ones(vc_ref[:, c:c + _LANES])
        for i, b_ref in enumerate(b_refs):
            q0 = (nsub * j + i) * sub
            ks = pl.multiple_of(jnp.clip(q0 - (kwin - sub) // 2, 0, seq - kwin), 256)
            q = q_ref[i * sub:(i + 1) * sub, c:c + _LANES]
            k = k_ref[pl.ds(ks, kwin), c:c + _LANES]
            v = v_ref[pl.ds(ks, kwin), c:c + _LANES]
            s = _dot_nt(q, k) + b_ref[0, hh]
            sc = _dot_nt(q, kc)
            m = jnp.maximum(jnp.max(s, axis=-1, keepdims=True), jnp.max(sc, axis=-1, keepdims=True))
            p = jnp.exp2(s - m).astype(_BF)
            pc = jnp.exp2(sc - m).astype(_BF)
            acc = _dot(p, _with_ones(v)) + _dot(pc, vcx)
            o_ref[i * sub:(i + 1) * sub, c:c + _LANES] = (acc[:, :_LANES] / acc[:, _LANES:]).astype(o_ref.dtype)


def _na_bias_table(rel_bias, rows, rq, rk):
    h, nr2, nc2 = rel_bias.shape
    na_rows, na_cols = (nr2 + 1) // 2, (nc2 + 1) // 2
    kr = min(na_rows, rows)
    w = _GRID_W
    nblk = rows // rq
    c = np.arange(w)
    c0 = np.clip(c - na_cols // 2, 0, w - na_cols)
    col_ok = (c[None, :] >= c0[:, None]) & (c[None, :] < c0[:, None] + na_cols)
    rbp = jnp.pad(rel_bias.astype(_F32) * _LOG2E, ((0, 0), (0, 0), (w, w)))
    tcol = jnp.stack([rbp[:, :, na_cols - 1 - qc + w:na_cols - 1 - qc + 2 * w] for qc in range(w)], axis=2)
    tcol = jnp.where(jnp.asarray(col_ok)[None, None], tcol, _NEG)
    tcol = jnp.concatenate([tcol, jnp.full((h, 1, w, w), _NEG, _F32)], axis=1)
    sel = np.zeros((3, rq, rk, nr2 + 1), np.float32)
    for ti, jb in enumerate((0, min(1, nblk - 1), nblk - 1)):
        kb0 = int(np.clip(jb * rq - (rk - rq) // 2, 0, rows - rk))
        for qi in range(rq):
            r = jb * rq + qi
            r0 = int(np.clip(r - kr // 2, 0, rows - kr))
            for ki in range(rk):
                krow = kb0 + ki
                sel[ti, qi, ki, krow - r + na_rows - 1 if r0 <= krow < r0 + kr else nr2] = 1.0
    tab = jnp.einsum("tqkd,hdcx->thqckx", jnp.asarray(sel), tcol, precision=lax.Precision.HIGHEST)
    return tab.reshape(3, h, rq * w, rk * w)


def _na_attention(QKV, bias_tab, geo, n_heads):
    b, s, lc = geo["batch"], geo["seq"], geo["ctx_len"]
    t = QKV.shape[0]
    sub, kwin = bias_tab.shape[2], bias_tab.shape[3]
    nsub = min(8, s // sub)
    bq = nsub * sub
    nj = s // bq
    cblk = geo["n_x_rows"] // lc
    hps = 2
    hw = hps * _LANES
    ng = n_heads // hps
    return pl.pallas_call(
        functools.partial(_na_body, s, sub, kwin, hps, nsub),
        grid=(b, ng, nj),
        in_specs=[
            pl.BlockSpec((bq, hw), lambda bi, h, j: (bi * nj + j, h)),
            pl.BlockSpec((s, hw), lambda bi, h, j: (bi, ng + h)),
            pl.BlockSpec((s, hw), lambda bi, h, j: (bi, 2 * ng + h)),
            pl.BlockSpec((lc, hw), lambda bi, h, j: (cblk + bi, ng + h)),
            pl.BlockSpec((lc, hw), lambda bi, h, j: (cblk + bi, 2 * ng + h)),
            pl.BlockSpec((1, hps, sub, kwin), lambda bi, h, j: (jnp.where(j == 0, 0, 1), h, 0, 0)),
            pl.BlockSpec((1, hps, sub, kwin), lambda bi, h, j: (1, h, 0, 0)),
            pl.BlockSpec((1, hps, sub, kwin), lambda bi, h, j: (jnp.where(j == nj - 1, 2, 1), h, 0, 0)),
        ],
        out_specs=pl.BlockSpec((bq, hw), lambda bi, h, j: (bi * nj + j, h)),
        out_shape=jax.ShapeDtypeStruct((t, n_heads * _LANES), _BF),
        compiler_params=_cp("arbitrary", "arbitrary", "arbitrary"),
        name="na_attn",
    )(QKV, QKV, QKV, QKV, QKV, bias_tab, bias_tab, bias_tab)


def _ctx_attn_body(group, has_sink, *refs):
    if has_sink:
        q_ref, k_ref, v_ref, sk_ref, _, o_ref = refs
    else:
        q_ref, k_ref, v_ref, _, o_ref = refs
    k = k_ref[...]
    vx = _with_ones(v_ref[...])
    for g in range(group):
        c = g * _LANES
        s = _dot_nt(q_ref[:, c:c + _LANES], k)
        m = jnp.max(s, axis=-1, keepdims=True)
        if has_sink:
            sk = sk_ref[0][:, c:c + 1]
            m = jnp.maximum(m, sk)
        acc = _dot(jnp.exp2(s - m).astype(_BF), vx)
        l = acc[:, _LANES:]
        if has_sink:
            l = l + jnp.exp2(sk - m)
        o_ref[:, c:c + _LANES] = (acc[:, :_LANES] / l).astype(o_ref.dtype)


def _ctx_attention(QKV, O, geo, n_heads, n_kv, sink=None):
    b, lc = geo["batch"], geo["ctx_len"]
    group = n_heads // n_kv
    gw = group * _LANES
    cblk = geo["n_x_rows"] // lc
    in_specs = [
        pl.BlockSpec((lc, gw), lambda bi, n: (cblk + bi, n)),
        pl.BlockSpec((lc, _LANES), lambda bi, n: (cblk + bi, n_heads + n)),
        pl.BlockSpec((lc, _LANES), lambda bi, n: (cblk + bi, n_heads + n_kv + n)),
    ]
    args = [QKV, QKV, QKV]
    if sink is not None:
        in_specs.append(pl.BlockSpec((1, 1, gw), lambda bi, n: (n, 0, 0)))
        args.append(sink)
    in_specs.append(pl.BlockSpec(memory_space=pl.ANY))
    args.append(O)
    return pl.pallas_call(
        functools.partial(_ctx_attn_body, group, sink is not None),
        grid=(b, n_kv),
        in_specs=in_specs,
        out_specs=pl.BlockSpec((lc, gw), lambda bi, n: (cblk + bi, n)),
        out_shape=jax.ShapeDtypeStruct(O.shape, O.dtype),
        input_output_aliases={len(args) - 1: 0},
        compiler_params=_cp("arbitrary", "arbitrary"),
        name="ctx_attn",
    )(*args)


def _swa_body(seq, bq, nblk, win, group, q_ref, k_ref, v_ref, kc_ref, vc_ref, sk_ref, o_ref):
    kw = bq + 2 * win
    kc = kc_ref[...]
    vcx = _with_ones(vc_ref[...])
    for bi in range(nblk):
        t = pl.program_id(2) * nblk + bi
        rows = slice(bi * bq, (bi + 1) * bq)
        ks = pl.multiple_of(jnp.clip(t * bq - win, 0, seq - kw), _LANES)
        k = k_ref[pl.ds(ks, kw), :]
        vx = _with_ones(v_ref[pl.ds(ks, kw), :])
        qpos = t * bq + lax.broadcasted_iota(jnp.int32, (bq, 1), 0)
        kpos = ks + lax.broadcasted_iota(jnp.int32, (1, kw), 1)
        band = jnp.abs(kpos - qpos) <= win
        for g in range(group):
            c = g * _LANES
            q = q_ref[rows, c:c + _LANES]
            s = jnp.where(band, _dot_nt(q, k), _NEG)
            sc = _dot_nt(q, kc)
            sk = sk_ref[0][:, c:c + 1]
            m = jnp.maximum(jnp.maximum(jnp.max(s, axis=-1, keepdims=True), jnp.max(sc, axis=-1, keepdims=True)), sk)
            p = jnp.exp2(s - m).astype(_BF)
            pc = jnp.exp2(sc - m).astype(_BF)
            acc = _dot(p, vx) + _dot(pc, vcx)
            l = acc[:, _LANES:] + jnp.exp2(sk - m)
            o_ref[rows, c:c + _LANES] = (acc[:, :_LANES] / l).astype(o_ref.dtype)


def _swa_attention(QKV, sink, geo, n_heads, n_kv, bq):
    b, s, lc = geo["batch"], geo["seq"], geo["ctx_len"]
    t = QKV.shape[0]
    group = n_heads // n_kv
    gw = group * _LANES
    nblk = 2
    bs = nblk * bq
    nq = s // bs
    cblk = geo["n_x_rows"] // lc
    return pl.pallas_call(
        functools.partial(_swa_body, s, bq, nblk, _SWA_WINDOW, group),
        grid=(b, n_kv, nq),
        in_specs=[
            pl.BlockSpec((bs, gw), lambda bi, n, j: (bi * nq + j, n)),
            pl.BlockSpec((s, _LANES), lambda bi, n, j: (bi, n_heads + n)),
            pl.BlockSpec((s, _LANES), lambda bi, n, j: (bi, n_heads + n_kv + n)),
            pl.BlockSpec((lc, _LANES), lambda bi, n, j: (cblk + bi, n_heads + n)),
            pl.BlockSpec((lc, _LANES), lambda bi, n, j: (cblk + bi, n_heads + n_kv + n)),
            pl.BlockSpec((1, 1, gw), lambda bi, n, j: (n, 0, 0)),
        ],
        out_specs=pl.BlockSpec((bs, gw), lambda bi, n, j: (bi * nq + j, n)),
        out_shape=jax.ShapeDtypeStruct((t, n_heads * _LANES), _BF),
        compiler_params=_cp("arbitrary", "arbitrary", "arbitrary"),
        name="swa_attn",
    )(QKV, QKV, QKV, QKV, QKV, sink)


def _gqa_body(seq, ck, group, q_ref, k_ref, v_ref, kc_ref, vc_ref, o_ref, m_ref, acc_ref):
    m_ref[...] = jnp.full(m_ref.shape, _NEG, _F32)
    acc_ref[...] = jnp.zeros(acc_ref.shape, _F32)

    def step(k, v):
        width = k.shape[0]
        vx = jnp.concatenate([v, jnp.ones((width, _LANES), v.dtype)], axis=1)
        for g in range(group):
            s = _dot_nt(q_ref[:, g * _LANES:(g + 1) * _LANES], k)
            m_old = m_ref[g]
            m_new = jnp.maximum(m_old, jnp.max(s, axis=-1, keepdims=True))
            alpha = jnp.exp2(m_old - m_new)
            p = jnp.exp2(s - _lane_tiles(m_new, width // _LANES)).astype(_BF)
            acc_ref[g] = _lane_tiles(alpha, 2) * acc_ref[g] + _dot(p, vx)
            m_ref[g] = m_new

    def chunk(c, carry):
        off = pl.multiple_of(c * ck, ck)
        step(k_ref[pl.ds(off, ck), :], v_ref[pl.ds(off, ck), :])
        return carry

    lax.fori_loop(0, seq // ck, chunk, 0, unroll=min(8, seq // ck))
    step(kc_ref[...], vc_ref[...])
    for g in range(group):
        acc = acc_ref[g]
        o_ref[:, g * _LANES:(g + 1) * _LANES] = (acc[:, :_LANES] / acc[:, _LANES:]).astype(o_ref.dtype)


def _gqa_attention(QKV, geo, n_heads, n_kv, bq, ck):
    b, s, lc = geo["batch"], geo["seq"], geo["ctx_len"]
    group = n_heads // n_kv
    gw = group * _LANES
    nq = s // bq
    cblk = geo["n_x_rows"] // lc
    return pl.pallas_call(
        functools.partial(_gqa_body, s, ck, group),
        grid=(b, n_kv, nq),
        in_specs=[
            pl.BlockSpec((bq, gw), lambda bi, n, j: (bi * nq + j, n)),
            pl.BlockSpec((s, _LANES), lambda bi, n, j: (bi, n_heads + n)),
            pl.BlockSpec((s, _LANES), lambda bi, n, j: (bi, n_heads + n_kv + n)),
            pl.BlockSpec((lc, _LANES), lambda bi, n, j: (cblk + bi, n_heads + n)),
            pl.BlockSpec((lc, _LANES), lambda bi, n, j: (cblk + bi, n_heads + n_kv + n)),
        ],
        out_specs=pl.BlockSpec((bq, gw), lambda bi, n, j: (bi * nq + j, n)),
        out_shape=jax.ShapeDtypeStruct((geo["n_x_rows"], n_heads * _LANES), _BF),
        scratch_shapes=[pltpu.VMEM((group, bq, _LANES), _F32), pltpu.VMEM((group, bq, 2 * _LANES), _F32)],
        compiler_params=_cp("arbitrary", "arbitrary", "arbitrary"),
        name="gqa_attn",
    )(QKV, QKV, QKV, QKV, QKV)


def _log_sigmoid(x):
    return jnp.minimum(x, 0.0) - jnp.log1p(jnp.exp(-jnp.abs(x)))


def _mlstm_chunk(d, st, n_heads, L, h, q_ref, k_ref, v_ref, gt_ref, gb_ref, o_ref, c_ref, n_ref, m_ref):
    gates = gt_ref[...] + gb_ref[...]
    lane = lax.broadcasted_iota(jnp.int32, (1, _LANES), 1)
    li_col = jnp.sum(jnp.where(lane == (2 * d) * n_heads + h, gates, 0.0), axis=-1, keepdims=True)
    lf_pre = jnp.sum(jnp.where(lane == (2 * d + 1) * n_heads + h, gates, 0.0), axis=-1, keepdims=True)
    lf_col = _log_sigmoid(lf_pre)
    ti = lax.broadcasted_iota(jnp.int32, (L, L), 0)
    si = lax.broadcasted_iota(jnp.int32, (L, L), 1)
    eye = ti == si
    li_row = jnp.sum(jnp.where(eye, li_col, 0.0), axis=0, keepdims=True)
    lf_row = jnp.sum(jnp.where(eye, lf_col, 0.0), axis=0, keepdims=True)
    allowed = (ti >= si) if d == 0 else (ti <= si)
    allowed_t = (si >= ti) if d == 0 else (si <= ti)
    b_col = jnp.sum(jnp.where(allowed, lf_row, 0.0), axis=-1, keepdims=True)
    b_row = jnp.sum(jnp.where(allowed_t, lf_col, 0.0), axis=0, keepdims=True)
    total = jnp.sum(lf_col, axis=0, keepdims=True)
    m_prev = m_ref[st, 0:1, 0:1]

    q = q_ref[...]
    k = k_ref[...]
    v = v_ref[...]
    dmat = jnp.where(allowed, b_col - b_row + li_row, _NEG)
    g_col = b_col + m_prev
    m_t = jnp.maximum(g_col, jnp.max(dmat, axis=-1, keepdims=True))
    w = jnp.exp(dmat - m_t) * _dot_nt(q, k)
    w_prev = jnp.exp(g_col - m_t)
    cmat = c_ref[st]
    nvec = n_ref[st, 0:1, :]
    num = _dot(w.astype(_BF), v) + w_prev * _dot(q, cmat.astype(_BF))
    qn = jnp.sum(q.astype(_F32) * nvec, axis=-1, keepdims=True)
    den = jnp.sum(w, axis=-1, keepdims=True) + w_prev * qn
    o_ref[...] = (num / jnp.maximum(jnp.abs(den), jnp.exp(-m_t))).astype(o_ref.dtype)

    lw = total - b_col + li_col
    m_new = jnp.maximum(total + m_prev, jnp.max(lw, axis=0, keepdims=True))
    decay = jnp.exp(total + m_prev - m_new)
    kw = k.astype(_F32) * jnp.exp(lw - m_new)
    c_ref[st] = decay * cmat + _dot_tn(kw.astype(_BF), v)
    n_ref[st] = jnp.broadcast_to(decay * nvec + jnp.sum(kw, axis=0, keepdims=True), n_ref.shape[1:])
    m_ref[st] = jnp.broadcast_to(m_new, m_ref.shape[1:])


def _mlstm_body(n_heads, hps, dqk, dv, chunk, qf_ref, kf_ref, vf_ref, gf_ref, qb_ref, kb_ref, vb_ref, gbk_ref,
                gb_ref, of_ref, ob_ref, c_ref, n_ref, m_ref):
    hg = pl.program_id(1)

    @pl.when(pl.program_id(2) == 0)
    def _():
        c_ref[...] = jnp.zeros(c_ref.shape, _F32)
        n_ref[...] = jnp.zeros(n_ref.shape, _F32)
        m_ref[...] = jnp.zeros(m_ref.shape, _F32)

    dirs = ((qf_ref, kf_ref, vf_ref, gf_ref, of_ref), (qb_ref, kb_ref, vb_ref, gbk_ref, ob_ref))
    for hh in range(hps):
        qs, vs = slice(hh * dqk, (hh + 1) * dqk), slice(hh * dv, (hh + 1) * dv)
        for d, (q_ref, k_ref, v_ref, g_ref, o_ref) in enumerate(dirs):
            _mlstm_chunk(d, d * hps + hh, n_heads, chunk, hg * hps + hh, q_ref.at[:, qs], k_ref.at[:, qs],
                         v_ref.at[:, vs], g_ref, gb_ref, o_ref.at[:, vs], c_ref, n_ref, m_ref)


def _mlstm_scan(P, G, gate_b, geo, n_heads, dqk, dv, chunk):
    b, s, lc = geo["batch"], geo["seq"], geo["ctx_len"]
    t = P.shape[0]
    nctx, nx = lc // chunk, s // chunk
    cbase = geo["n_x_rows"] // chunk

    def rblk(d, bi, c):
        cc = c - nctx
        in_ctx = cbase + bi * nctx + (c if d == 0 else nctx - 1 - c)
        in_x = bi * nx + (cc if d == 0 else nx - 1 - cc)
        return jnp.where(c < nctx, in_ctx, in_x)

    hps = 4
    ng = n_heads // hps
    k_off = ng
    v_off = 2 * n_heads * dqk // (hps * dv)
    in_specs, args = [], []
    for d in (0, 1):
        in_specs += [
            pl.BlockSpec((chunk, hps * dqk), lambda bi, h, c, d=d: (rblk(d, bi, c), h)),
            pl.BlockSpec((chunk, hps * dqk), lambda bi, h, c, d=d: (rblk(d, bi, c), k_off + h)),
            pl.BlockSpec((chunk, hps * dv), lambda bi, h, c, d=d: (rblk(d, bi, c), v_off + h)),
            pl.BlockSpec((chunk, _LANES), lambda bi, h, c, d=d: (rblk(d, bi, c), 0)),
        ]
        args += [P, P, P, G]
    in_specs.append(pl.BlockSpec((1, _LANES), lambda bi, h, c: (0, 0)))
    args.append(gate_b)
    out_sds = jax.ShapeDtypeStruct((t, n_heads * dv), _BF)
    ns = 2 * hps
    return pl.pallas_call(
        functools.partial(_mlstm_body, n_heads, hps, dqk, dv, chunk),
        grid=(b, ng, nctx + nx),
        in_specs=in_specs,
        out_specs=[pl.BlockSpec((chunk, hps * dv), lambda bi, h, c, d=d: (rblk(d, bi, c), h)) for d in (0, 1)],
        out_shape=[out_sds, out_sds],
        scratch_shapes=[pltpu.VMEM((ns, dqk, dv), _F32), pltpu.VMEM((ns, 8, dqk), _F32), pltpu.VMEM((ns, 8, _LANES), _F32)],
        compiler_params=_cp("arbitrary", "arbitrary", "arbitrary"),
        name="mlstm_scan",
    )(*args)


def _ml_out_body(n_heads, dv, hf_ref, hb_ref, og_ref, hg_ref, w_ref, x_ref, gt_ref, o_ref):
    hsum = hf_ref[...].astype(_F32) + hb_ref[...].astype(_F32)
    parts = []
    for hh in range(n_heads):
        a = hsum[:, hh * dv:(hh + 1) * dv]
        hn = a * lax.rsqrt(jnp.mean(a * a, axis=-1, keepdims=True) + _EPS) * hg_ref[:, hh * dv:(hh + 1) * dv]
        og = og_ref[:, hh * dv:(hh + 1) * dv].astype(_F32)
        parts.append((jax.nn.sigmoid(og) * hn).astype(_BF))
    a = jnp.concatenate(parts, axis=-1)
    o_ref[...] = x_ref[...] + gt_ref[0] * _dot(a, w_ref[...])


def _ml_out(Hf, Hb, P, head_g, W, X, gate, geo, n_heads, dv, bm):
    t, d = X.shape
    kd = n_heads * dv
    o_off = (P.shape[1] - kd) // kd
    mrow = functools.partial(geo["mrow_bm"], bm)
    return pl.pallas_call(
        functools.partial(_ml_out_body, n_heads, dv),
        grid=(t // bm,),
        in_specs=[
            pl.BlockSpec((bm, kd), lambda i: (i, 0)),
            pl.BlockSpec((bm, kd), lambda i: (i, 0)),
            pl.BlockSpec((bm, kd), lambda i: (i, o_off)),
            pl.BlockSpec((1, kd), lambda i: (0, 0)),
            pl.BlockSpec((kd, d), lambda i: (0, 0)),
            pl.BlockSpec((bm, d), lambda i: (i, 0)),
            pl.BlockSpec((1, 1, d), lambda i: (mrow(i), 0, 0)),
        ],
        out_specs=pl.BlockSpec((bm, d), lambda i: (i, 0)),
        out_shape=jax.ShapeDtypeStruct((t, d), _F32),
        compiler_params=_cp("arbitrary"),
        name="ml_out",
    )(Hf, Hb, P, head_g, W, X, gate)


def _rope_tables(seq, pad_rows):
    tpos = np.arange(seq)
    row = (tpos // _GRID_W).astype(np.float32)
    col = (tpos % _GRID_W).astype(np.float32)
    nf = _LANES // 4
    inv = (_ROPE_BASE ** (-jnp.arange(nf, dtype=_F32) / nf))
    ar = jnp.asarray(row)[:, None] * inv
    ac = jnp.asarray(col)[:, None] * inv
    cr, sr, cc, sc = jnp.cos(ar), jnp.sin(ar), jnp.cos(ac), jnp.sin(ac)
    cs = jnp.concatenate([cr, cc, cr, cc], axis=1)
    sn = jnp.concatenate([-sr, -sc, sr, sc], axis=1)
    cs = jnp.concatenate([cs, jnp.ones((pad_rows, _LANES), _F32)], axis=0)
    sn = jnp.concatenate([sn, jnp.zeros((pad_rows, _LANES), _F32)], axis=0)
    return cs, sn


def _pair_major(v):
    q = _LANES // 4
    return v.reshape(v.shape[:-1] + (v.shape[-1] // _LANES, 2, 2, q)).swapaxes(-3, -2).reshape(v.shape)


def _rope_operands(w_qkv, q_gain, k_gain, n_qk_heads, cs, sn):
    nqk = n_qk_heads * _LANES
    w = jnp.concatenate([_pair_major(w_qkv[:, :nqk]), w_qkv[:, nqk:]], axis=1).astype(_BF)
    tabs = []
    for gain in (q_gain, k_gain):
        gp = _pair_major(gain.astype(_F32))
        tabs += [cs * gp[None, :], sn * jnp.roll(gp, _LANES // 2)[None, :]]
    return w, tuple(tabs)


def _geometry(batch, seq, ctx_len):
    bm = min(1024, batch * ctx_len)
    assert seq % bm == 0 and (batch * ctx_len) % bm == 0
    assert seq & (seq - 1) == 0 and ctx_len & (ctx_len - 1) == 0
    n_x_rows = batch * seq

    def mrow_bm(bm_, i):
        return jnp.minimum((i * bm_) // seq, batch)

    return dict(batch=batch, seq=seq, ctx_len=ctx_len, bm=bm, n_x_rows=n_x_rows,
                n_x_tiles=n_x_rows // bm, tiles_per_seq=seq // bm,
                mrow=functools.partial(mrow_bm, bm), mrow_bm=mrow_bm)


def kernel(x, c, ctx, c_ctx, ada_w, ada_b, norm1_g, norm2_g, ffn_w_in, ffn_conv_w, ffn_conv_b, ffn_w_out,
           na_w_qkv, na_q_g, na_k_g, na_rel_bias, na_w_o,
           swa_w_qkv, swa_q_g, swa_k_g, swa_sinks, swa_w_o,
           ml_w_in, ml_gate_b, ml_head_g, ml_w_o,
           gqa_w_qkv, gqa_q_g, gqa_k_g, gqa_w_o):
    batch, seq, d = x.shape
    ctx_len = ctx.shape[1]
    depth = ada_w.shape[0]
    n_heads = d // _LANES
    geo = _geometry(batch, seq, ctx_len)
    bm = geo["bm"]
    n_x_rows = geo["n_x_rows"]
    n_x_tiles = geo["n_x_tiles"]
    qscale = (_LANES ** -0.5) * _LOG2E

    X = (x.reshape(n_x_rows, d), ctx.reshape(batch * ctx_len, d))
    t_all = n_x_rows + batch * ctx_len
    nrt_all = t_all // bm

    cond = jnp.concatenate([c, c_ctx[None, :], jnp.zeros((8 - batch - 1, d), _F32)], axis=0)
    mods = _adaln(cond, ada_w, ada_b)

    cs, sn = _rope_tables(seq, bm)
    f = ffn_conv_w.shape[2]
    w_in_b = ffn_w_in.astype(_BF)
    w_out_b = ffn_w_out.astype(_BF)
    conv_b3 = ffn_conv_b.reshape(depth, 1, f)
    bf = 512
    bm_o = min(512, bm)

    for i in range(depth):
        kind, jl = i % 4, i // 4
        need_ctx = i < depth - 1
        mod = [mods[i, :batch + 1, k * d:(k + 1) * d].reshape(batch + 1, 1, d) for k in range(6)]
        sh1, sc1, g1, sh2, sc2, g2 = mod
        n1 = norm1_g[i].reshape(1, d)
        n2 = norm2_g[i].reshape(1, d)
        nrt_o = t_all // bm_o if need_ctx else n_x_rows // bm_o

        if kind == 0:
            w = na_w_qkv[jl].astype(_BF)
            gq = (na_q_g[jl] * qscale).reshape(1, _LANES)
            gk = na_k_g[jl].reshape(1, _LANES)
            bn = _wide_tile(w.shape[1])
            QKV = _proj(X, n1, sh1, sc1, w, _qkv_layouts(n_heads, n_heads, bn, False), geo, bn, _BF, gains=(gq, gk))
            rows = seq // _GRID_W
            tab = _na_bias_table(na_rel_bias[jl], rows, 4, 12)
            O = _na_attention(QKV, tab, geo, n_heads)
            if need_ctx:
                O = _ctx_attention(QKV, O, geo, n_heads, n_heads)
            X = _oproj(O, na_w_o[jl].astype(_BF), X, g1, geo, nrt_o, bm_o, d)
        elif kind == 1:
            n_kv = (swa_w_qkv.shape[2] // _LANES - n_heads) // 2
            w, tabs = _rope_operands(swa_w_qkv[jl], swa_q_g[jl] * qscale, swa_k_g[jl], n_heads + n_kv, cs, sn)
            bn = _wide_tile(w.shape[1])
            QKV = _proj(X, n1, sh1, sc1, w, _qkv_layouts(n_heads, n_kv, bn, True), geo, bn, _BF, rope=tabs)
            group = n_heads // n_kv
            sink = jnp.repeat(swa_sinks[jl].astype(_F32) * _LOG2E, _LANES).reshape(n_kv, 1, group * _LANES)
            O = _swa_attention(QKV, sink, geo, n_heads, n_kv, 256)
            if need_ctx:
                O = _ctx_attention(QKV, O, geo, n_heads, n_kv, sink)
            X = _oproj(O, swa_w_o[jl].astype(_BF), X, g1, geo, nrt_o, bm_o, d)
        elif kind == 2:
            mh = ml_gate_b.shape[1] // 4
            dv = d // mh
            dqk = dv // 2
            nmain = 2 * mh * dqk + 2 * mh * dv
            w_main = ml_w_in[jl][:, :nmain].astype(_BF)
            w_gate = jnp.pad(ml_w_in[jl][:, nmain:], ((0, 0), (0, _LANES - 4 * mh))).astype(_BF)
            bn = _wide_tile(nmain)
            segs_by_tile = []
            k0, k1 = mh * dqk, 2 * mh * dqk
            for jt in range(nmain // bn):
                lo, hi = jt * bn, (jt + 1) * bn
                cuts = sorted({lo, hi, min(max(k0, lo), hi), min(max(k1, lo), hi)})
                segs = tuple((a - lo, b_ - lo, "plain", 0, False, dqk ** -0.5 if k0 <= a < k1 else 1.0)
                             for a, b_ in zip(cuts[:-1], cuts[1:]))
                segs_by_tile.append((jt, jt + 1, segs))
            P = _proj(X, n1, sh1, sc1, w_main, tuple(segs_by_tile), geo, bn, _BF)
            G = _proj(X, n1, sh1, sc1, w_gate, ((0, 1, ((0, _LANES, "plain", 0, False, 1.0),)),), geo, _LANES, _F32)
            gb = jnp.pad(ml_gate_b[jl].astype(_F32), (0, _LANES - 4 * mh)).reshape(1, _LANES)
            Hf, Hb = _mlstm_scan(P, G, gb, geo, mh, dqk, dv, 256)
            X = _ml_out(Hf, Hb, P, ml_head_g[jl].reshape(1, mh * dv), ml_w_o[jl].astype(_BF), X, g1, geo, mh, dv, bm_o)
        else:
            n_kv = (gqa_w_qkv.shape[2] // _LANES - n_heads) // 2
            w, tabs = _rope_operands(gqa_w_qkv[jl], gqa_q_g[jl] * qscale, gqa_k_g[jl], n_heads + n_kv, cs, sn)
            bn = _wide_tile(w.shape[1])
            QKV = _proj(X, n1, sh1, sc1, w, _qkv_layouts(n_heads, n_kv, bn, True), geo, bn, _BF, rope=tabs)
            O = _gqa_attention(QKV, geo, n_heads, n_kv, min(1024, seq), 512)
            if need_ctx:
                Oc = jnp.zeros((t_all, d), _BF).at[:n_x_rows].set(O)
                O = _ctx_attention(QKV, Oc, geo, n_heads, n_kv)
            X = _oproj(O, gqa_w_o[jl].astype(_BF), X, g1, geo, nrt_o, bm_o, d)

        nrt_f = nrt_all if need_ctx else n_x_tiles
        A = _ffn1(X, n2, sh2, sc2, w_in_b, ffn_conv_w, conv_b3, i, geo, nrt_f, bf)
        X = _oproj(A, w_out_b, X, g2, geo, A.shape[0] // bm, bm, 512, layer=i)

    return X[:n_x_rows].reshape(batch, seq, d)
```

```python
import functools

import numpy as np
import jax
import jax.numpy as jnp
from jax import lax
from jax.experimental import pallas as pl
from jax.experimental.pallas import tpu as pltpu

_F32 = jnp.float32
_BF = jnp.bfloat16
_EPS = 1e-6
_NEG = -1e30
_LOG2E = 1.4426950408889634
_GRID_W = 64
_SWA_WINDOW = 128
_ROPE_BASE = 10000.0
_LANES = 128
_HALO = 8
_VMEM_LIMIT = 56 << 20


def _cp(*sem):
    return pltpu.CompilerParams(dimension_semantics=sem, vmem_limit_bytes=_VMEM_LIMIT)


def _dot(a, b):
    return jnp.dot(a, b, preferred_element_type=_F32)


def _dot_nt(a, b):
    return lax.dot_general(a, b, (((1,), (1,)), ((), ())), preferred_element_type=_F32)


def _dot_tn(a, b):
    return lax.dot_general(a, b, (((0,), (0,)), ((), ())), preferred_element_type=_F32)


def _norm_mod(x, g, sh, sc):
    ms = jnp.mean(x * x, axis=-1, keepdims=True)
    return (x * lax.rsqrt(ms + _EPS)) * (g * (1.0 + sc)) + sh


def _adaln_body(c_ref, w_ref, b_ref, o_ref):
    c = c_ref[...]
    s = (c * jax.nn.sigmoid(c)).astype(_BF)
    o_ref[0] = _dot(s, w_ref[0].astype(_BF)) + b_ref[0]


def _adaln(cond, ada_w, ada_b):
    depth, d, n = ada_w.shape
    bn = 1024
    return pl.pallas_call(
        _adaln_body,
        grid=(depth, n // bn),
        in_specs=[
            pl.BlockSpec((8, d), lambda l, j: (0, 0)),
            pl.BlockSpec((1, d, bn), lambda l, j: (l, 0, j)),
            pl.BlockSpec((1, 1, bn), lambda l, j: (l, 0, j)),
        ],
        out_specs=pl.BlockSpec((1, 8, bn), lambda l, j: (l, 0, j)),
        out_shape=jax.ShapeDtypeStruct((depth, 8, n), _F32),
        compiler_params=_cp("arbitrary", "arbitrary"),
        name="adaln",
    )(cond, ada_w, ada_b.reshape(depth, 1, n))


def _next_rows_chunks(bm, nj):
    nc = 1
    while nc * 2 <= nj and bm // (nc * 2) >= _LANES:
        nc *= 2
    return nc, bm // nc


def _proj_body(layouts, has_rope, n_gain, n_lat_tiles, *refs):
    x_refs = refs[:1] if n_lat_tiles is None else refs[:2]
    refs = refs[len(x_refs):]
    g_ref, sh_ref, sc_ref, w_ref = refs[:4]
    pos = 4
    tab_refs = ()
    if has_rope:
        tab_refs = refs[pos:pos + 4]
        pos += 4
    gain_refs = refs[pos:pos + n_gain]
    pos += n_gain
    o_ref, h_ref = refs[pos], refs[pos + 1]
    j = pl.program_id(1)

    is_lat = True if n_lat_tiles is None else pl.program_id(0) < n_lat_tiles
    for src_ref, mine in zip(x_refs, (is_lat, jnp.logical_not(is_lat))):
        @pl.when((j == 0) & mine)
        def _(src_ref=src_ref):
            h_ref[...] = _norm_mod(src_ref[...], g_ref[...], sh_ref[0], sc_ref[0]).astype(_BF)

    for lo, hi, segs in layouts:
        @pl.when((j >= lo) & (j < hi))
        def _(segs=segs):
            acc = _dot(h_ref[...], w_ref[...])
            for c0, c1, kind, gi, rope, mult in segs:
                if kind == "plain":
                    a = acc[:, c0:c1]
                    if mult != 1.0:
                        a = a * mult
                    o_ref[:, c0:c1] = a.astype(o_ref.dtype)
                    continue
                for c in range(c0, c1, _LANES):
                    a = acc[:, c:c + _LANES]
                    inv = lax.rsqrt(jnp.mean(a * a, axis=-1, keepdims=True) + _EPS)
                    if rope:
                        y = a * tab_refs[2 * gi][...] + pltpu.roll(a, _LANES // 2, 1) * tab_refs[2 * gi + 1][...]
                    else:
                        y = a * gain_refs[gi][...]
                    o_ref[:, c:c + _LANES] = (y * inv).astype(o_ref.dtype)


def _proj(X, g, sh, sc, W, layouts, geo, bn, out_dtype, rope=None, gains=()):
    n = W.shape[1]
    bm = geo["bm"]
    nj = n // bn
    mrow = geo["mrow"]
    if isinstance(X, tuple):
        nlt = X[0].shape[0] // bm
        t, d = X[0].shape[0] + X[1].shape[0], X[0].shape[1]
        x_specs = [pl.BlockSpec((bm, d), lambda i, j: (jnp.minimum(i, nlt - 1), 0)),
                   pl.BlockSpec((bm, d), lambda i, j: (jnp.maximum(i - nlt, 0), 0),
                                pipeline_mode=pl.Buffered(1))]
        x_args = list(X)
    else:
        nlt = None
        t, d = X.shape
        x_specs = [pl.BlockSpec((bm, d), lambda i, j: (i, 0))]
        x_args = [X]
    nrt = t // bm
    in_specs = x_specs + [
        pl.BlockSpec((1, d), lambda i, j: (0, 0)),
        pl.BlockSpec((1, 1, d), lambda i, j: (mrow(i), 0, 0)),
        pl.BlockSpec((1, 1, d), lambda i, j: (mrow(i), 0, 0)),
        pl.BlockSpec((d, bn), lambda i, j: (0, j)),
    ]
    args = x_args + [g, sh, sc, W]
    if rope is not None:
        tps, nxt = geo["tiles_per_seq"], geo["n_x_tiles"]
        tab = lambda i, j: (jnp.where(i < nxt, i % tps, tps), 0)
        in_specs += [pl.BlockSpec((bm, _LANES), tab)] * 4
        args += list(rope)
    for gn in gains:
        in_specs.append(pl.BlockSpec((1, _LANES), lambda i, j: (0, 0)))
        args.append(gn)
    return pl.pallas_call(
        functools.partial(_proj_body, layouts, rope is not None, len(gains), nlt),
        grid=(nrt, nj),
        in_specs=in_specs,
        out_specs=pl.BlockSpec((bm, bn), lambda i, j: (i, j)),
        out_shape=jax.ShapeDtypeStruct((t, n), out_dtype),
        scratch_shapes=[pltpu.VMEM((bm, d), _BF)],
        compiler_params=_cp("arbitrary", "arbitrary"),
        name="proj",
    )(*args)


def _wide_tile(n, cap=1536):
    return max(b for b in range(_LANES, cap + 1, _LANES) if n % b == 0)


def _qkv_layouts(n_q, n_kv, bn, rope):
    hd = _LANES
    bounds = [(0, n_q * hd, "q"), (n_q * hd, (n_q + n_kv) * hd, "k"), ((n_q + n_kv) * hd, (n_q + 2 * n_kv) * hd, "v")]
    n = (n_q + 2 * n_kv) * hd
    per_tile = []
    for jt in range(n // bn):
        lo, hi = jt * bn, (jt + 1) * bn
        segs = []
        for b0, b1, nm in bounds:
            s0, s1 = max(lo, b0), min(hi, b1)
            if s0 < s1:
                if nm == "v":
                    segs.append((s0 - lo, s1 - lo, "plain", 0, False, 1.0))
                else:
                    segs.append((s0 - lo, s1 - lo, "head", 0 if nm == "q" else 1, rope, 1.0))
        per_tile.append(tuple(segs))
    layouts = []
    for jt, segs in enumerate(per_tile):
        if layouts and layouts[-1][2] == segs and layouts[-1][1] == jt:
            layouts[-1] = (layouts[-1][0], jt + 1, segs)
        else:
            layouts.append((jt, jt + 1, segs))
    return tuple(layouts)


def _oproj_body(n_lat_tiles, a_ref, w_ref, *refs):
    x_refs, (gt_ref, o_ref) = refs[:-2], refs[-2:]
    x = x_refs[0][...]
    if n_lat_tiles is not None:
        x = jnp.where(pl.program_id(0) < n_lat_tiles, x, x_refs[1][...])
    o_ref[...] = x + gt_ref[0] * _dot(a_ref[...], w_ref[...])


def _oproj(A, W, X, gate, geo, nrt, bm, bn, layer=None):
    k = A.shape[1]
    d = W.shape[-1]
    mrow = functools.partial(geo["mrow_bm"], bm)
    if layer is None:
        w_spec = pl.BlockSpec((k, bn), lambda i, j: (0, j))
    else:
        w_spec = pl.BlockSpec((None, k, bn), lambda i, j: (layer, 0, j))
    if isinstance(X, tuple):
        nlt = X[0].shape[0] // bm
        x_specs = [pl.BlockSpec((bm, bn), lambda i, j: (jnp.minimum(i, nlt - 1), j)),
                   pl.BlockSpec((bm, bn), lambda i, j: (jnp.maximum(i - nlt, 0), j))]
        x_args = list(X)
    else:
        nlt = None
        x_specs = [pl.BlockSpec((bm, bn), lambda i, j: (i, j))]
        x_args = [X]
    return pl.pallas_call(
        functools.partial(_oproj_body, nlt),
        grid=(nrt, d // bn),
        in_specs=[pl.BlockSpec((bm, k), lambda i, j: (i, 0)), w_spec] + x_specs
        + [pl.BlockSpec((1, 1, bn), lambda i, j: (mrow(i), 0, j))],
        out_specs=pl.BlockSpec((bm, bn), lambda i, j: (i, j)),
        out_shape=jax.ShapeDtypeStruct((nrt * bm, d), _F32),
        compiler_params=_cp("arbitrary", "arbitrary"),
        name="oproj",
    )(A, W, *x_args, gate)


def _ffn1_body(bm, n_x_rows, seq, ctx_len, nc, rc, x0_ref, x0p_ref, x0n_ref, xn_ref, xnp_ref, xnn_ref,
               g_ref, sh_ref, sc_ref, shn_ref, scn_ref, wg_ref, wu_ref, cw_ref, cb_ref, o_ref, ha_ref, hb_ref):
    i = pl.program_id(0)
    j = pl.program_id(1)
    h_refs = (ha_ref, hb_ref)

    @pl.when((i == 0) & (j == 0))
    def _():
        g, sh, sc = g_ref[...], sh_ref[0], sc_ref[0]
        ha_ref[0:bm, :] = _norm_mod(x0_ref[...], g, sh, sc).astype(_BF)
        halo = jnp.concatenate([x0p_ref[...], x0n_ref[...]], axis=0)
        ha_ref[bm:bm + 2 * _HALO, :] = _norm_mod(halo, g, sh, sc).astype(_BF)

    for par in (0, 1):
        @pl.when(i % 2 == par)
        def _(par=par):
            h_ref, hn_ref = h_refs[par], h_refs[1 - par]
            gx = _dot(h_ref[...], wg_ref[...])
            u = _dot(h_ref[0:bm, :], wu_ref[...])
            g, shn, scn = g_ref[...], shn_ref[0], scn_ref[0]
            r0 = pl.multiple_of(jnp.minimum(j, nc - 1) * rc, rc)
            hn_ref[pl.ds(r0, rc), :] = _norm_mod(xn_ref[...], g, shn, scn).astype(_BF)
            halo = jnp.concatenate([xnp_ref[...], xnn_ref[...]], axis=0)
            hn_ref[bm:bm + 2 * _HALO, :] = _norm_mod(halo, g, shn, scn).astype(_BF)
            gm = gx[0:bm]
            g_prev = gx[bm + _HALO - 1:bm + _HALO]
            g_next = gx[bm + _HALO:bm + _HALO + 1]
            row = lax.broadcasted_iota(jnp.int32, (bm, 1), 0)
            tok = i * bm + row
            period = jnp.where(i * bm >= n_x_rows, ctx_len, seq)
            up = jnp.where(row == 0, g_prev, pltpu.roll(gm, 1, 0))
            dn = jnp.where(row == bm - 1, g_next, pltpu.roll(gm, bm - 1, 0))
            up = jnp.where((tok & (period - 1)) != 0, up, 0.0)
            dn = jnp.where(((tok + 1) & (period - 1)) != 0, dn, 0.0)
            cw = cw_ref[...]
            gc = cb_ref[...] + up * cw[0:1] + gm * cw[1:2] + dn * cw[2:3]
            o_ref[...] = (jax.nn.gelu(gc) * u).astype(o_ref.dtype)


def _ffn1(X, g, sh, sc, w_in, conv_w, conv_b, layer, geo, nrt, bf):
    t, d = X.shape
    f = conv_w.shape[2]
    bm = geo["bm"]
    mrow = geo["mrow"]
    hb = bm // _HALO
    last = t // _HALO - 1
    nf = f // bf
    nc, rc = _next_rows_chunks(bm, nf)
    nxt = lambda i: jnp.minimum(i + 1, nrt - 1)
    body = functools.partial(_ffn1_body, bm, geo["n_x_rows"], geo["seq"], geo["ctx_len"], nc, rc)
    return pl.pallas_call(
        body,
        grid=(nrt, nf),
        in_specs=[
            pl.BlockSpec((bm, d), lambda i, j: (0, 0)),
            pl.BlockSpec((_HALO, d), lambda i, j: (0, 0)),
            pl.BlockSpec((_HALO, d), lambda i, j: (jnp.minimum(hb, last), 0)),
            pl.BlockSpec((rc, d), lambda i, j: (nxt(i) * nc + jnp.minimum(j, nc - 1), 0)),
            pl.BlockSpec((_HALO, d), lambda i, j: (jnp.maximum(nxt(i) * hb - 1, 0), 0)),
            pl.BlockSpec((_HALO, d), lambda i, j: (jnp.minimum((nxt(i) + 1) * hb, last), 0)),
            pl.BlockSpec((1, d), lambda i, j: (0, 0)),
            pl.BlockSpec((1, 1, d), lambda i, j: (mrow(i), 0, 0)),
            pl.BlockSpec((1, 1, d), lambda i, j: (mrow(i), 0, 0)),
            pl.BlockSpec((1, 1, d), lambda i, j: (mrow(nxt(i)), 0, 0)),
            pl.BlockSpec((1, 1, d), lambda i, j: (mrow(nxt(i)), 0, 0)),
            pl.BlockSpec((None, d, bf), lambda i, j: (layer, 0, j)),
            pl.BlockSpec((None, d, bf), lambda i, j: (layer, 0, j + nf)),
            pl.BlockSpec((None, 3, bf), lambda i, j: (layer, 0, j)),
            pl.BlockSpec((None, 1, bf), lambda i, j: (layer, 0, j)),
        ],
        out_specs=pl.BlockSpec((bm, bf), lambda i, j: (i, j)),
        out_shape=jax.ShapeDtypeStruct((nrt * bm, f), _BF),
        scratch_shapes=[pltpu.VMEM((bm + 2 * _HALO, d), _BF), pltpu.VMEM((bm + 2 * _HALO, d), _BF)],
        compiler_params=_cp("arbitrary", "arbitrary"),
        name="ffn1",
    )(X, X, X, X, X, X, g, sh, sc, sh, sc, w_in, w_in, conv_w, conv_b)


def _with_ones(v):
    return jnp.concatenate([v, jnp.ones(v.shape, v.dtype)], axis=1)


def _lane_tiles(x, n):
    return jnp.concatenate([x] * n, axis=1) if n > 1 else x


def _na_body(seq, sub, kwin, hps, nsub, q_ref, k_ref, v_ref, kc_ref, vc_ref, ba_ref, bm_ref, bb_ref, o_ref):
    j = pl.program_id(2)
    b_refs = (ba_ref,) + (bm_ref,) * (nsub - 2) + (bb_ref,)
    for hh in range(hps):
        c = hh * _LANES
        kc = kc_ref[:, c:c + _LANES]
        vcx = _with_ones(vc_ref[:, c:c + _LANES])
        for i, b_ref in enumerate(b_refs):
            q0 = (nsub * j + i) * sub
            ks = pl.multiple_of(jnp.clip(q0 - (kwin - sub) // 2, 0, seq - kwin), 256)
            q = q_ref[i * sub:(i + 1) * sub, c:c + _LANES]
            k = k_ref[pl.ds(ks, kwin), c:c + _LANES]
            v = v_ref[pl.ds(ks, kwin), c:c + _LANES]
            s = _dot_nt(q, k) + b_ref[0, hh]
            sc = _dot_nt(q, kc)
            m = jnp.maximum(jnp.max(s, axis=-1, keepdims=True), jnp.max(sc, axis=-1, keepdims=True))
            p = jnp.exp2(s - m).astype(_BF)
            pc = jnp.exp2(sc - m).astype(_BF)
            acc = _dot(p, _with_ones(v)) + _dot(pc, vcx)
            o_ref[i * sub:(i + 1) * sub, c:c + _LANES] = (acc[:, :_LANES] / acc[:, _LANES:]).astype(o_ref.dtype)


def _na_bias_table(rel_bias, rows, rq, rk):
    h, nr2, nc2 = rel_bias.shape
    na_rows, na_cols = (nr2 + 1) // 2, (nc2 + 1) // 2
    kr = min(na_rows, rows)
    w = _GRID_W
    nblk = rows // rq
    c = np.arange(w)
    c0 = np.clip(c - na_cols // 2, 0, w - na_cols)
    col_ok = (c[None, :] >= c0[:, None]) & (c[None, :] < c0[:, None] + na_cols)
    rbp = jnp.pad(rel_bias.astype(_F32) * _LOG2E, ((0, 0), (0, 0), (w, w)))
    tcol = jnp.stack([rbp[:, :, na_cols - 1 - qc + w:na_cols - 1 - qc + 2 * w] for qc in range(w)], axis=2)
    tcol = jnp.where(jnp.asarray(col_ok)[None, None], tcol, _NEG)
    tcol = jnp.concatenate([tcol, jnp.full((h, 1, w, w), _NEG, _F32)], axis=1)
    sel = np.zeros((3, rq, rk, nr2 + 1), np.float32)
    for ti, jb in enumerate((0, min(1, nblk - 1), nblk - 1)):
        kb0 = int(np.clip(jb * rq - (rk - rq) // 2, 0, rows - rk))
        for qi in range(rq):
            r = jb * rq + qi
            r0 = int(np.clip(r - kr // 2, 0, rows - kr))
            for ki in range(rk):
                krow = kb0 + ki
                sel[ti, qi, ki, krow - r + na_rows - 1 if r0 <= krow < r0 + kr else nr2] = 1.0
    tab = jnp.einsum("tqkd,hdcx->thqckx", jnp.asarray(sel), tcol, precision=lax.Precision.HIGHEST)
    return tab.reshape(3, h, rq * w, rk * w)


def _na_attention(QKV, bias_tab, geo, n_heads):
    b, s, lc = geo["batch"], geo["seq"], geo["ctx_len"]
    t = QKV.shape[0]
    sub, kwin = bias_tab.shape[2], bias_tab.shape[3]
    nsub = min(8, s // sub)
    bq = nsub * sub
    nj = s // bq
    cblk = geo["n_x_rows"] // lc
    hps = 2
    hw = hps * _LANES
    ng = n_heads // hps
    return pl.pallas_call(
        functools.partial(_na_body, s, sub, kwin, hps, nsub),
        grid=(b, ng, nj),
        in_specs=[
            pl.BlockSpec((bq, hw), lambda bi, h, j: (bi * nj + j, h)),
            pl.BlockSpec((s, hw), lambda bi, h, j: (bi, ng + h)),
            pl.BlockSpec((s, hw), lambda bi, h, j: (bi, 2 * ng + h)),
            pl.BlockSpec((lc, hw), lambda bi, h, j: (cblk + bi, ng + h)),
            pl.BlockSpec((lc, hw), lambda bi, h, j: (cblk + bi, 2 * ng + h)),
            pl.BlockSpec((1, hps, sub, kwin), lambda bi, h, j: (jnp.where(j == 0, 0, 1), h, 0, 0)),
            pl.BlockSpec((1, hps, sub, kwin), lambda bi, h, j: (1, h, 0, 0)),
            pl.BlockSpec((1, hps, sub, kwin), lambda bi, h, j: (jnp.where(j == nj - 1, 2, 1), h, 0, 0)),
        ],
        out_specs=pl.BlockSpec((bq, hw), lambda bi, h, j: (bi * nj + j, h)),
        out_shape=jax.ShapeDtypeStruct((t, n_heads * _LANES), _BF),
        compiler_params=_cp("arbitrary", "arbitrary", "arbitrary"),
        name="na_attn",
    )(QKV, QKV, QKV, QKV, QKV, bias_tab, bias_tab, bias_tab)


def _ctx_attn_body(group, has_sink, *refs):
    if has_sink:
        q_ref, k_ref, v_ref, sk_ref, _, o_ref = refs
    else:
        q_ref, k_ref, v_ref, _, o_ref = refs
    k = k_ref[...]
    vx = _with_ones(v_ref[...])
    for g in range(group):
        c = g * _LANES
        s = _dot_nt(q_ref[:, c:c + _LANES], k)
        m = jnp.max(s, axis=-1, keepdims=True)
        if has_sink:
            sk = sk_ref[0][:, c:c + 1]
            m = jnp.maximum(m, sk)
        acc = _dot(jnp.exp2(s - m).astype(_BF), vx)
        l = acc[:, _LANES:]
        if has_sink:
            l = l + jnp.exp2(sk - m)
        o_ref[:, c:c + _LANES] = (acc[:, :_LANES] / l).astype(o_ref.dtype)


def _ctx_attention(QKV, O, geo, n_heads, n_kv, sink=None):
    b, lc = geo["batch"], geo["ctx_len"]
    group = n_heads // n_kv
    gw = group * _LANES
    cblk = geo["n_x_rows"] // lc
    in_specs = [
        pl.BlockSpec((lc, gw), lambda bi, n: (cblk + bi, n)),
        pl.BlockSpec((lc, _LANES), lambda bi, n: (cblk + bi, n_heads + n)),
        pl.BlockSpec((lc, _LANES), lambda bi, n: (cblk + bi, n_heads + n_kv + n)),
    ]
    args = [QKV, QKV, QKV]
    if sink is not None:
        in_specs.append(pl.BlockSpec((1, 1, gw), lambda bi, n: (n, 0, 0)))
        args.append(sink)
    in_specs.append(pl.BlockSpec(memory_space=pl.ANY))
    args.append(O)
    return pl.pallas_call(
        functools.partial(_ctx_attn_body, group, sink is not None),
        grid=(b, n_kv),
        in_specs=in_specs,
        out_specs=pl.BlockSpec((lc, gw), lambda bi, n: (cblk + bi, n)),
        out_shape=jax.ShapeDtypeStruct(O.shape, O.dtype),
        input_output_aliases={len(args) - 1: 0},
        compiler_params=_cp("arbitrary", "arbitrary"),
        name="ctx_attn",
    )(*args)


def _swa_body(seq, bq, nblk, win, group, q_ref, k_ref, v_ref, kc_ref, vc_ref, sk_ref, o_ref):
    kw = bq + 2 * win
    kc = kc_ref[...]
    vcx = _with_ones(vc_ref[...])
    for bi in range(nblk):
        t = pl.program_id(2) * nblk + bi
        rows = slice(bi * bq, (bi + 1) * bq)
        ks = pl.multiple_of(jnp.clip(t * bq - win, 0, seq - kw), _LANES)
        k = k_ref[pl.ds(ks, kw), :]
        vx = _with_ones(v_ref[pl.ds(ks, kw), :])
        qpos = t * bq + lax.broadcasted_iota(jnp.int32, (bq, 1), 0)
        kpos = ks + lax.broadcasted_iota(jnp.int32, (1, kw), 1)
        band = jnp.abs(kpos - qpos) <= win
        for g in range(group):
            c = g * _LANES
            q = q_ref[rows, c:c + _LANES]
            s = jnp.where(band, _dot_nt(q, k), _NEG)
            sc = _dot_nt(q, kc)
            sk = sk_ref[0][:, c:c + 1]
            m = jnp.maximum(jnp.maximum(jnp.max(s, axis=-1, keepdims=True), jnp.max(sc, axis=-1, keepdims=True)), sk)
            p = jnp.exp2(s - m).astype(_BF)
            pc = jnp.exp2(sc - m).astype(_BF)
            acc = _dot(p, vx) + _dot(pc, vcx)
            l = acc[:, _LANES:] + jnp.exp2(sk - m)
            o_ref[rows, c:c + _LANES] = (acc[:, :_LANES] / l).astype(o_ref.dtype)


def _swa_attention(QKV, sink, geo, n_heads, n_kv, bq):
    b, s, lc = geo["batch"], geo["seq"], geo["ctx_len"]
    t = QKV.shape[0]
    group = n_heads // n_kv
    gw = group * _LANES
    nblk = 2
    bs = nblk * bq
    nq = s // bs
    cblk = geo["n_x_rows"] // lc
    return pl.pallas_call(
        functools.partial(_swa_body, s, bq, nblk, _SWA_WINDOW, group),
        grid=(b, n_kv, nq),
        in_specs=[
            pl.BlockSpec((bs, gw), lambda bi, n, j: (bi * nq + j, n)),
            pl.BlockSpec((s, _LANES), lambda bi, n, j: (bi, n_heads + n)),
            pl.BlockSpec((s, _LANES), lambda bi, n, j: (bi, n_heads + n_kv + n)),
            pl.BlockSpec((lc, _LANES), lambda bi, n, j: (cblk + bi, n_heads + n)),
            pl.BlockSpec((lc, _LANES), lambda bi, n, j: (cblk + bi, n_heads + n_kv + n)),
            pl.BlockSpec((1, 1, gw), lambda bi, n, j: (n, 0, 0)),
        ],
        out_specs=pl.BlockSpec((bs, gw), lambda bi, n, j: (bi * nq + j, n)),
        out_shape=jax.ShapeDtypeStruct((t, n_heads * _LANES), _BF),
        compiler_params=_cp("arbitrary", "arbitrary", "arbitrary"),
        name="swa_attn",
    )(QKV, QKV, QKV, QKV, QKV, sink)


def _gqa_body(seq, ck, group, q_ref, k_ref, v_ref, kc_ref, vc_ref, o_ref, m_ref, acc_ref):
    m_ref[...] = jnp.full(m_ref.shape, _NEG, _F32)
    acc_ref[...] = jnp.zeros(acc_ref.shape, _F32)

    def step(k, v):
        width = k.shape[0]
        vx = jnp.concatenate([v, jnp.ones((width, _LANES), v.dtype)], axis=1)
        for g in range(group):
            s = _dot_nt(q_ref[:, g * _LANES:(g + 1) * _LANES], k)
            m_old = m_ref[g]
            m_new = jnp.maximum(m_old, jnp.max(s, axis=-1, keepdims=True))
            alpha = jnp.exp2(m_old - m_new)
            p = jnp.exp2(s - _lane_tiles(m_new, width // _LANES)).astype(_BF)
            acc_ref[g] = _lane_tiles(alpha, 2) * acc_ref[g] + _dot(p, vx)
            m_ref[g] = m_new

    def chunk(c, carry):
        off = pl.multiple_of(c * ck, ck)
        step(k_ref[pl.ds(off, ck), :], v_ref[pl.ds(off, ck), :])
        return carry

    lax.fori_loop(0, seq // ck, chunk, 0, unroll=True)
    step(kc_ref[...], vc_ref[...])
    for g in range(group):
        acc = acc_ref[g]
        o_ref[:, g * _LANES:(g + 1) * _LANES] = (acc[:, :_LANES] / acc[:, _LANES:]).astype(o_ref.dtype)


def _gqa_attention(QKV, geo, n_heads, n_kv, bq, ck):
    b, s, lc = geo["batch"], geo["seq"], geo["ctx_len"]
    group = n_heads // n_kv
    gw = group * _LANES
    nq = s // bq
    cblk = geo["n_x_rows"] // lc
    return pl.pallas_call(
        functools.partial(_gqa_body, s, ck, group),
        grid=(b, n_kv, nq),
        in_specs=[
            pl.BlockSpec((bq, gw), lambda bi, n, j: (bi * nq + j, n)),
            pl.BlockSpec((s, _LANES), lambda bi, n, j: (bi, n_heads + n)),
            pl.BlockSpec((s, _LANES), lambda bi, n, j: (bi, n_heads + n_kv + n)),
            pl.BlockSpec((lc, _LANES), lambda bi, n, j: (cblk + bi, n_heads + n)),
            pl.BlockSpec((lc, _LANES), lambda bi, n, j: (cblk + bi, n_heads + n_kv + n)),
        ],
        out_specs=pl.BlockSpec((bq, gw), lambda bi, n, j: (bi * nq + j, n)),
        out_shape=jax.ShapeDtypeStruct((geo["n_x_rows"], n_heads * _LANES), _BF),
        scratch_shapes=[pltpu.VMEM((group, bq, _LANES), _F32), pltpu.VMEM((group, bq, 2 * _LANES), _F32)],
        compiler_params=_cp("arbitrary", "arbitrary", "arbitrary"),
        name="gqa_attn",
    )(QKV, QKV, QKV, QKV, QKV)


def _log_sigmoid(x):
    return jnp.minimum(x, 0.0) - jnp.log1p(jnp.exp(-jnp.abs(x)))


def _mlstm_chunk(d, st, n_heads, L, h, q_ref, k_ref, v_ref, gt_ref, gb_ref, o_ref, c_ref, n_ref, m_ref):
    gates = gt_ref[...] + gb_ref[...]
    lane = lax.broadcasted_iota(jnp.int32, (1, _LANES), 1)
    li_col = jnp.sum(jnp.where(lane == (2 * d) * n_heads + h, gates, 0.0), axis=-1, keepdims=True)
    lf_pre = jnp.sum(jnp.where(lane == (2 * d + 1) * n_heads + h, gates, 0.0), axis=-1, keepdims=True)
    lf_col = _log_sigmoid(lf_pre)
    ti = lax.broadcasted_iota(jnp.int32, (L, L), 0)
    si = lax.broadcasted_iota(jnp.int32, (L, L), 1)
    eye = ti == si
    li_row = jnp.sum(jnp.where(eye, li_col, 0.0), axis=0, keepdims=True)
    lf_row = jnp.sum(jnp.where(eye, lf_col, 0.0), axis=0, keepdims=True)
    allowed = (ti >= si) if d == 0 else (ti <= si)
    allowed_t = (si >= ti) if d == 0 else (si <= ti)
    b_col = jnp.sum(jnp.where(allowed, lf_row, 0.0), axis=-1, keepdims=True)
    b_row = jnp.sum(jnp.where(allowed_t, lf_col, 0.0), axis=0, keepdims=True)
    total = jnp.sum(lf_col, axis=0, keepdims=True)
    m_prev = m_ref[st, 0:1, 0:1]

    q = q_ref[...]
    k = k_ref[...]
    v = v_ref[...]
    dmat = jnp.where(allowed, b_col - b_row + li_row, _NEG)
    g_col = b_col + m_prev
    m_t = jnp.maximum(g_col, jnp.max(dmat, axis=-1, keepdims=True))
    w = jnp.exp(dmat - m_t) * _dot_nt(q, k)
    w_prev = jnp.exp(g_col - m_t)
    cmat = c_ref[st]
    nvec = n_ref[st, 0:1, :]
    num = _dot(w.astype(_BF), v) + w_prev * _dot(q, cmat.astype(_BF))
    qn = jnp.sum(q.astype(_F32) * nvec, axis=-1, keepdims=True)
    den = jnp.sum(w, axis=-1, keepdims=True) + w_prev * qn
    o_ref[...] = (num / jnp.maximum(jnp.abs(den), jnp.exp(-m_t))).astype(o_ref.dtype)

    lw = total - b_col + li_col
    m_new = jnp.maximum(total + m_prev, jnp.max(lw, axis=0, keepdims=True))
    decay = jnp.exp(total + m_prev - m_new)
    kw = k.astype(_F32) * jnp.exp(lw - m_new)
    c_ref[st] = decay * cmat + _dot_tn(kw.astype(_BF), v)
    n_ref[st] = jnp.broadcast_to(decay * nvec + jnp.sum(kw, axis=0, keepdims=True), n_ref.shape[1:])
    m_ref[st] = jnp.broadcast_to(m_new, m_ref.shape[1:])


def _mlstm_body(n_heads, hps, dqk, dv, chunk, qf_ref, kf_ref, vf_ref, gf_ref, qb_ref, kb_ref, vb_ref, gbk_ref,
                gb_ref, of_ref, ob_ref, c_ref, n_ref, m_ref):
    hg = pl.program_id(1)

    @pl.when(pl.program_id(2) == 0)
    def _():
        c_ref[...] = jnp.zeros(c_ref.shape, _F32)
        n_ref[...] = jnp.zeros(n_ref.shape, _F32)
        m_ref[...] = jnp.zeros(m_ref.shape, _F32)

    dirs = ((qf_ref, kf_ref, vf_ref, gf_ref, of_ref), (qb_ref, kb_ref, vb_ref, gbk_ref, ob_ref))
    for hh in range(hps):
        qs, vs = slice(hh * dqk, (hh + 1) * dqk), slice(hh * dv, (hh + 1) * dv)
        for d, (q_ref, k_ref, v_ref, g_ref, o_ref) in enumerate(dirs):
            _mlstm_chunk(d, d * hps + hh, n_heads, chunk, hg * hps + hh, q_ref.at[:, qs], k_ref.at[:, qs],
                         v_ref.at[:, vs], g_ref, gb_ref, o_ref.at[:, vs], c_ref, n_ref, m_ref)


def _mlstm_scan(P, G, gate_b, geo, n_heads, dqk, dv, chunk):
    b, s, lc = geo["batch"], geo["seq"], geo["ctx_len"]
    t = P.shape[0]
    nctx, nx = lc // chunk, s // chunk
    cbase = geo["n_x_rows"] // chunk

    def rblk(d, bi, c):
        cc = c - nctx
        in_ctx = cbase + bi * nctx + (c if d == 0 else nctx - 1 - c)
        in_x = bi * nx + (cc if d == 0 else nx - 1 - cc)
        return jnp.where(c < nctx, in_ctx, in_x)

    hps = 4
    ng = n_heads // hps
    k_off = ng
    v_off = 2 * n_heads * dqk // (hps * dv)
    in_specs, args = [], []
    for d in (0, 1):
        in_specs += [
            pl.BlockSpec((chunk, hps * dqk), lambda bi, h, c, d=d: (rblk(d, bi, c), h)),
            pl.BlockSpec((chunk, hps * dqk), lambda bi, h, c, d=d: (rblk(d, bi, c), k_off + h)),
            pl.BlockSpec((chunk, hps * dv), lambda bi, h, c, d=d: (rblk(d, bi, c), v_off + h)),
            pl.BlockSpec((chunk, _LANES), lambda bi, h, c, d=d: (rblk(d, bi, c), 0)),
        ]
        args += [P, P, P, G]
    in_specs.append(pl.BlockSpec((1, _LANES), lambda bi, h, c: (0, 0)))
    args.append(gate_b)
    out_sds = jax.ShapeDtypeStruct((t, n_heads * dv), _BF)
    ns = 2 * hps
    return pl.pallas_call(
        functools.partial(_mlstm_body, n_heads, hps, dqk, dv, chunk),
        grid=(b, ng, nctx + nx),
        in_specs=in_specs,
        out_specs=[pl.BlockSpec((chunk, hps * dv), lambda bi, h, c, d=d: (rblk(d, bi, c), h)) for d in (0, 1)],
        out_shape=[out_sds, out_sds],
        scratch_shapes=[pltpu.VMEM((ns, dqk, dv), _F32), pltpu.VMEM((ns, 8, dqk), _F32), pltpu.VMEM((ns, 8, _LANES), _F32)],
        compiler_params=_cp("arbitrary", "arbitrary", "arbitrary"),
        name="mlstm_scan",
    )(*args)


def _ml_out_body(n_heads, dv, hf_ref, hb_ref, og_ref, hg_ref, w_ref, x_ref, gt_ref, o_ref):
    hsum = hf_ref[...].astype(_F32) + hb_ref[...].astype(_F32)
    parts = []
    for hh in range(n_heads):
        a = hsum[:, hh * dv:(hh + 1) * dv]
        hn = a * lax.rsqrt(jnp.mean(a * a, axis=-1, keepdims=True) + _EPS) * hg_ref[:, hh * dv:(hh + 1) * dv]
        og = og_ref[:, hh * dv:(hh + 1) * dv].astype(_F32)
        parts.append((jax.nn.sigmoid(og) * hn).astype(_BF))
    a = jnp.concatenate(parts, axis=-1)
    o_ref[...] = x_ref[...] + gt_ref[0] * _dot(a, w_ref[...])


def _ml_out(Hf, Hb, P, head_g, W, X, gate, geo, n_heads, dv, bm):
    t, d = X.shape
    kd = n_heads * dv
    o_off = (P.shape[1] - kd) // kd
    mrow = functools.partial(geo["mrow_bm"], bm)
    return pl.pallas_call(
        functools.partial(_ml_out_body, n_heads, dv),
        grid=(t // bm,),
        in_specs=[
            pl.BlockSpec((bm, kd), lambda i: (i, 0)),
            pl.BlockSpec((bm, kd), lambda i: (i, 0)),
            pl.BlockSpec((bm, kd), lambda i: (i, o_off)),
            pl.BlockSpec((1, kd), lambda i: (0, 0)),
            pl.BlockSpec((kd, d), lambda i: (0, 0)),
            pl.BlockSpec((bm, d), lambda i: (i, 0)),
            pl.BlockSpec((1, 1, d), lambda i: (mrow(i), 0, 0)),
        ],
        out_specs=pl.BlockSpec((bm, d), lambda i: (i, 0)),
        out_shape=jax.ShapeDtypeStruct((t, d), _F32),
        compiler_params=_cp("arbitrary"),
        name="ml_out",
    )(Hf, Hb, P, head_g, W, X, gate)


def _rope_tables(seq, pad_rows):
    tpos = np.arange(seq)
    row = (tpos // _GRID_W).astype(np.float32)
    col = (tpos % _GRID_W).astype(np.float32)
    nf = _LANES // 4
    inv = (_ROPE_BASE ** (-jnp.arange(nf, dtype=_F32) / nf))
    ar = jnp.asarray(row)[:, None] * inv
    ac = jnp.asarray(col)[:, None] * inv
    cr, sr, cc, sc = jnp.cos(ar), jnp.sin(ar), jnp.cos(ac), jnp.sin(ac)
    cs = jnp.concatenate([cr, cc, cr, cc], axis=1)
    sn = jnp.concatenate([-sr, -sc, sr, sc], axis=1)
    cs = jnp.concatenate([cs, jnp.ones((pad_rows, _LANES), _F32)], axis=0)
    sn = jnp.concatenate([sn, jnp.zeros((pad_rows, _LANES), _F32)], axis=0)
    return cs, sn


def _pair_major(v):
    q = _LANES // 4
    return v.reshape(v.shape[:-1] + (v.shape[-1] // _LANES, 2, 2, q)).swapaxes(-3, -2).reshape(v.shape)


def _rope_operands(w_qkv, q_gain, k_gain, n_qk_heads, cs, sn):
    nqk = n_qk_heads * _LANES
    w = jnp.concatenate([_pair_major(w_qkv[:, :nqk]), w_qkv[:, nqk:]], axis=1).astype(_BF)
    tabs = []
    for gain in (q_gain, k_gain):
        gp = _pair_major(gain.astype(_F32))
        tabs += [cs * gp[None, :], sn * jnp.roll(gp, _LANES // 2)[None, :]]
    return w, tuple(tabs)


def _geometry(batch, seq, ctx_len):
    bm = min(1024, batch * ctx_len)
    assert seq % bm == 0 and (batch * ctx_len) % bm == 0
    assert seq & (seq - 1) == 0 and ctx_len & (ctx_len - 1) == 0
    n_x_rows = batch * seq

    def mrow_bm(bm_, i):
        return jnp.minimum((i * bm_) // seq, batch)

    return dict(batch=batch, seq=seq, ctx_len=ctx_len, bm=bm, n_x_rows=n_x_rows,
                n_x_tiles=n_x_rows // bm, tiles_per_seq=seq // bm,
                mrow=functools.partial(mrow_bm, bm), mrow_bm=mrow_bm)


def kernel(x, c, ctx, c_ctx, ada_w, ada_b, norm1_g, norm2_g, ffn_w_in, ffn_conv_w, ffn_conv_b, ffn_w_out,
           na_w_qkv, na_q_g, na_k_g, na_rel_bias, na_w_o,
           swa_w_qkv, swa_q_g, swa_k_g, swa_sinks, swa_w_o,
           ml_w_in, ml_gate_b, ml_head_g, ml_w_o,
           gqa_w_qkv, gqa_q_g, gqa_k_g, gqa_w_o):
    batch, seq, d = x.shape
    ctx_len = ctx.shape[1]
    depth = ada_w.shape[0]
    n_heads = d // _LANES
    geo = _geometry(batch, seq, ctx_len)
    bm = geo["bm"]
    n_x_rows = geo["n_x_rows"]
    n_x_tiles = geo["n_x_tiles"]
    qscale = (_LANES ** -0.5) * _LOG2E

    X = (x.reshape(n_x_rows, d), ctx.reshape(batch * ctx_len, d))
    t_all = n_x_rows + batch * ctx_len
    nrt_all = t_all // bm

    cond = jnp.concatenate([c, c_ctx[None, :], jnp.zeros((8 - batch - 1, d), _F32)], axis=0)
    mods = _adaln(cond, ada_w, ada_b)

    cs, sn = _rope_tables(seq, bm)
    f = ffn_conv_w.shape[2]
    w_in_b = ffn_w_in.astype(_BF)
    w_out_b = ffn_w_out.astype(_BF)
    conv_b3 = ffn_conv_b.reshape(depth, 1, f)
    bf = 512
    bm_o = min(512, bm)

    for i in range(depth):
        kind, jl = i % 4, i // 4
        need_ctx = i < depth - 1
        mod = [mods[i, :batch + 1, k * d:(k + 1) * d].reshape(batch + 1, 1, d) for k in range(6)]
        sh1, sc1, g1, sh2, sc2, g2 = mod
        n1 = norm1_g[i].reshape(1, d)
        n2 = norm2_g[i].reshape(1, d)
        nrt_o = t_all // bm_o if need_ctx else n_x_rows // bm_o

        if kind == 0:
            w = na_w_qkv[jl].astype(_BF)
            gq = (na_q_g[jl] * qscale).reshape(1, _LANES)
            gk = na_k_g[jl].reshape(1, _LANES)
            bn = _wide_tile(w.shape[1])
            QKV = _proj(X, n1, sh1, sc1, w, _qkv_layouts(n_heads, n_heads, bn, False), geo, bn, _BF, gains=(gq, gk))
            rows = seq // _GRID_W
            tab = _na_bias_table(na_rel_bias[jl], rows, 4, 12)
            O = _na_attention(QKV, tab, geo, n_heads)
            if need_ctx:
                O = _ctx_attention(QKV, O, geo, n_heads, n_heads)
            X = _oproj(O, na_w_o[jl].astype(_BF), X, g1, geo, nrt_o, bm_o, d)
        elif kind == 1:
            n_kv = (swa_w_qkv.shape[2] // _LANES - n_heads) // 2
            w, tabs = _rope_operands(swa_w_qkv[jl], swa_q_g[jl] * qscale, swa_k_g[jl], n_heads + n_kv, cs, sn)
            bn = _wide_tile(w.shape[1])
            QKV = _proj(X, n1, sh1, sc1, w, _qkv_layouts(n_heads, n_kv, bn, True), geo, bn, _BF, rope=tabs)
            group = n_heads // n_kv
            sink = jnp.repeat(swa_sinks[jl].astype(_F32) * _LOG2E, _LANES).reshape(n_kv, 1, group * _LANES)
            O = _swa_attention(QKV, sink, geo, n_heads, n_kv, 256)
            if need_ctx:
                O = _ctx_attention(QKV, O, geo, n_heads, n_kv, sink)
            X = _oproj(O, swa_w_o[jl].astype(_BF), X, g1, geo, nrt_o, bm_o, d)
        elif kind == 2:
            mh = ml_gate_b.shape[1] // 4
            dv = d // mh
            dqk = dv // 2
            nmain = 2 * mh * dqk + 2 * mh * dv
            w_main = ml_w_in[jl][:, :nmain].astype(_BF)
            w_gate = jnp.pad(ml_w_in[jl][:, nmain:], ((0, 0), (0, _LANES - 4 * mh))).astype(_BF)
            bn = _wide_tile(nmain)
            segs_by_tile = []
            k0, k1 = mh * dqk, 2 * mh * dqk
            for jt in range(nmain // bn):
                lo, hi = jt * bn, (jt + 1) * bn
                cuts = sorted({lo, hi, min(max(k0, lo), hi), min(max(k1, lo), hi)})
                segs = tuple((a - lo, b_ - lo, "plain", 0, False, dqk ** -0.5 if k0 <= a < k1 else 1.0)
                             for a, b_ in zip(cuts[:-1], cuts[1:]))
                segs_by_tile.append((jt, jt + 1, segs))
            P = _proj(X, n1, sh1, sc1, w_main, tuple(segs_by_tile), geo, bn, _BF)
            G = _proj(X, n1, sh1, sc1, w_gate, ((0, 1, ((0, _LANES, "plain", 0, False, 1.0),)),), geo, _LANES, _F32)
            gb = jnp.pad(ml_gate_b[jl].astype(_F32), (0, _LANES - 4 * mh)).reshape(1, _LANES)
            Hf, Hb = _mlstm_scan(P, G, gb, geo, mh, dqk, dv, 256)
            X = _ml_out(Hf, Hb, P, ml_head_g[jl].reshape(1, mh * dv), ml_w_o[jl].astype(_BF), X, g1, geo, mh, dv, bm_o)
        else:
            n_kv = (gqa_w_qkv.shape[2] // _LANES - n_heads) // 2
            w, tabs = _rope_operands(gqa_w_qkv[jl], gqa_q_g[jl] * qscale, gqa_k_g[jl], n_heads + n_kv, cs, sn)
            bn = _wide_tile(w.shape[1])
            QKV = _proj(X, n1, sh1, sc1, w, _qkv_layouts(n_heads, n_kv, bn, True), geo, bn, _BF, rope=tabs)
            O = _gqa_attention(QKV, geo, n_heads, n_kv, min(1024, seq), 512)
            if need_ctx:
                Oc = jnp.zeros((t_all, d), _BF).at[:n_x_rows].set(O)
                O = _ctx_attention(QKV, Oc, geo, n_heads, n_kv)
            X = _oproj(O, gqa_w_o[jl].astype(_BF), X, g1, geo, nrt_o, bm_o, d)

        nrt_f = nrt_all if need_ctx else n_x_tiles
        A = _ffn1(X, n2, sh2, sc2, w_in_b, ffn_conv_w, conv_b3, i, geo, nrt_f, bf)
        X = _oproj(A, w_out_b, X, g2, geo, A.shape[0] // bm, bm, 512, layer=i)

    return X[:n_x_rows].reshape(batch, seq, d)
```

```python
import functools

import numpy as np
import jax
import jax.numpy as jnp
from jax import lax
from jax.experimental import pallas as pl
from jax.experimental.pallas import tpu as pltpu

_F32 = jnp.float32
_BF = jnp.bfloat16
_EPS = 1e-6
_NEG = -1e30
_LOG2E = 1.4426950408889634
_GRID_W = 64
_SWA_WINDOW = 128
_ROPE_BASE = 10000.0
_LANES = 128
_HALO = 8
_VMEM_LIMIT = 56 << 20


def _cp(*sem):
    return pltpu.CompilerParams(dimension_semantics=sem, vmem_limit_bytes=_VMEM_LIMIT)


def _dot(a, b):
    return jnp.dot(a, b, preferred_element_type=_F32)


def _dot_nt(a, b):
    return lax.dot_general(a, b, (((1,), (1,)), ((), ())), preferred_element_type=_F32)


def _dot_tn(a, b):
    return lax.dot_general(a, b, (((0,), (0,)), ((), ())), preferred_element_type=_F32)


def _norm_mod(x, g, sh, sc):
    ms = jnp.mean(x * x, axis=-1, keepdims=True)
    return (x * lax.rsqrt(ms + _EPS)) * (g * (1.0 + sc)) + sh


def _adaln_body(c_ref, w_ref, b_ref, o_ref):
    c = c_ref[...]
    s = (c * jax.nn.sigmoid(c)).astype(_BF)
    o_ref[0] = _dot(s, w_ref[0].astype(_BF)) + b_ref[0]


def _adaln(cond, ada_w, ada_b):
    depth, d, n = ada_w.shape
    bn = 1024
    return pl.pallas_call(
        _adaln_body,
        grid=(depth, n // bn),
        in_specs=[
            pl.BlockSpec((8, d), lambda l, j: (0, 0)),
            pl.BlockSpec((1, d, bn), lambda l, j: (l, 0, j)),
            pl.BlockSpec((1, 1, bn), lambda l, j: (l, 0, j)),
        ],
        out_specs=pl.BlockSpec((1, 8, bn), lambda l, j: (l, 0, j)),
        out_shape=jax.ShapeDtypeStruct((depth, 8, n), _F32),
        compiler_params=_cp("arbitrary", "arbitrary"),
        name="adaln",
    )(cond, ada_w, ada_b.reshape(depth, 1, n))


def _next_rows_chunks(bm, nj):
    nc = 1
    while nc * 2 <= nj and bm // (nc * 2) >= _LANES:
        nc *= 2
    return nc, bm // nc


def _proj_body(layouts, has_rope, n_gain, n_lat_tiles, *refs):
    x_refs = refs[:1] if n_lat_tiles is None else refs[:2]
    refs = refs[len(x_refs):]
    g_ref, sh_ref, sc_ref, w_ref = refs[:4]
    pos = 4
    tab_refs = ()
    if has_rope:
        tab_refs = refs[pos:pos + 4]
        pos += 4
    gain_refs = refs[pos:pos + n_gain]
    pos += n_gain
    o_ref, h_ref = refs[pos], refs[pos + 1]
    j = pl.program_id(1)

    is_lat = True if n_lat_tiles is None else pl.program_id(0) < n_lat_tiles
    for src_ref, mine in zip(x_refs, (is_lat, jnp.logical_not(is_lat))):
        @pl.when((j == 0) & mine)
        def _(src_ref=src_ref):
            h_ref[...] = _norm_mod(src_ref[...], g_ref[...], sh_ref[0], sc_ref[0]).astype(_BF)

    for lo, hi, segs in layouts:
        @pl.when((j >= lo) & (j < hi))
        def _(segs=segs):
            acc = _dot(h_ref[...], w_ref[...])
            for c0, c1, kind, gi, rope, mult in segs:
                if kind == "plain":
                    a = acc[:, c0:c1]
                    if mult != 1.0:
                        a = a * mult
                    o_ref[:, c0:c1] = a.astype(o_ref.dtype)
                    continue
                for c in range(c0, c1, _LANES):
                    a = acc[:, c:c + _LANES]
                    inv = lax.rsqrt(jnp.mean(a * a, axis=-1, keepdims=True) + _EPS)
                    if rope:
                        y = a * tab_refs[2 * gi][...] + pltpu.roll(a, _LANES // 2, 1) * tab_refs[2 * gi + 1][...]
                    else:
                        y = a * gain_refs[gi][...]
                    o_ref[:, c:c + _LANES] = (y * inv).astype(o_ref.dtype)


def _proj(X, g, sh, sc, W, layouts, geo, bn, out_dtype, rope=None, gains=()):
    n = W.shape[1]
    bm = geo["bm"]
    nj = n // bn
    mrow = geo["mrow"]
    if isinstance(X, tuple):
        nlt = X[0].shape[0] // bm
        t, d = X[0].shape[0] + X[1].shape[0], X[0].shape[1]
        x_specs = [pl.BlockSpec((bm, d), lambda i, j: (jnp.minimum(i, nlt - 1), 0)),
                   pl.BlockSpec((bm, d), lambda i, j: (jnp.maximum(i - nlt, 0), 0),
                                pipeline_mode=pl.Buffered(1))]
        x_args = list(X)
    else:
        nlt = None
        t, d = X.shape
        x_specs = [pl.BlockSpec((bm, d), lambda i, j: (i, 0))]
        x_args = [X]
    nrt = t // bm
    in_specs = x_specs + [
        pl.BlockSpec((1, d), lambda i, j: (0, 0)),
        pl.BlockSpec((1, 1, d), lambda i, j: (mrow(i), 0, 0)),
        pl.BlockSpec((1, 1, d), lambda i, j: (mrow(i), 0, 0)),
        pl.BlockSpec((d, bn), lambda i, j: (0, j)),
    ]
    args = x_args + [g, sh, sc, W]
    if rope is not None:
        tps, nxt = geo["tiles_per_seq"], geo["n_x_tiles"]
        tab = lambda i, j: (jnp.where(i < nxt, i % tps, tps), 0)
        in_specs += [pl.BlockSpec((bm, _LANES), tab)] * 4
        args += list(rope)
    for gn in gains:
        in_specs.append(pl.BlockSpec((1, _LANES), lambda i, j: (0, 0)))
        args.append(gn)
    return pl.pallas_call(
        functools.partial(_proj_body, layouts, rope is not None, len(gains), nlt),
        grid=(nrt, nj),
        in_specs=in_specs,
        out_specs=pl.BlockSpec((bm, bn), lambda i, j: (i, j)),
        out_shape=jax.ShapeDtypeStruct((t, n), out_dtype),
        scratch_shapes=[pltpu.VMEM((bm, d), _BF)],
        compiler_params=_cp("arbitrary", "arbitrary"),
        name="proj",
    )(*args)


def _wide_tile(n, cap=1536):
    return max(b for b in range(_LANES, cap + 1, _LANES) if n % b == 0)


def _qkv_layouts(n_q, n_kv, bn, rope):
    hd = _LANES
    bounds = [(0, n_q * hd, "q"), (n_q * hd, (n_q + n_kv) * hd, "k"), ((n_q + n_kv) * hd, (n_q + 2 * n_kv) * hd, "v")]
    n = (n_q + 2 * n_kv) * hd
    per_tile = []
    for jt in range(n // bn):
        lo, hi = jt * bn, (jt + 1) * bn
        segs = []
        for b0, b1, nm in bounds:
            s0, s1 = max(lo, b0), min(hi, b1)
            if s0 < s1:
                if nm == "v":
                    segs.append((s0 - lo, s1 - lo, "plain", 0, False, 1.0))
                else:
                    segs.append((s0 - lo, s1 - lo, "head", 0 if nm == "q" else 1, rope, 1.0))
        per_tile.append(tuple(segs))
    layouts = []
    for jt, segs in enumerate(per_tile):
        if layouts and layouts[-1][2] == segs and layouts[-1][1] == jt:
            layouts[-1] = (layouts[-1][0], jt + 1, segs)
        else:
            layouts.append((jt, jt + 1, segs))
    return tuple(layouts)


def _oproj_body(n_lat_tiles, a_ref, w_ref, *refs):
    x_refs, (gt_ref, o_ref) = refs[:-2], refs[-2:]
    x = x_refs[0][...]
    if n_lat_tiles is not None:
        x = jnp.where(pl.program_id(0) < n_lat_tiles, x, x_refs[1][...])
    o_ref[...] = x + gt_ref[0] * _dot(a_ref[...], w_ref[...])


def _oproj(A, W, X, gate, geo, nrt, bm, bn, layer=None):
    k = A.shape[1]
    d = W.shape[-1]
    mrow = functools.partial(geo["mrow_bm"], bm)
    if layer is None:
        w_spec = pl.BlockSpec((k, bn), lambda i, j: (0, j))
    else:
        w_spec = pl.BlockSpec((None, k, bn), lambda i, j: (layer, 0, j))
    if isinstance(X, tuple):
        nlt = X[0].shape[0] // bm
        x_specs = [pl.BlockSpec((bm, bn), lambda i, j: (jnp.minimum(i, nlt - 1), j)),
                   pl.BlockSpec((bm, bn), lambda i, j: (jnp.maximum(i - nlt, 0), j))]
        x_args = list(X)
    else:
        nlt = None
        x_specs = [pl.BlockSpec((bm, bn), lambda i, j: (i, j))]
        x_args = [X]
    return pl.pallas_call(
        functools.partial(_oproj_body, nlt),
        grid=(nrt, d // bn),
        in_specs=[pl.BlockSpec((bm, k), lambda i, j: (i, 0)), w_spec] + x_specs
        + [pl.BlockSpec((1, 1, bn), lambda i, j: (mrow(i), 0, j))],
        out_specs=pl.BlockSpec((bm, bn), lambda i, j: (i, j)),
        out_shape=jax.ShapeDtypeStruct((nrt * bm, d), _F32),
        compiler_params=_cp("arbitrary", "arbitrary"),
        name="oproj",
    )(A, W, *x_args, gate)


def _ffn1_body(bm, n_x_rows, seq, ctx_len, nc, rc, x0_ref, x0p_ref, x0n_ref, xn_ref, xnp_ref, xnn_ref,
               g_ref, sh_ref, sc_ref, shn_ref, scn_ref, wg_ref, wu_ref, cw_ref, cb_ref, o_ref, ha_ref, hb_ref):
    i = pl.program_id(0)
    j = pl.program_id(1)
    h_refs = (ha_ref, hb_ref)

    @pl.when((i == 0) & (j == 0))
    def _():
        g, sh, sc = g_ref[...], sh_ref[0], sc_ref[0]
        ha_ref[0:bm, :] = _norm_mod(x0_ref[...], g, sh, sc).astype(_BF)
        halo = jnp.concatenate([x0p_ref[...], x0n_ref[...]], axis=0)
        ha_ref[bm:bm + 2 * _HALO, :] = _norm_mod(halo, g, sh, sc).astype(_BF)

    for par in (0, 1):
        @pl.when(i % 2 == par)
        def _(par=par):
            h_ref, hn_ref = h_refs[par], h_refs[1 - par]
            gx = _dot(h_ref[...], wg_ref[...])
            u = _dot(h_ref[0:bm, :], wu_ref[...])
            g, shn, scn = g_ref[...], shn_ref[0], scn_ref[0]
            r0 = pl.multiple_of(jnp.minimum(j, nc - 1) * rc, rc)
            hn_ref[pl.ds(r0, rc), :] = _norm_mod(xn_ref[...], g, shn, scn).astype(_BF)
            halo = jnp.concatenate([xnp_ref[...], xnn_ref[...]], axis=0)
            hn_ref[bm:bm + 2 * _HALO, :] = _norm_mod(halo, g, shn, scn).astype(_BF)
            gm = gx[0:bm]
            g_prev = gx[bm + _HALO - 1:bm + _HALO]
            g_next = gx[bm + _HALO:bm + _HALO + 1]
            row = lax.broadcasted_iota(jnp.int32, (bm, 1), 0)
            tok = i * bm + row
            period = jnp.where(i * bm >= n_x_rows, ctx_len, seq)
            up = jnp.where(row == 0, g_prev, pltpu.roll(gm, 1, 0))
            dn = jnp.where(row == bm - 1, g_next, pltpu.roll(gm, bm - 1, 0))
            up = jnp.where((tok & (period - 1)) != 0, up, 0.0)
            dn = jnp.where(((tok + 1) & (period - 1)) != 0, dn, 0.0)
            cw = cw_ref[...]
            gc = cb_ref[...] + up * cw[0:1] + gm * cw[1:2] + dn * cw[2:3]
            o_ref[...] = (jax.nn.gelu(gc) * u).astype(o_ref.dtype)


def _ffn1(X, g, sh, sc, w_in, conv_w, conv_b, layer, geo, nrt, bf):
    t, d = X.shape
    f = conv_w.shape[2]
    bm = geo["bm"]
    mrow = geo["mrow"]
    hb = bm // _HALO
    last = t // _HALO - 1
    nf = f // bf
    nc, rc = _next_rows_chunks(bm, nf)
    nxt = lambda i: jnp.minimum(i + 1, nrt - 1)
    body = functools.partial(_ffn1_body, bm, geo["n_x_rows"], geo["seq"], geo["ctx_len"], nc, rc)
    return pl.pallas_call(
        body,
        grid=(nrt, nf),
        in_specs=[
            pl.BlockSpec((bm, d), lambda i, j: (0, 0)),
            pl.BlockSpec((_HALO, d), lambda i, j: (0, 0)),
            pl.BlockSpec((_HALO, d), lambda i, j: (jnp.minimum(hb, last), 0)),
            pl.BlockSpec((rc, d), lambda i, j: (nxt(i) * nc + jnp.minimum(j, nc - 1), 0)),
            pl.BlockSpec((_HALO, d), lambda i, j: (jnp.maximum(nxt(i) * hb - 1, 0), 0)),
            pl.BlockSpec((_HALO, d), lambda i, j: (jnp.minimum((nxt(i) + 1) * hb, last), 0)),
            pl.BlockSpec((1, d), lambda i, j: (0, 0)),
            pl.BlockSpec((1, 1, d), lambda i, j: (mrow(i), 0, 0)),
            pl.BlockSpec((1, 1, d), lambda i, j: (mrow(i), 0, 0)),
            pl.BlockSpec((1, 1, d), lambda i, j: (mrow(nxt(i)), 0, 0)),
            pl.BlockSpec((1, 1, d), lambda i, j: (mrow(nxt(i)), 0, 0)),
            pl.BlockSpec((None, d, bf), lambda i, j: (layer, 0, j)),
            pl.BlockSpec((None, d, bf), lambda i, j: (layer, 0, j + nf)),
            pl.BlockSpec((None, 3, bf), lambda i, j: (layer, 0, j)),
            pl.BlockSpec((None, 1, bf), lambda i, j: (layer, 0, j)),
        ],
        out_specs=pl.BlockSpec((bm, bf), lambda i, j: (i, j)),
        out_shape=jax.ShapeDtypeStruct((nrt * bm, f), _BF),
        scratch_shapes=[pltpu.VMEM((bm + 2 * _HALO, d), _BF), pltpu.VMEM((bm + 2 * _HALO, d), _BF)],
        compiler_params=_cp("arbitrary", "arbitrary"),
        name="ffn1",
    )(X, X, X, X, X, X, g, sh, sc, sh, sc, w_in, w_in, conv_w, conv_b)


def _with_ones(v):
    return jnp.concatenate([v, jnp.ones(v.shape, v.dtype)], axis=1)


def _lane_tiles(x, n):
    return jnp.concatenate([x] * n, axis=1) if n > 1 else x


def _na_body(seq, sub, kwin, hps, nsub, q_ref, k_ref, v_ref, kc_ref, vc_ref, ba_ref, bm_ref, bb_ref, o_ref):
    j = pl.program_id(2)
    b_refs = (ba_ref,) + (bm_ref,) * (nsub - 2) + (bb_ref,)
    for hh in range(hps):
        c = hh * _LANES
        kc = kc_ref[:, c:c + _LANES]
        vcx = _with_ones(vc_ref[:, c:c + _LANES])
        for i, b_ref in enumerate(b_refs):
            q0 = (nsub * j + i) * sub
            ks = pl.multiple_of(jnp.clip(q0 - (kwin - sub) // 2, 0, seq - kwin), 256)
            q = q_ref[i * sub:(i + 1) * sub, c:c + _LANES]
            k = k_ref[pl.ds(ks, kwin), c:c + _LANES]
            v = v_ref[pl.ds(ks, kwin), c:c + _LANES]
            s = _dot_nt(q, k) + b_ref[0, hh]
            sc = _dot_nt(q, kc)
            m = jnp.maximum(jnp.max(s, axis=-1, keepdims=True), jnp.max(sc, axis=-1, keepdims=True))
            p = jnp.exp2(s - m).astype(_BF)
            pc = jnp.exp2(sc - m).astype(_BF)
            acc = _dot(p, _with_ones(v)) + _dot(pc, vcx)
            o_ref[i * sub:(i + 1) * sub, c:c + _LANES] = (acc[:, :_LANES] / acc[:, _LANES:]).astype(o_ref.dtype)


def _na_bias_table(rel_bias, rows, rq, rk):
    h, nr2, nc2 = rel_bias.shape
    na_rows, na_cols = (nr2 + 1) // 2, (nc2 + 1) // 2
    kr = min(na_rows, rows)
    w = _GRID_W
    nblk = rows // rq
    c = np.arange(w)
    c0 = np.clip(c - na_cols // 2, 0, w - na_cols)
    col_ok = (c[None, :] >= c0[:, None]) & (c[None, :] < c0[:, None] + na_cols)
    rbp = jnp.pad(rel_bias.astype(_F32) * _LOG2E, ((0, 0), (0, 0), (w, w)))
    tcol = jnp.stack([rbp[:, :, na_cols - 1 - qc + w:na_cols - 1 - qc + 2 * w] for qc in range(w)], axis=2)
    tcol = jnp.where(jnp.asarray(col_ok)[None, None], tcol, _NEG)
    tcol = jnp.concatenate([tcol, jnp.full((h, 1, w, w), _NEG, _F32)], axis=1)
    sel = np.zeros((3, rq, rk, nr2 + 1), np.float32)
    for ti, jb in enumerate((0, min(1, nblk - 1), nblk - 1)):
        kb0 = int(np.clip(jb * rq - (rk - rq) // 2, 0, rows - rk))
        for qi in range(rq):
            r = jb * rq + qi
            r0 = int(np.clip(r - kr // 2, 0, rows - kr))
            for ki in range(rk):
                krow = kb0 + ki
                sel[ti, qi, ki, krow - r + na_rows - 1 if r0 <= krow < r0 + kr else nr2] = 1.0
    tab = jnp.einsum("tqkd,hdcx->thqckx", jnp.asarray(sel), tcol, precision=lax.Precision.HIGHEST)
    return tab.reshape(3, h, rq * w, rk * w)


def _na_attention(QKV, bias_tab, geo, n_heads):
    b, s, lc = geo["batch"], geo["seq"], geo["ctx_len"]
    t = QKV.shape[0]
    sub, kwin = bias_tab.shape[2], bias_tab.shape[3]
    nsub = min(8, s // sub)
    bq = nsub * sub
    nj = s // bq
    cblk = geo["n_x_rows"] // lc
    hps = 2
    hw = hps * _LANES
    ng = n_heads // hps
    return pl.pallas_call(
        functools.partial(_na_body, s, sub, kwin, hps, nsub),
        grid=(b, ng, nj),
        in_specs=[
            pl.BlockSpec((bq, hw), lambda bi, h, j: (bi * nj + j, h)),
            pl.BlockSpec((s, hw), lambda bi, h, j: (bi, ng + h)),
            pl.BlockSpec((s, hw), lambda bi, h, j: (bi, 2 * ng + h)),
            pl.BlockSpec((lc, hw), lambda bi, h, j: (cblk + bi, ng + h)),
            pl.BlockSpec((lc, hw), lambda bi, h, j: (cblk + bi, 2 * ng + h)),
            pl.BlockSpec((1, hps, sub, kwin), lambda bi, h, j: (jnp.where(j == 0, 0, 1), h, 0, 0)),
            pl.BlockSpec((1, hps, sub, kwin), lambda bi, h, j: (1, h, 0, 0)),
            pl.BlockSpec((1, hps, sub, kwin), lambda bi, h, j: (jnp.where(j == nj - 1, 2, 1), h, 0, 0)),
        ],
        out_specs=pl.BlockSpec((bq, hw), lambda bi, h, j: (bi * nj + j, h)),
        out_shape=jax.ShapeDtypeStruct((t, n_heads * _LANES), _BF),
        compiler_params=_cp("arbitrary", "arbitrary", "arbitrary"),
        name="na_attn",
    )(QKV, QKV, QKV, QKV, QKV, bias_tab, bias_tab, bias_tab)


def _ctx_attn_body(group, has_sink, *refs):
    if has_sink:
        q_ref, k_ref, v_ref, sk_ref, _, o_ref = refs
    else:
        q_ref, k_ref, v_ref, _, o_ref = refs
    k = k_ref[...]
    vx = _with_ones(v_ref[...])
    for g in range(group):
        c = g * _LANES
        s = _dot_nt(q_ref[:, c:c + _LANES], k)
        m = jnp.max(s, axis=-1, keepdims=True)
        if has_sink:
            sk = sk_ref[0][:, c:c + 1]
            m = jnp.maximum(m, sk)
        acc = _dot(jnp.exp2(s - m).astype(_BF), vx)
        l = acc[:, _LANES:]
        if has_sink:
            l = l + jnp.exp2(sk - m)
        o_ref[:, c:c + _LANES] = (acc[:, :_LANES] / l).astype(o_ref.dtype)


def _ctx_attention(QKV, O, geo, n_heads, n_kv, sink=None):
    b, lc = geo["batch"], geo["ctx_len"]
    group = n_heads // n_kv
    gw = group * _LANES
    cblk = geo["n_x_rows"] // lc
    in_specs = [
        pl.BlockSpec((lc, gw), lambda bi, n: (cblk + bi, n)),
        pl.BlockSpec((lc, _LANES), lambda bi, n: (cblk + bi, n_heads + n)),
        pl.BlockSpec((lc, _LANES), lambda bi, n: (cblk + bi, n_heads + n_kv + n)),
    ]
    args = [QKV, QKV, QKV]
    if sink is not None:
        in_specs.append(pl.BlockSpec((1, 1, gw), lambda bi, n: (n, 0, 0)))
        args.append(sink)
    in_specs.append(pl.BlockSpec(memory_space=pl.ANY))
    args.append(O)
    return pl.pallas_call(
        functools.partial(_ctx_attn_body, group, sink is not None),
        grid=(b, n_kv),
        in_specs=in_specs,
        out_specs=pl.BlockSpec((lc, gw), lambda bi, n: (cblk + bi, n)),
        out_shape=jax.ShapeDtypeStruct(O.shape, O.dtype),
        input_output_aliases={len(args) - 1: 0},
        compiler_params=_cp("arbitrary", "arbitrary"),
        name="ctx_attn",
    )(*args)


def _swa_body(seq, bq, nblk, win, group, q_ref, k_ref, v_ref, kc_ref, vc_ref, sk_ref, o_ref):
    kw = bq + 2 * win
    kc = kc_ref[...]
    vcx = _with_ones(vc_ref[...])
    for bi in range(nblk):
        t = pl.program_id(2) * nblk + bi
        rows = slice(bi * bq, (bi + 1) * bq)
        ks = pl.multiple_of(jnp.clip(t * bq - win, 0, seq - kw), _LANES)
        k = k_ref[pl.ds(ks, kw), :]
        vx = _with_ones(v_ref[pl.ds(ks, kw), :])
        qpos = t * bq + lax.broadcasted_iota(jnp.int32, (bq, 1), 0)
        kpos = ks + lax.broadcasted_iota(jnp.int32, (1, kw), 1)
        band = jnp.abs(kpos - qpos) <= win
        for g in range(group):
            c = g * _LANES
            q = q_ref[rows, c:c + _LANES]
            s = jnp.where(band, _dot_nt(q, k), _NEG)
            sc = _dot_nt(q, kc)
            sk = sk_ref[0][:, c:c + 1]
            m = jnp.maximum(jnp.maximum(jnp.max(s, axis=-1, keepdims=True), jnp.max(sc, axis=-1, keepdims=True)), sk)
            p = jnp.exp2(s - m).astype(_BF)
            pc = jnp.exp2(sc - m).astype(_BF)
            acc = _dot(p, vx) + _dot(pc, vcx)
            l = acc[:, _LANES:] + jnp.exp2(sk - m)
            o_ref[rows, c:c + _LANES] = (acc[:, :_LANES] / l).astype(o_ref.dtype)


def _swa_attention(QKV, sink, geo, n_heads, n_kv, bq):
    b, s, lc = geo["batch"], geo["seq"], geo["ctx_len"]
    t = QKV.shape[0]
    group = n_heads // n_kv
    gw = group * _LANES
    nblk = 2
    bs = nblk * bq
    nq = s // bs
    cblk = geo["n_x_rows"] // lc
    return pl.pallas_call(
        functools.partial(_swa_body, s, bq, nblk, _SWA_WINDOW, group),
        grid=(b, n_kv, nq),
        in_specs=[
            pl.BlockSpec((bs, gw), lambda bi, n, j: (bi * nq + j, n)),
            pl.BlockSpec((s, _LANES), lambda bi, n, j: (bi, n_heads + n)),
            pl.BlockSpec((s, _LANES), lambda bi, n, j: (bi, n_heads + n_kv + n)),
            pl.BlockSpec((lc, _LANES), lambda bi, n, j: (cblk + bi, n_heads + n)),
            pl.BlockSpec((lc, _LANES), lambda bi, n, j: (cblk + bi, n_heads + n_kv + n)),
            pl.BlockSpec((1, 1, gw), lambda bi, n, j: (n, 0, 0)),
        ],
        out_specs=pl.BlockSpec((bs, gw), lambda bi, n, j: (bi * nq + j, n)),
        out_shape=jax.ShapeDtypeStruct((t, n_heads * _LANES), _BF),
        compiler_params=_cp("arbitrary", "arbitrary", "arbitrary"),
        name="swa_attn",
    )(QKV, QKV, QKV, QKV, QKV, sink)


def _gqa_body(seq, ck, group, q_ref, k_ref, v_ref, kc_ref, vc_ref, o_ref, m_ref, acc_ref):
    m_ref[...] = jnp.full(m_ref.shape, _NEG, _F32)
    acc_ref[...] = jnp.zeros(acc_ref.shape, _F32)

    def step(k, v):
        width = k.shape[0]
        vx = jnp.concatenate([v, jnp.ones((width, _LANES), v.dtype)], axis=1)
        for g in range(group):
            s = _dot_nt(q_ref[:, g * _LANES:(g + 1) * _LANES], k)
            m_old = m_ref[g]
            m_new = jnp.maximum(m_old, jnp.max(s, axis=-1, keepdims=True))
            alpha = jnp.exp2(m_old - m_new)
            p = jnp.exp2(s - _lane_tiles(m_new, width // _LANES)).astype(_BF)
            acc_ref[g] = _lane_tiles(alpha, 2) * acc_ref[g] + _dot(p, vx)
            m_ref[g] = m_new

    def chunk(c, carry):
        off = pl.multiple_of(c * ck, ck)
        step(k_ref[pl.ds(off, ck), :], v_ref[pl.ds(off, ck), :])
        return carry

    lax.fori_loop(0, seq // ck, chunk, 0, unroll=min(8, seq // ck))
    step(kc_ref[...], vc_ref[...])
    for g in range(group):
        acc = acc_ref[g]
        o_ref[:, g * _LANES:(g + 1) * _LANES] = (acc[:, :_LANES] / acc[:, _LANES:]).astype(o_ref.dtype)


def _gqa_attention(QKV, geo, n_heads, n_kv, bq, ck):
    b, s, lc = geo["batch"], geo["seq"], geo["ctx_len"]
    group = n_heads // n_kv
    gw = group * _LANES
    nq = s // bq
    cblk = geo["n_x_rows"] // lc
    return pl.pallas_call(
        functools.partial(_gqa_body, s, ck, group),
        grid=(b, n_kv, nq),
        in_specs=[
            pl.BlockSpec((bq, gw), lambda bi, n, j: (bi * nq + j, n)),
            pl.BlockSpec((s, _LANES), lambda bi, n, j: (bi, n_heads + n)),
            pl.BlockSpec((s, _LANES), lambda bi, n, j: (bi, n_heads + n_kv + n)),
            pl.BlockSpec((lc, _LANES), lambda bi, n, j: (cblk + bi, n_heads + n)),
            pl.BlockSpec((lc, _LANES), lambda bi, n, j: (cblk + bi, n_heads + n_kv + n)),
        ],
        out_specs=pl.BlockSpec((bq, gw), lambda bi, n, j: (bi * nq + j, n)),
        out_shape=jax.ShapeDtypeStruct((geo["n_x_rows"], n_heads * _LANES), _BF),
        scratch_shapes=[pltpu.VMEM((group, bq, _LANES), _F32), pltpu.VMEM((group, bq, 2 * _LANES), _F32)],
        compiler_params=_cp("arbitrary", "arbitrary", "arbitrary"),
        name="gqa_attn",
    )(QKV, QKV, QKV, QKV, QKV)


def _log_sigmoid(x):
    return jnp.minimum(x, 0.0) - jnp.log1p(jnp.exp(-jnp.abs(x)))


def _mlstm_chunk(d, st, n_heads, L, h, q_ref, k_ref, v_ref, gt_ref, gb_ref, o_ref, c_ref, n_ref, m_ref):
    gates = gt_ref[...] + gb_ref[...]
    lane = lax.broadcasted_iota(jnp.int32, (1, _LANES), 1)
    li_col = jnp.sum(jnp.where(lane == (2 * d) * n_heads + h, gates, 0.0), axis=-1, keepdims=True)
    lf_pre = jnp.sum(jnp.where(lane == (2 * d + 1) * n_heads + h, gates, 0.0), axis=-1, keepdims=True)
    lf_col = _log_sigmoid(lf_pre)
    ti = lax.broadcasted_iota(jnp.int32, (L, L), 0)
    si = lax.broadcasted_iota(jnp.int32, (L, L), 1)
    eye = ti == si
    li_row = jnp.sum(jnp.where(eye, li_col, 0.0), axis=0, keepdims=True)
    lf_row = jnp.sum(jnp.where(eye, lf_col, 0.0), axis=0, keepdims=True)
    allowed = (ti >= si) if d == 0 else (ti <= si)
    allowed_t = (si >= ti) if d == 0 else (si <= ti)
    b_col = jnp.sum(jnp.where(allowed, lf_row, 0.0), axis=-1, keepdims=True)
    b_row = jnp.sum(jnp.where(allowed_t, lf_col, 0.0), axis=0, keepdims=True)
    total = jnp.sum(lf_col, axis=0, keepdims=True)
    m_prev = m_ref[st, 0:1, 0:1]

    q = q_ref[...]
    k = k_ref[...]
    v = v_ref[...]
    dmat = jnp.where(allowed, b_col - b_row + li_row, _NEG)
    g_col = b_col + m_prev
    m_t = jnp.maximum(g_col, jnp.max(dmat, axis=-1, keepdims=True))
    w = jnp.exp(dmat - m_t) * _dot_nt(q, k)
    w_prev = jnp.exp(g_col - m_t)
    cmat = c_ref[st]
    nvec = n_ref[st, 0:1, :]
    num = _dot(w.astype(_BF), v) + w_prev * _dot(q, cmat.astype(_BF))
    qn = jnp.sum(q.astype(_F32) * nvec, axis=-1, keepdims=True)
    den = jnp.sum(w, axis=-1, keepdims=True) + w_prev * qn
    o_ref[...] = (num / jnp.maximum(jnp.abs(den), jnp.exp(-m_t))).astype(o_ref.dtype)

    lw = total - b_col + li_col
    m_new = jnp.maximum(total + m_prev, jnp.max(lw, axis=0, keepdims=True))
    decay = jnp.exp(total + m_prev - m_new)
    kw = k.astype(_F32) * jnp.exp(lw - m_new)
    c_ref[st] = decay * cmat + _dot_tn(kw.astype(_BF), v)
    n_ref[st] = jnp.broadcast_to(decay * nvec + jnp.sum(kw, axis=0, keepdims=True), n_ref.shape[1:])
    m_ref[st] = jnp.broadcast_to(m_new, m_ref.shape[1:])


def _mlstm_body(n_heads, hps, dqk, dv, chunk, qf_ref, kf_ref, vf_ref, gf_ref, qb_ref, kb_ref, vb_ref, gbk_ref,
                gb_ref, of_ref, ob_ref, c_ref, n_ref, m_ref):
    hg = pl.program_id(1)

    @pl.when(pl.program_id(2) == 0)
    def _():
        c_ref[...] = jnp.zeros(c_ref.shape, _F32)
        n_ref[...] = jnp.zeros(n_ref.shape, _F32)
        m_ref[...] = jnp.zeros(m_ref.shape, _F32)

    dirs = ((qf_ref, kf_ref, vf_ref, gf_ref, of_ref), (qb_ref, kb_ref, vb_ref, gbk_ref, ob_ref))
    for hh in range(hps):
        qs, vs = slice(hh * dqk, (hh + 1) * dqk), slice(hh * dv, (hh + 1) * dv)
        for d, (q_ref, k_ref, v_ref, g_ref, o_ref) in enumerate(dirs):
            _mlstm_chunk(d, d * hps + hh, n_heads, chunk, hg * hps + hh, q_ref.at[:, qs], k_ref.at[:, qs],
                         v_ref.at[:, vs], g_ref, gb_ref, o_ref.at[:, vs], c_ref, n_ref, m_ref)


def _mlstm_scan(P, G, gate_b, geo, n_heads, dqk, dv, chunk):
    b, s, lc = geo["batch"], geo["seq"], geo["ctx_len"]
    t = P.shape[0]
    nctx, nx = lc // chunk, s // chunk
    cbase = geo["n_x_rows"] // chunk

    def rblk(d, bi, c):
        cc = c - nctx
        in_ctx = cbase + bi * nctx + (c if d == 0 else nctx - 1 - c)
        in_x = bi * nx + (cc if d == 0 else nx - 1 - cc)
        return jnp.where(c < nctx, in_ctx, in_x)

    hps = 4
    ng = n_heads // hps
    k_off = ng
    v_off = 2 * n_heads * dqk // (hps * dv)
    in_specs, args = [], []
    for d in (0, 1):
        in_specs += [
            pl.BlockSpec((chunk, hps * dqk), lambda bi, h, c, d=d: (rblk(d, bi, c), h)),
            pl.BlockSpec((chunk, hps * dqk), lambda bi, h, c, d=d: (rblk(d, bi, c), k_off + h)),
            pl.BlockSpec((chunk, hps * dv), lambda bi, h, c, d=d: (rblk(d, bi, c), v_off + h)),
            pl.BlockSpec((chunk, _LANES), lambda bi, h, c, d=d: (rblk(d, bi, c), 0)),
        ]
        args += [P, P, P, G]
    in_specs.append(pl.BlockSpec((1, _LANES), lambda bi, h, c: (0, 0)))
    args.append(gate_b)
    out_sds = jax.ShapeDtypeStruct((t, n_heads * dv), _BF)
    ns = 2 * hps
    return pl.pallas_call(
        functools.partial(_mlstm_body, n_heads, hps, dqk, dv, chunk),
        grid=(b, ng, nctx + nx),
        in_specs=in_specs,
        out_specs=[pl.BlockSpec((chunk, hps * dv), lambda bi, h, c, d=d: (rblk(d, bi, c), h)) for d in (0, 1)],
        out_shape=[out_sds, out_sds],
        scratch_shapes=[pltpu.VMEM((ns, dqk, dv), _F32), pltpu.VMEM((ns, 8, dqk), _F32), pltpu.VMEM((ns, 8, _LANES), _F32)],
        compiler_params=_cp("arbitrary", "arbitrary", "arbitrary"),
        name="mlstm_scan",
    )(*args)


def _ml_out_body(n_heads, dv, hf_ref, hb_ref, og_ref, hg_ref, w_ref, x_ref, gt_ref, o_ref):
    hsum = hf_ref[...].astype(_F32) + hb_ref[...].astype(_F32)
    parts = []
    for hh in range(n_heads):
        a = hsum[:, hh * dv:(hh + 1) * dv]
        hn = a * lax.rsqrt(jnp.mean(a * a, axis=-1, keepdims=True) + _EPS) * hg_ref[:, hh * dv:(hh + 1) * dv]
        og = og_ref[:, hh * dv:(hh + 1) * dv].astype(_F32)
        parts.append((jax.nn.sigmoid(og) * hn).astype(_BF))
    a = jnp.concatenate(parts, axis=-1)
    o_ref[...] = x_ref[...] + gt_ref[0] * _dot(a, w_ref[...])


def _ml_out(Hf, Hb, P, head_g, W, X, gate, geo, n_heads, dv, bm):
    t, d = X.shape
    kd = n_heads * dv
    o_off = (P.shape[1] - kd) // kd
    mrow = functools.partial(geo["mrow_bm"], bm)
    return pl.pallas_call(
        functools.partial(_ml_out_body, n_heads, dv),
        grid=(t // bm,),
        in_specs=[
            pl.BlockSpec((bm, kd), lambda i: (i, 0)),
            pl.BlockSpec((bm, kd), lambda i: (i, 0)),
            pl.BlockSpec((bm, kd), lambda i: (i, o_off)),
            pl.BlockSpec((1, kd), lambda i: (0, 0)),
            pl.BlockSpec((kd, d), lambda i: (0, 0)),
            pl.BlockSpec((bm, d), lambda i: (i, 0)),
            pl.BlockSpec((1, 1, d), lambda i: (mrow(i), 0, 0)),
        ],
        out_specs=pl.BlockSpec((bm, d), lambda i: (i, 0)),
        out_shape=jax.ShapeDtypeStruct((t, d), _F32),
        compiler_params=_cp("arbitrary"),
        name="ml_out",
    )(Hf, Hb, P, head_g, W, X, gate)


def _rope_tables(seq, pad_rows):
    tpos = np.arange(seq)
    row = (tpos // _GRID_W).astype(np.float32)
    col = (tpos % _GRID_W).astype(np.float32)
    nf = _LANES // 4
    inv = (_ROPE_BASE ** (-jnp.arange(nf, dtype=_F32) / nf))
    ar = jnp.asarray(row)[:, None] * inv
    ac = jnp.asarray(col)[:, None] * inv
    cr, sr, cc, sc = jnp.cos(ar), jnp.sin(ar), jnp.cos(ac), jnp.sin(ac)
    cs = jnp.concatenate([cr, cc, cr, cc], axis=1)
    sn = jnp.concatenate([-sr, -sc, sr, sc], axis=1)
    cs = jnp.concatenate([cs, jnp.ones((pad_rows, _LANES), _F32)], axis=0)
    sn = jnp.concatenate([sn, jnp.zeros((pad_rows, _LANES), _F32)], axis=0)
    return cs, sn


def _pair_major(v):
    q = _LANES // 4
    return v.reshape(v.shape[:-1] + (v.shape[-1] // _LANES, 2, 2, q)).swapaxes(-3, -2).reshape(v.shape)


def _rope_operands(w_qkv, q_gain, k_gain, n_qk_heads, cs, sn):
    nqk = n_qk_heads * _LANES
    w = jnp.concatenate([_pair_major(w_qkv[:, :nqk]), w_qkv[:, nqk:]], axis=1).astype(_BF)
    tabs = []
    for gain in (q_gain, k_gain):
        gp = _pair_major(gain.astype(_F32))
        tabs += [cs * gp[None, :], sn * jnp.roll(gp, _LANES // 2)[None, :]]
    return w, tuple(tabs)


def _geometry(batch, seq, ctx_len):
    bm = min(1024, batch * ctx_len)
    assert seq % bm == 0 and (batch * ctx_len) % bm == 0
    assert seq & (seq - 1) == 0 and ctx_len & (ctx_len - 1) == 0
    n_x_rows = batch * seq

    def mrow_bm(bm_, i):
        return jnp.minimum((i * bm_) // seq, batch)

    return dict(batch=batch, seq=seq, ctx_len=ctx_len, bm=bm, n_x_rows=n_x_rows,
                n_x_tiles=n_x_rows // bm, tiles_per_seq=seq // bm,
                mrow=functools.partial(mrow_bm, bm), mrow_bm=mrow_bm)


def kernel(x, c, ctx, c_ctx, ada_w, ada_b, norm1_g, norm2_g, ffn_w_in, ffn_conv_w, ffn_conv_b, ffn_w_out,
           na_w_qkv, na_q_g, na_k_g, na_rel_bias, na_w_o,
           swa_w_qkv, swa_q_g, swa_k_g, swa_sinks, swa_w_o,
           ml_w_in, ml_gate_b, ml_head_g, ml_w_o,
           gqa_w_qkv, gqa_q_g, gqa_k_g, gqa_w_o):
    batch, seq, d = x.shape
    ctx_len = ctx.shape[1]
    depth = ada_w.shape[0]
    n_heads = d // _LANES
    geo = _geometry(batch, seq, ctx_len)
    bm = geo["bm"]
    n_x_rows = geo["n_x_rows"]
    n_x_tiles = geo["n_x_tiles"]
    qscale = (_LANES ** -0.5) * _LOG2E

    X = (x.reshape(n_x_rows, d), ctx.reshape(batch * ctx_len, d))
    t_all = n_x_rows + batch * ctx_len
    nrt_all = t_all // bm

    cond = jnp.concatenate([c, c_ctx[None, :], jnp.zeros((8 - batch - 1, d), _F32)], axis=0)
    mods = _adaln(cond, ada_w, ada_b)

    cs, sn = _rope_tables(seq, bm)
    f = ffn_conv_w.shape[2]
    w_in_b = ffn_w_in.astype(_BF)
    w_out_b = ffn_w_out.astype(_BF)
    conv_b3 = ffn_conv_b.reshape(depth, 1, f)
    bf = 512
    bm_o = min(512, bm)

    for i in range(depth):
        kind, jl = i % 4, i // 4
        need_ctx = i < depth - 1
        mod = [mods[i, :batch + 1, k * d:(k + 1) * d].reshape(batch + 1, 1, d) for k in range(6)]
        sh1, sc1, g1, sh2, sc2, g2 = mod
        n1 = norm1_g[i].reshape(1, d)
        n2 = norm2_g[i].reshape(1, d)
        nrt_o = t_all // bm_o if need_ctx else n_x_rows // bm_o

        if kind == 0:
            w = na_w_qkv[jl].astype(_BF)
            gq = (na_q_g[jl] * qscale).reshape(1, _LANES)
            gk = na_k_g[jl].reshape(1, _LANES)
            bn = _wide_tile(w.shape[1])
            QKV = _proj(X, n1, sh1, sc1, w, _qkv_layouts(n_heads, n_heads, bn, False), geo, bn, _BF, gains=(gq, gk))
            rows = seq // _GRID_W
            tab = _na_bias_table(na_rel_bias[jl], rows, 4, 12)
            O = _na_attention(QKV, tab, geo, n_heads)
            if need_ctx:
                O = _ctx_attention(QKV, O, geo, n_heads, n_heads)
            X = _oproj(O, na_w_o[jl].astype(_BF), X, g1, geo, nrt_o, bm_o, d)
        elif kind == 1:
            n_kv = (swa_w_qkv.shape[2] // _LANES - n_heads) // 2
            w, tabs = _rope_operands(swa_w_qkv[jl], swa_q_g[jl] * qscale, swa_k_g[jl], n_heads + n_kv, cs, sn)
            bn = _wide_tile(w.shape[1])
            QKV = _proj(X, n1, sh1, sc1, w, _qkv_layouts(n_heads, n_kv, bn, True), geo, bn, _BF, rope=tabs)
            group = n_heads // n_kv
            sink = jnp.repeat(swa_sinks[jl].astype(_F32) * _LOG2E, _LANES).reshape(n_kv, 1, group * _LANES)
            O = _swa_attention(QKV, sink, geo, n_heads, n_kv, 256)
            if need_ctx:
                O = _ctx_attention(QKV, O, geo, n_heads, n_kv, sink)
            X = _oproj(O, swa_w_o[jl].astype(_BF), X, g1, geo, nrt_o, bm_o, d)
        elif kind == 2:
            mh = ml_gate_b.shape[1] // 4
            dv = d // mh
            dqk = dv // 2
            nmain = 2 * mh * dqk + 2 * mh * dv
            w_main = ml_w_in[jl][:, :nmain].astype(_BF)
            w_gate = jnp.pad(ml_w_in[jl][:, nmain:], ((0, 0), (0, _LANES - 4 * mh))).astype(_BF)
            bn = _wide_tile(nmain)
            segs_by_tile = []
            k0, k1 = mh * dqk, 2 * mh * dqk
            for jt in range(nmain // bn):
                lo, hi = jt * bn, (jt + 1) * bn
                cuts = sorted({lo, hi, min(max(k0, lo), hi), min(max(k1, lo), hi)})
                segs = tuple((a - lo, b_ - lo, "plain", 0, False, dqk ** -0.5 if k0 <= a < k1 else 1.0)
                             for a, b_ in zip(cuts[:-1], cuts[1:]))
                segs_by_tile.append((jt, jt + 1, segs))
            P = _proj(X, n1, sh1, sc1, w_main, tuple(segs_by_tile), geo, bn, _BF)
            G = _proj(X, n1, sh1, sc1, w_gate, ((0, 1, ((0, _LANES, "plain", 0, False, 1.0),)),), geo, _LANES, _F32)
            gb = jnp.pad(ml_gate_b[jl].astype(_F32), (0, _LANES - 4 * mh)).reshape(1, _LANES)
            Hf, Hb = _mlstm_scan(P, G, gb, geo, mh, dqk, dv, 256)
            X = _ml_out(Hf, Hb, P, ml_head_g[jl].reshape(1, mh * dv), ml_w_o[jl].astype(_BF), X, g1, geo, mh, dv, bm_o)
        else:
            n_kv = (gqa_w_qkv.shape[2] // _LANES - n_heads) // 2
            w, tabs = _rope_operands(gqa_w_qkv[jl], gqa_q_g[jl] * qscale, gqa_k_g[jl], n_heads + n_kv, cs, sn)
            bn = _wide_tile(w.shape[1])
            QKV = _proj(X, n1, sh1, sc1, w, _qkv_layouts(n_heads, n_kv, bn, True), geo, bn, _BF, rope=tabs)
            O = _gqa_attention(QKV, geo, n_heads, n_kv, min(1024, seq), 512)
            if need_ctx:
                Oc = jnp.zeros((t_all, d), _BF).at[:n_x_rows].set(O)
                O = _ctx_attention(QKV, Oc, geo, n_heads, n_kv)
            X = _oproj(O, gqa_w_o[jl].astype(_BF), X, g1, geo, nrt_o, bm_o, d)

        nrt_f = nrt_all if need_ctx else n_x_tiles
        A = _ffn1(X, n2, sh2, sc2, w_in_b, ffn_conv_w, conv_b3, i, geo, nrt_f, bf)
        X = _oproj(A, w_out_b, X, g2, geo, A.shape[0] // bm, bm, 512, layer=i)

    return X[:n_x_rows].reshape(batch, seq, d)
```
